```python
import math
import jax, jax.numpy as jnp
from jax import lax
import numpy as np

D_MODEL = 1024
BATCH = 16
SEQ = 4096
DEPTH = 4

N_MIXERS = 2
N_DIFF_LAYERS = (DEPTH + 1) // 2
N_RET_LAYERS = DEPTH // 2

DIFF_HEADS = 8
DIFF_HEAD_DIM = D_MODEL // DIFF_HEADS // 2
DIFF_V_DIM = 2 * DIFF_HEAD_DIM
DIFF_Q_BLOCK = 128

RET_HEADS = 4
RET_QK_DIM = D_MODEL // RET_HEADS
RET_V_DIM = 2 * RET_QK_DIM
RET_VALUE_WIDTH = RET_HEADS * RET_V_DIM
RET_CHUNK = 128

N_EXPERTS = 32
TOP_K = 4
EXPERT_FF = D_MODEL
SWIGLU_ALPHA = 1.702
SWIGLU_LIMIT = 7.0

NORM_EPS = 1e-5

kernel_name = "hybrid_diffattn_retention_moe_encoder"


def rms_norm(x, gain=None, eps=NORM_EPS):
    xf = x.astype(jnp.float32)
    y = xf * lax.rsqrt(jnp.mean(xf * xf, axis=-1, keepdims=True) + eps)
    if gain is not None:
        y = y * gain.astype(jnp.float32)
    return y.astype(x.dtype)


def alibi_slopes(n_heads):
    return jnp.exp2(-8.0 * jnp.arange(1, n_heads + 1, dtype=jnp.float32) / n_heads)


def lambda_init_fn(layer_idx):
    return 0.8 - 0.6 * math.exp(-0.3 * layer_idx)


def diff_attention(h, w_in, w_out, lq1, lk1, lq2, lk2, q_gain, k_gain, subln_gain, lambda_init):
    B_, S_, _ = h.shape
    H, dh, QB = DIFF_HEADS, DIFF_HEAD_DIM, DIFF_Q_BLOCK
    qkv = h @ w_in
    q, k, v = jnp.split(qkv, 3, axis=-1)
    q = rms_norm(q.reshape(B_, S_, H, 2, dh), q_gain) * (dh ** -0.5)
    k = rms_norm(k.reshape(B_, S_, H, 2, dh), k_gain)
    v = v.reshape(B_, S_, H, DIFF_V_DIM)
    lam = (jnp.exp(jnp.sum(lq1.astype(jnp.float32) * lk1.astype(jnp.float32)))
           - jnp.exp(jnp.sum(lq2.astype(jnp.float32) * lk2.astype(jnp.float32)))
           + lambda_init)
    slopes = alibi_slopes(H)[:, None, None, None]
    key_pos = jnp.arange(S_, dtype=jnp.float32)
    nb = S_ // QB
    q_blocks = q.reshape(B_, nb, QB, H, 2, dh).transpose(1, 0, 2, 3, 4, 5)
    starts = jnp.arange(nb, dtype=jnp.int32) * QB

    def block(args):
        qb, start = args
        s = jnp.einsum('bqhcd,bkhcd->bhcqk', qb, k).astype(jnp.float32)
        q_pos = (start + jnp.arange(QB, dtype=jnp.int32)).astype(jnp.float32)
        dist = jnp.abs(q_pos[:, None] - key_pos[None, :])
        s = s - slopes * dist[None, None]
        p = jax.nn.softmax(s, axis=-1)
        a = (p[:, :, 0] - lam * p[:, :, 1]).astype(v.dtype)
        return jnp.einsum('bhqk,bkhd->bqhd', a, v)

    o = lax.map(block, (q_blocks, starts))
    o = o.transpose(1, 0, 2, 3, 4).reshape(B_, S_, H, DIFF_V_DIM)
    o = rms_norm(o, subln_gain) * (1.0 - lambda_init)
    return o.reshape(B_, S_, H * DIFF_V_DIM) @ w_out


def retention_scan(q, k, v, log_g, strict):
    B_, S_, H, dk = q.shape
    dv = v.shape[-1]
    C = RET_CHUNK
    nc = S_ // C

    def chunks(t):
        return t.reshape(B_, nc, C, H, t.shape[-1]).transpose(1, 0, 3, 2, 4)

    lg = log_g.astype(jnp.float32)
    idx = jnp.arange(C, dtype=jnp.float32)
    rel = idx[:, None] - idx[None, :]
    mask = rel > 0 if strict else rel >= 0
    d_intra = jnp.where(mask, jnp.exp(lg[:, None, None] * jnp.maximum(rel, 0.0)), 0.0).astype(q.dtype)
    xi = jnp.exp(lg[:, None] * (idx + 1.0))[..., None].astype(q.dtype)
    zeta = jnp.exp(lg[:, None] * (C - 1.0 - idx))[..., None].astype(q.dtype)
    chunk_decay = jnp.exp(lg * C)[:, None, None].astype(q.dtype)

    def step(state, xs):
        qc, kc, vc = xs
        scores = jnp.einsum('bhid,bhjd->bhij', qc, kc) * d_intra
        out = (jnp.einsum('bhij,bhjv->bhiv', scores, vc)
               + jnp.einsum('bhid,bhdv->bhiv', qc, state) * xi)
        state = state * chunk_decay + jnp.einsum('bhjd,bhjv->bhdv', kc * zeta, vc)
        return state, out

    init = jnp.zeros((B_, H, dk, dv), q.dtype)
    _, o = lax.scan(step, init, (chunks(q), chunks(k), chunks(v)))
    return o.transpose(1, 0, 3, 2, 4).reshape(B_, S_, H, dv)


def retention(h, w_in, w_out, log_g_fwd, log_g_bwd):
    B_, S_, _ = h.shape
    H = RET_HEADS
    proj = h @ w_in
    q, k, v, g = jnp.split(proj, [D_MODEL, 2 * D_MODEL, 2 * D_MODEL + RET_VALUE_WIDTH], axis=-1)
    q = q.reshape(B_, S_, H, RET_QK_DIM)
    k = k.reshape(B_, S_, H, RET_QK_DIM) * (RET_QK_DIM ** -0.5)
    v = v.reshape(B_, S_, H, RET_V_DIM)
    o_fwd = retention_scan(q, k, v, log_g_fwd, strict=False)
    o_bwd = jnp.flip(retention_scan(jnp.flip(q, axis=1), jnp.flip(k, axis=1), jnp.flip(v, axis=1),
                                    log_g_bwd, strict=True), axis=1)
    o = rms_norm(o_fwd + o_bwd)
    o = jax.nn.silu(g) * o.reshape(B_, S_, RET_VALUE_WIDTH)
    return o @ w_out


def moe_ffn(h, router_w, router_b, w1, b1, w2, b2):
    logits = (h @ router_w + router_b).astype(jnp.float32)
    vals, idx = lax.top_k(logits, TOP_K)
    wts = jax.nn.softmax(vals, axis=-1)
    gates = jnp.einsum('bske,bsk->bse', jax.nn.one_hot(idx, N_EXPERTS, dtype=jnp.float32), wts).astype(h.dtype)
    out = jnp.zeros_like(h)
    for e in range(N_EXPERTS):
        u = h @ w1[e] + b1[e]
        x_glu = jnp.minimum(u[..., ::2], SWIGLU_LIMIT)
        x_lin = jnp.clip(u[..., 1::2], -SWIGLU_LIMIT, SWIGLU_LIMIT)
        act = x_glu * jax.nn.sigmoid(SWIGLU_ALPHA * x_glu) * (x_lin + 1.0)
        out = out + gates[..., e:e + 1] * (act @ w2[e] + b2[e])
    return out


def setup_inputs(seed: int = 0) -> dict:
    key = jax.random.key(seed)
    ks = jax.random.split(key, 24)
    f32 = jnp.float32
    nd, nr = N_DIFF_LAYERS, N_RET_LAYERS
    D, dh = D_MODEL, DIFF_HEAD_DIM
    out_scale = (2.0 * DEPTH) ** -0.5

    def nrm(k, shape, scale):
        return jax.random.normal(k, shape, f32) * scale

    def gain(k, shape):
        return 1.0 + 0.05 * jax.random.normal(k, shape, f32)

    base_log_decay = jnp.log(1.0 - jnp.exp2(-5.0 - jnp.arange(RET_HEADS, dtype=f32)))
    return {
        "x": jax.random.normal(ks[0], (BATCH, SEQ, D), f32),
        "diff_norm": gain(ks[1], (nd, D)),
        "diff_w_in": nrm(ks[2], (nd, D, 3 * D), D ** -0.5),
        "diff_w_out": nrm(ks[3], (nd, D, D), D ** -0.5 * out_scale),
        "diff_lambda_q1": nrm(ks[4], (nd, dh), 0.1),
        "diff_lambda_k1": nrm(ks[5], (nd, dh), 0.1),
        "diff_lambda_q2": nrm(ks[6], (nd, dh), 0.1),
        "diff_lambda_k2": nrm(ks[7], (nd, dh), 0.1),
        "diff_q_norm": gain(ks[8], (nd, dh)),
        "diff_k_norm": gain(ks[9], (nd, dh)),
        "diff_subln": gain(ks[10], (nd, DIFF_V_DIM)),
        "ret_norm": gain(ks[11], (nr, D)),
        "ret_w_in": nrm(ks[12], (nr, D, 2 * D + 2 * RET_VALUE_WIDTH), D ** -0.5),
        "ret_w_out": nrm(ks[13], (nr, RET_VALUE_WIDTH, D), RET_VALUE_WIDTH ** -0.5 * out_scale),
        "ret_log_decay_fwd": base_log_decay * (1.0 + 0.05 * jax.random.uniform(ks[14], (nr, RET_HEADS), f32, -1.0, 1.0)),
        "ret_log_decay_bwd": base_log_decay * (1.0 + 0.05 * jax.random.uniform(ks[15], (nr, RET_HEADS), f32, -1.0, 1.0)),
        "moe_norm": gain(ks[16], (DEPTH, D)),
        "moe_router_w": nrm(ks[17], (DEPTH, D, N_EXPERTS), D ** -0.5),
        "moe_router_b": nrm(ks[18], (DEPTH, N_EXPERTS), 0.01),
        "moe_w1": nrm(ks[19], (DEPTH, N_EXPERTS, D, 2 * EXPERT_FF), D ** -0.5),
        "moe_b1": nrm(ks[20], (DEPTH, N_EXPERTS, 2 * EXPERT_FF), 0.02),
        "moe_w2": nrm(ks[21], (DEPTH, N_EXPERTS, EXPERT_FF, D), EXPERT_FF ** -0.5 * out_scale),
        "moe_b2": nrm(ks[22], (DEPTH, N_EXPERTS, D), 0.02),
    }


def reference(x, diff_norm, diff_w_in, diff_w_out, diff_lambda_q1, diff_lambda_k1, diff_lambda_q2,
              diff_lambda_k2, diff_q_norm, diff_k_norm, diff_subln, ret_norm, ret_w_in, ret_w_out,
              ret_log_decay_fwd, ret_log_decay_bwd, moe_norm, moe_router_w, moe_router_b,
              moe_w1, moe_b1, moe_w2, moe_b2):
    for i in range(DEPTH):
        j = i // N_MIXERS
        if i % N_MIXERS == 0:
            h = rms_norm(x, diff_norm[j])
            x = x + diff_attention(h, diff_w_in[j], diff_w_out[j], diff_lambda_q1[j], diff_lambda_k1[j],
                                   diff_lambda_q2[j], diff_lambda_k2[j], diff_q_norm[j], diff_k_norm[j],
                                   diff_subln[j], lambda_init_fn(i))
        else:
            h = rms_norm(x, ret_norm[j])
            x = x + retention(h, ret_w_in[j], ret_w_out[j], ret_log_decay_fwd[j], ret_log_decay_bwd[j])
        h = rms_norm(x, moe_norm[i])
        x = x + moe_ffn(h, moe_router_w[i], moe_router_b[i], moe_w1[i], moe_b1[i], moe_w2[i], moe_b2[i])
    return x
```

```python
import functools
import math

import jax
import jax.numpy as jnp
from jax import lax
from jax.experimental import pallas as pl
from jax.experimental.pallas import tpu as pltpu

F32 = jnp.float32
BF16 = jnp.bfloat16
I32 = jnp.int32

NORM_EPS = 1e-5
N_MIXERS = 2

DIFF_HEADS = 8
DIFF_HEAD_DIM = 64
DIFF_V_DIM = 2 * DIFF_HEAD_DIM

RET_HEADS = 4

N_EXPERTS = 32
TOP_K = 4
SWIGLU_ALPHA = 1.702
SWIGLU_LIMIT = 7.0

V7X_VMEM_LIMIT_BYTES = 56 * 1024 * 1024
LANES = 128


def _lambda_init(layer_idx):
    return 0.8 - 0.6 * math.exp(-0.3 * layer_idx)


def _params(semantics):
    return pltpu.CompilerParams(dimension_semantics=semantics, vmem_limit_bytes=V7X_VMEM_LIMIT_BYTES)


def _rms(x, eps=NORM_EPS):
    return x * lax.rsqrt(jnp.mean(x * x, axis=-1, keepdims=True) + eps)


def _in_proj_kernel(x_ref, g_ref, w_ref, gmat_ref, qkg_ref, o_ref, h_scr, *, tn, n_normed):
    h_scr[...] = (_rms(x_ref[...]) * g_ref[...]).astype(BF16)
    n_total = w_ref.shape[1]
    for c in range(n_total // tn):
        cols = slice(c * tn, (c + 1) * tn)
        acc = jnp.dot(h_scr[...], w_ref[:, cols], preferred_element_type=F32)
        if c * tn < n_normed:
            sq = acc * acc
            hi = sq.astype(BF16)
            lo = (sq - hi.astype(F32)).astype(BF16)
            msq = (jnp.dot(hi, gmat_ref[...], preferred_element_type=F32)
                   + jnp.dot(lo, gmat_ref[...], preferred_element_type=F32))
            acc = acc * lax.rsqrt(msq + NORM_EPS) * qkg_ref[:, cols]
        o_ref[:, cols] = acc.astype(BF16)


def _in_proj(x, gain, w_bf16, *, tm, tn, qk_gain_row=None):
    t, d = x.shape
    n = w_bf16.shape[1]
    if qk_gain_row is None:
        n_normed = 0
        qk_gain_row = jnp.zeros((1, n), F32)
    else:
        n_normed = qk_gain_row.shape[1]
        qk_gain_row = jnp.pad(qk_gain_row, ((0, 0), (0, n - n_normed)))
    grp = jnp.arange(tn) // DIFF_HEAD_DIM
    gmat = jnp.where(grp[:, None] == grp[None, :], 1.0 / DIFF_HEAD_DIM, 0.0).astype(BF16)
    return pl.pallas_call(
        functools.partial(_in_proj_kernel, tn=tn, n_normed=n_normed),
        grid=(t // tm,),
        in_specs=[
            pl.BlockSpec((tm, d), lambda i: (i, 0)),
            pl.BlockSpec((1, d), lambda i: (0, 0)),
            pl.BlockSpec((d, n), lambda i: (0, 0)),
            pl.BlockSpec((tn, tn), lambda i: (0, 0)),
            pl.BlockSpec((1, n), lambda i: (0, 0)),
        ],
        out_specs=pl.BlockSpec((tm, n), lambda i: (i, 0)),
        out_shape=jax.ShapeDtypeStruct((t, n), BF16),
        scratch_shapes=[pltpu.VMEM((tm, d), BF16)],
        compiler_params=_params(("parallel",)),
        name="in_proj",
    )(x, gain.reshape(1, d), w_bf16, gmat, qk_gain_row)


def _diff_attn_kernel(slopes_ref, lam_ref, q_ref, k_ref, v_ref, sg_ref, o_ref, *, tq, tk, out_scale):
    h = pl.program_id(1)
    qi = pl.program_id(2)
    slope = slopes_ref[h]
    lam = lam_ref[0]
    seq = k_ref.shape[0]

    q = q_ref[...]
    lane = lax.broadcasted_iota(I32, q.shape, 1)
    zero = jnp.zeros_like(q)
    qq = jnp.concatenate([jnp.where(lane < DIFF_HEAD_DIM, q, zero),
                          jnp.where(lane >= DIFF_HEAD_DIM, q, zero)], axis=0)

    rel = (lax.broadcasted_iota(I32, (tq, tk), 0) - lax.broadcasted_iota(I32, (tq, tk), 1)
           + qi * tq)

    def body(j, carry):
        m, l, acc = carry
        start = pl.multiple_of(j * tk, tk)
        k = k_ref[pl.ds(start, tk), :]
        v = v_ref[pl.ds(start, tk), :]
        s = lax.dot_general(qq, k, (((1,), (1,)), ((), ())), preferred_element_type=F32)
        bias = slope * jnp.abs(rel - j * tk).astype(F32)
        s = s - jnp.concatenate([bias, bias], axis=0)
        m_new = jnp.maximum(m, jnp.max(s, axis=-1, keepdims=True))
        alpha = jnp.exp(m - m_new)
        p = jnp.exp(s - m_new)
        l = alpha * l + jnp.sum(p, axis=-1, keepdims=True)
        acc = alpha * acc + jnp.dot(p.astype(BF16), v, preferred_element_type=F32)
        return m_new, l, acc

    m0 = jnp.full((2 * tq, 1), -jnp.inf, F32)
    l0 = jnp.zeros((2 * tq, 1), F32)
    acc0 = jnp.zeros((2 * tq, DIFF_V_DIM), F32)
    _, l, acc = lax.fori_loop(0, seq // tk, body, (m0, l0, acc0))
    o = acc / l
    o = o[:tq] - lam * o[tq:]
    o_ref[...] = (_rms(o) * sg_ref[...] * out_scale).astype(BF16)


def _diff_attention(qkv, slopes, lam, subln_gain, lambda_init, *, batch, seq, tq, tk):
    d = DIFF_HEADS * DIFF_V_DIM
    qkv3 = qkv.reshape(batch, seq, 3 * d)
    out = pl.pallas_call(
        functools.partial(_diff_attn_kernel, tq=tq, tk=tk, out_scale=1.0 - lambda_init),
        grid=(batch, DIFF_HEADS, seq // tq),
        in_specs=[
            pl.BlockSpec(memory_space=pltpu.SMEM),
            pl.BlockSpec(memory_space=pltpu.SMEM),
            pl.BlockSpec((None, tq, DIFF_V_DIM), lambda b, h, i: (b, i, h)),
            pl.BlockSpec((None, seq, DIFF_V_DIM), lambda b, h, i: (b, 0, DIFF_HEADS + h)),
            pl.BlockSpec((None, seq, DIFF_V_DIM), lambda b, h, i: (b, 0, 2 * DIFF_HEADS + h)),
            pl.BlockSpec((1, DIFF_V_DIM), lambda b, h, i: (0, 0)),
        ],
        out_specs=pl.BlockSpec((None, tq, DIFF_V_DIM), lambda b, h, i: (b, i, h)),
        out_shape=jax.ShapeDtypeStruct((batch, seq, d), BF16),
        compiler_params=_params(("parallel", "parallel", "parallel")),
        name="diff_attn",
    )(slopes, lam.reshape(1), qkv3, qkv3, qkv3, subln_gain.reshape(1, DIFF_V_DIM))
    return out.reshape(batch * seq, d)


def _retention_kernel(lgf_ref, lgb_ref, q_ref, k_ref, v_ref, g_ref, o_ref, ob_scr, st_scr, *, chunk):
    h = pl.program_id(1)
    lgf = lgf_ref[h]
    lgb = lgb_ref[h]
    seq = q_ref.shape[0]
    nc = seq // chunk
    c_f = float(chunk)

    row = lax.broadcasted_iota(I32, (chunk, chunk), 0)
    col = lax.broadcasted_iota(I32, (chunk, chunk), 1)
    rel = (row - col).astype(F32)
    d_both = (jnp.where(rel >= 0, jnp.exp(lgf * jnp.maximum(rel, 0.0)), 0.0)
              + jnp.where(rel < 0, jnp.exp(lgb * jnp.maximum(-rel, 0.0)), 0.0))
    idx = lax.broadcasted_iota(I32, (chunk, 1), 0).astype(F32)
    xi_f = jnp.exp(lgf * (idx + 1.0))
    zeta_f = jnp.exp(lgf * (c_f - 1.0 - idx))
    xi_b = jnp.exp(lgb * (c_f - idx))
    zeta_b = jnp.exp(lgb * idx)
    decay_f = jnp.exp(lgf * c_f)
    decay_b = jnp.exp(lgb * c_f)

    def state_update(k, v, zeta, decay):
        kz_t = (k.astype(F32) * zeta).T.astype(BF16)
        upd = jnp.dot(kz_t, v, preferred_element_type=F32)
        st_scr[...] = st_scr[...] * decay + upd

    st_scr[...] = jnp.zeros_like(st_scr)

    def bwd_body(i, _):
        c = nc - 1 - i
        rows = pl.ds(pl.multiple_of(c * chunk, chunk), chunk)
        q = q_ref[rows, :]
        inter = jnp.dot(q, st_scr[...].astype(BF16), preferred_element_type=F32)
        ob_scr[rows, :] = inter * xi_b
        state_update(k_ref[rows, :], v_ref[rows, :], zeta_b, decay_b)
        return 0

    lax.fori_loop(0, nc, bwd_body, 0)

    st_scr[...] = jnp.zeros_like(st_scr)

    def fwd_body(c, _):
        rows = pl.ds(pl.multiple_of(c * chunk, chunk), chunk)
        q = q_ref[rows, :]
        k = k_ref[rows, :]
        v = v_ref[rows, :]
        scores = lax.dot_general(q, k, (((1,), (1,)), ((), ())), preferred_element_type=F32) * d_both
        o = jnp.dot(scores.astype(BF16), v, preferred_element_type=F32)
        o = o + jnp.dot(q, st_scr[...].astype(BF16), preferred_element_type=F32) * xi_f
        o = o + ob_scr[rows, :]
        state_update(k, v, zeta_f, decay_f)
        g = g_ref[rows, :].astype(F32)
        o_ref[rows, :] = (g * jax.nn.sigmoid(g) * _rms(o)).astype(BF16)
        return 0

    lax.fori_loop(0, nc, fwd_body, 0)


def _retention(proj, lg_fwd, lg_bwd, *, batch, seq, d_model, chunk):
    dk = d_model // RET_HEADS
    dv = 2 * dk
    vw = RET_HEADS * dv
    p3 = proj.reshape(batch, seq, 2 * d_model + 2 * vw)
    out = pl.pallas_call(
        functools.partial(_retention_kernel, chunk=chunk),
        grid=(batch, RET_HEADS),
        in_specs=[
            pl.BlockSpec(memory_space=pltpu.SMEM),
            pl.BlockSpec(memory_space=pltpu.SMEM),
            pl.BlockSpec((None, seq, dk), lambda b, h: (b, 0, h)),
            pl.BlockSpec((None, seq, dk), lambda b, h: (b, 0, RET_HEADS + h)),
            pl.BlockSpec((None, seq, dv), lambda b, h: (b, 0, RET_HEADS + h)),
            pl.BlockSpec((None, seq, dv), lambda b, h: (b, 0, 2 * RET_HEADS + h)),
        ],
        out_specs=pl.BlockSpec((None, seq, dv), lambda b, h: (b, 0, h)),
        out_shape=jax.ShapeDtypeStruct((batch, seq, vw), BF16),
        scratch_shapes=[pltpu.VMEM((seq, dv), F32), pltpu.VMEM((dk, dv), F32)],
        compiler_params=_params(("parallel", "parallel")),
        name="retention",
    )(lg_fwd, lg_bwd, p3, p3, p3, p3)
    return out.reshape(batch * seq, vw)


def _out_router_kernel(o_ref, w_ref, x_ref, g_ref, rwh_ref, rwl_ref, rb_ref,
                       xn_ref, hm_ref, idx_ref, wts_ref, rank_ref, cnt_ref, carry_scr):
    i = pl.program_id(0)
    tm = x_ref.shape[0]

    @pl.when(i == 0)
    def _():
        carry_scr[...] = jnp.zeros_like(carry_scr)

    xn = x_ref[...] + jnp.dot(o_ref[...], w_ref[...], preferred_element_type=F32)
    xn_ref[...] = xn
    hm = _rms(xn) * g_ref[...]
    hm_ref[...] = hm

    hm_hi = hm.astype(BF16)
    hm_lo = (hm - hm_hi.astype(F32)).astype(BF16)
    nt = (((1,), (1,)), ((), ()))
    logits = (lax.dot_general(rwh_ref[...], hm_hi, nt, preferred_element_type=F32)
              + lax.dot_general(rwh_ref[...], hm_lo, nt, preferred_element_type=F32)
              + lax.dot_general(rwl_ref[...], hm_hi, nt, preferred_element_type=F32)
              + rb_ref[...])

    eidx = lax.broadcasted_iota(I32, (N_EXPERTS, tm), 0)
    work = logits
    vals, sels, hots = [], [], []
    for _ in range(TOP_K):
        m = jnp.max(work, axis=0, keepdims=True)
        sel = jnp.min(jnp.where(work == m, eidx, N_EXPERTS), axis=0, keepdims=True)
        hot = eidx == sel
        vals.append(m)
        sels.append(sel)
        hots.append(hot)
        work = jnp.where(hot, -jnp.inf, work)
    exps = [jnp.exp(v - vals[0]) for v in vals]
    denom = exps[0] + exps[1] + exps[2] + exps[3]

    mask = jnp.where(hots[0] | hots[1] | hots[2] | hots[3], 1.0, 0.0)
    tri = (lax.broadcasted_iota(I32, (tm, tm), 0) < lax.broadcasted_iota(I32, (tm, tm), 1))
    excl = jnp.dot(mask.astype(BF16), jnp.where(tri, 1.0, 0.0).astype(BF16), preferred_element_type=F32)
    rank_e = excl + carry_scr[:, 0:1]
    for k in range(TOP_K):
        idx_ref[k:k + 1, :] = sels[k]
        wts_ref[k:k + 1, :] = exps[k] / denom
        rank_ref[k:k + 1, :] = jnp.sum(jnp.where(hots[k], rank_e, 0.0), axis=0, keepdims=True).astype(I32)
    carry_scr[...] = carry_scr[...] + jnp.sum(mask, axis=1, keepdims=True)
    cnt_ref[...] = carry_scr[...].astype(I32)


def _out_router(o, w_out_bf16, x, moe_gain, rw_hi_t, rw_lo_t, router_b, *, tm):
    t, d = x.shape
    dv = o.shape[1]
    tok = lambda i: (i, 0)
    fixed = lambda i: (0, 0)
    lanes_tok = lambda i: (0, i)
    return pl.pallas_call(
        _out_router_kernel,
        grid=(t // tm,),
        in_specs=[
            pl.BlockSpec((tm, dv), tok),
            pl.BlockSpec((dv, d), fixed),
            pl.BlockSpec((tm, d), tok),
            pl.BlockSpec((1, d), fixed),
            pl.BlockSpec((N_EXPERTS, d), fixed),
            pl.BlockSpec((N_EXPERTS, d), fixed),
            pl.BlockSpec((N_EXPERTS, 1), fixed),
        ],
        out_specs=[
            pl.BlockSpec((tm, d), tok),
            pl.BlockSpec((tm, d), tok),
            pl.BlockSpec((TOP_K, tm), lanes_tok),
            pl.BlockSpec((TOP_K, tm), lanes_tok),
            pl.BlockSpec((TOP_K, tm), lanes_tok),
            pl.BlockSpec((N_EXPERTS, LANES), fixed),
        ],
        out_shape=[
            jax.ShapeDtypeStruct((t, d), F32),
            jax.ShapeDtypeStruct((t, d), F32),
            jax.ShapeDtypeStruct((TOP_K, t), I32),
            jax.ShapeDtypeStruct((TOP_K, t), F32),
            jax.ShapeDtypeStruct((TOP_K, t), I32),
            jax.ShapeDtypeStruct((N_EXPERTS, LANES), I32),
        ],
        scratch_shapes=[pltpu.VMEM((N_EXPERTS, LANES), F32)],
        compiler_params=_params(("arbitrary",)),
        name="out_router",
    )(o, w_out_bf16, x, moe_gain.reshape(1, d), rw_hi_t, rw_lo_t, router_b.reshape(N_EXPERTS, 1))


def _row_copy(src_ref, src_row, dst_ref, dst_row, sem):
    return pltpu.make_async_copy(src_ref.at[pl.ds(src_row, 1), :], dst_ref.at[pl.ds(dst_row, 1), :], sem)


def _dispatch_kernel(ends_ref, padded_ref, dest_hbm, hm_ref, xs_hbm, idx_smem, zero_scr, idx_sem, row_sem,
                     *, tm, tr):
    i = pl.program_id(0)
    idx_copy = pltpu.make_async_copy(dest_hbm.at[i], idx_smem, idx_sem)
    idx_copy.start()

    @pl.when(i == 0)
    def _():
        zero_scr[...] = jnp.zeros_like(zero_scr)

        def fill(e):
            start = pl.multiple_of(ends_ref[e] - tr, tr)
            return pltpu.make_async_copy(zero_scr, xs_hbm.at[pl.ds(start, tr), :], row_sem)

        for e in range(N_EXPERTS):
            @pl.when(padded_ref[e] > 0)
            def _():
                fill(e).start()
        for e in range(N_EXPERTS):
            @pl.when(padded_ref[e] > 0)
            def _():
                fill(e).wait()

    idx_copy.wait()

    def start_row(r, _):
        _row_copy(hm_ref, r % tm, xs_hbm, idx_smem[r], row_sem).start()
        return 0

    lax.fori_loop(0, TOP_K * tm, start_row, 0)

    def wait_row(r, _):
        _row_copy(hm_ref, 0, xs_hbm, 0, row_sem).wait()
        return 0

    lax.fori_loop(0, TOP_K * tm, wait_row, 0)


def _dispatch(hm, dest_tiles, ends, padded, *, tm, tr, n_rows):
    t, d = hm.shape
    return pl.pallas_call(
        functools.partial(_dispatch_kernel, tm=tm, tr=tr),
        grid_spec=pltpu.PrefetchScalarGridSpec(
            num_scalar_prefetch=2,
            grid=(t // tm,),
            in_specs=[
                pl.BlockSpec(memory_space=pl.ANY),
                pl.BlockSpec((tm, d), lambda i, *_: (i, 0)),
            ],
            out_specs=pl.BlockSpec(memory_space=pl.ANY),
            scratch_shapes=[
                pltpu.SMEM((TOP_K * tm,), I32),
                pltpu.VMEM((tr, d), F32),
                pltpu.SemaphoreType.DMA(()),
                pltpu.SemaphoreType.DMA(()),
            ],
        ),
        out_shape=jax.ShapeDtypeStruct((n_rows, d), F32),
        compiler_params=_params(("arbitrary",)),
        name="moe_dispatch",
    )(ends, padded, dest_tiles, hm)


def _experts_kernel(te_ref, tv_ref, xs_ref, w1g_ref, w1l_ref, b1g_ref, b1l_ref, w2_ref, b2_ref, ys_ref):
    i = pl.program_id(0)

    @pl.when(tv_ref[i] > 0)
    def _():
        x = xs_ref[...].astype(BF16)
        glu = jnp.dot(x, w1g_ref[...], preferred_element_type=F32) + b1g_ref[...]
        lin = jnp.dot(x, w1l_ref[...], preferred_element_type=F32) + b1l_ref[...]
        glu = jnp.minimum(glu, SWIGLU_LIMIT)
        lin = jnp.clip(lin, -SWIGLU_LIMIT, SWIGLU_LIMIT)
        act = glu * jax.nn.sigmoid(SWIGLU_ALPHA * glu) * (lin + 1.0)
        ys_ref[...] = jnp.dot(act.astype(BF16), w2_ref[...], preferred_element_type=F32) + b2_ref[...]

    @pl.when(tv_ref[i] == 0)
    def _():
        ys_ref[...] = jnp.zeros_like(ys_ref)


def _experts(xs, tile_expert, tile_valid, w1g, w1l, b1g, b1l, w2, b2, *, tr):
    n_rows, d = xs.shape
    ff = w1g.shape[2]
    n_tiles = n_rows // tr
    row_in = lambda i, te, tv: (jnp.where(tv[i] > 0, i, 0), 0)
    row_out = lambda i, te, tv: (jnp.where(tv[i] > 0, i, n_tiles), 0)
    by_expert = lambda i, te, tv: (te[i], 0, 0)
    return pl.pallas_call(
        _experts_kernel,
        grid_spec=pltpu.PrefetchScalarGridSpec(
            num_scalar_prefetch=2,
            grid=(n_tiles,),
            in_specs=[
                pl.BlockSpec((tr, d), row_in),
                pl.BlockSpec((None, d, ff), by_expert),
                pl.BlockSpec((None, d, ff), by_expert),
                pl.BlockSpec((None, 1, ff), by_expert),
                pl.BlockSpec((None, 1, ff), by_expert),
                pl.BlockSpec((None, ff, d), by_expert),
                pl.BlockSpec((None, 1, d), by_expert),
            ],
            out_specs=pl.BlockSpec((tr, d), row_out),
        ),
        out_shape=jax.ShapeDtypeStruct((n_rows + tr, d), F32),
        compiler_params=_params(("arbitrary",)),
        name="moe_experts",
    )(tile_expert, tile_valid, xs, w1g, w1l, b1g, b1l, w2, b2)


def _combine_kernel(dest_hbm, ys_hbm, x_ref, wt_ref, o_ref, idx_smem, rows_scr, idx_sem, row_sem, *, tm):
    i = pl.program_id(0)
    idx_copy = pltpu.make_async_copy(dest_hbm.at[i], idx_smem, idx_sem)
    idx_copy.start()
    idx_copy.wait()

    def start_row(r, _):
        _row_copy(ys_hbm, idx_smem[r], rows_scr, r, row_sem).start()
        return 0

    lax.fori_loop(0, TOP_K * tm, start_row, 0)

    def wait_row(r, _):
        _row_copy(ys_hbm, 0, rows_scr, 0, row_sem).wait()
        return 0

    lax.fori_loop(0, TOP_K * tm, wait_row, 0)

    out = x_ref[...]
    for k in range(TOP_K):
        out = out + wt_ref[:, k:k + 1] * rows_scr[k * tm:(k + 1) * tm, :]
    o_ref[...] = out


def _combine(ys, dest_tiles, x, wts_t, *, tm):
    t, d = x.shape
    return pl.pallas_call(
        functools.partial(_combine_kernel, tm=tm),
        grid=(t // tm,),
        in_specs=[
            pl.BlockSpec(memory_space=pl.ANY),
            pl.BlockSpec(memory_space=pl.ANY),
            pl.BlockSpec((tm, d), lambda i: (i, 0)),
            pl.BlockSpec((tm, TOP_K), lambda i: (i, 0)),
        ],
        out_specs=pl.BlockSpec((tm, d), lambda i: (i, 0)),
        out_shape=jax.ShapeDtypeStruct((t, d), F32),
        scratch_shapes=[
            pltpu.SMEM((TOP_K * tm,), I32),
            pltpu.VMEM((TOP_K * tm, d), F32),
            pltpu.SemaphoreType.DMA(()),
            pltpu.SemaphoreType.DMA(()),
        ],
        compiler_params=_params(("arbitrary",)),
        name="moe_combine",
    )(dest_tiles, ys, x, wts_t)


def _tiles(batch, seq):
    t = batch * seq
    tm = min(512, t)
    return dict(
        tm=tm,
        tn=256,
        tq=min(256, seq),
        tk=min(512, seq),
        chunk=min(128, seq),
        tmd=min(256, t),
        tr=min(512, max(8, TOP_K * t // N_EXPERTS)),
    )


def _moe(o, w_out, x, moe_gain, router_w, router_b, w1, b1, w2, b2, *, cfg):
    t, d = x.shape
    tm, tmd, tr = cfg["tm"], cfg["tmd"], cfg["tr"]
    rw_t = router_w.T
    rw_hi = rw_t.astype(BF16)
    rw_lo = (rw_t - rw_hi.astype(F32)).astype(BF16)
    x, hm, idx, wts, rank, cnt = _out_router(o, w_out.astype(BF16), x, moe_gain, rw_hi, rw_lo, router_b, tm=tm)

    counts = cnt[:, 0]
    padded = ((counts + tr - 1) // tr) * tr
    ends = jnp.cumsum(padded).astype(I32)
    offs = ends - padded
    dest = offs[idx] + rank
    dest_tiles = dest.reshape(TOP_K, t // tmd, tmd).transpose(1, 0, 2).reshape(t // tmd, TOP_K * tmd)
    n_rows = TOP_K * t + N_EXPERTS * tr
    tile_start = jnp.arange(n_rows // tr, dtype=I32) * tr
    tile_expert = jnp.minimum(jnp.searchsorted(ends, tile_start, side="right"), N_EXPERTS - 1).astype(I32)
    tile_valid = (tile_start < ends[-1]).astype(I32)

    xs = _dispatch(hm, dest_tiles, ends, padded.astype(I32), tm=tmd, tr=tr, n_rows=n_rows)
    ff = w2.shape[1]
    ys = _experts(xs, tile_expert, tile_valid,
                  w1[:, :, 0::2].astype(BF16), w1[:, :, 1::2].astype(BF16),
                  b1[:, 0::2].reshape(N_EXPERTS, 1, ff), b1[:, 1::2].reshape(N_EXPERTS, 1, ff),
                  w2.astype(BF16), b2.reshape(N_EXPERTS, 1, d), tr=tr)
    return _combine(ys, dest_tiles, x, wts.T, tm=tmd)


def kernel(x, diff_norm, diff_w_in, diff_w_out, diff_lambda_q1, diff_lambda_k1, diff_lambda_q2, diff_lambda_k2, diff_q_norm, diff_k_norm, diff_subln, ret_norm, ret_w_in, ret_w_out, ret_log_decay_fwd, ret_log_decay_bwd, moe_norm, moe_router_w, moe_router_b, moe_w1, moe_b1, moe_w2, moe_b2):
    batch, seq, d = x.shape
    depth = moe_norm.shape[0]
    cfg = _tiles(batch, seq)
    x = x.reshape(batch * seq, d)
    slopes = jnp.exp2(-8.0 * jnp.arange(1, DIFF_HEADS + 1, dtype=F32) / DIFF_HEADS)
    for i in range(depth):
        j = i // N_MIXERS
        if i % N_MIXERS == 0:
            lambda_init = _lambda_init(i)
            lam = (jnp.exp(jnp.sum(diff_lambda_q1[j] * diff_lambda_k1[j]))
                   - jnp.exp(jnp.sum(diff_lambda_q2[j] * diff_lambda_k2[j])) + lambda_init)
            reps = d // DIFF_HEAD_DIM
            qk_gain = jnp.concatenate([jnp.tile(diff_q_norm[j], reps) * DIFF_HEAD_DIM ** -0.5,
                                       jnp.tile(diff_k_norm[j], reps)]).reshape(1, 2 * d)
            qkv = _in_proj(x, diff_norm[j], diff_w_in[j].astype(BF16), tm=cfg["tm"], tn=cfg["tn"],
                           qk_gain_row=qk_gain)
            o = _diff_attention(qkv, slopes, lam, diff_subln[j], lambda_init,
                                batch=batch, seq=seq, tq=cfg["tq"], tk=cfg["tk"])
            w_out = diff_w_out[j]
        else:
            dk = d // RET_HEADS
            col_scale = jnp.ones((ret_w_in.shape[2],), F32).at[d:2 * d].set(dk ** -0.5)
            proj = _in_proj(x, ret_norm[j], (ret_w_in[j] * col_scale).astype(BF16), tm=cfg["tm"], tn=cfg["tn"])
            o = _retention(proj, ret_log_decay_fwd[j], ret_log_decay_bwd[j],
                           batch=batch, seq=seq, d_model=d, chunk=cfg["chunk"])
            w_out = ret_w_out[j]
        x = _moe(o, w_out, x, moe_norm[i], moe_router_w[i], moe_router_b[i],
                 moe_w1[i], moe_b1[i], moe_w2[i], moe_b2[i], cfg=cfg)
    return x.reshape(batch, seq, d)
```

```python
import functools
import math

import jax
import jax.numpy as jnp
from jax import lax
from jax.experimental import pallas as pl
from jax.experimental.pallas import tpu as pltpu

F32 = jnp.float32
BF16 = jnp.bfloat16
I32 = jnp.int32

NORM_EPS = 1e-5
N_MIXERS = 2

DIFF_HEADS = 8
DIFF_HEAD_DIM = 64
DIFF_V_DIM = 2 * DIFF_HEAD_DIM

RET_HEADS = 4

N_EXPERTS = 32
TOP_K = 4
SWIGLU_ALPHA = 1.702
SWIGLU_LIMIT = 7.0

V7X_VMEM_LIMIT_BYTES = 56 * 1024 * 1024
LANES = 128


def _lambda_init(layer_idx):
    return 0.8 - 0.6 * math.exp(-0.3 * layer_idx)


def _params(semantics):
    return pltpu.CompilerParams(dimension_semantics=semantics, vmem_limit_bytes=V7X_VMEM_LIMIT_BYTES)


def _rms(x, eps=NORM_EPS):
    return x * lax.rsqrt(jnp.mean(x * x, axis=-1, keepdims=True) + eps)


def _in_proj_kernel(x_ref, g_ref, w_ref, gmat_ref, qkg_ref, o_ref, h_scr, *, tn, n_normed):
    h_scr[...] = (_rms(x_ref[...]) * g_ref[...]).astype(BF16)
    n_total = w_ref.shape[1]
    for c in range(n_total // tn):
        cols = slice(c * tn, (c + 1) * tn)
        acc = jnp.dot(h_scr[...], w_ref[:, cols], preferred_element_type=F32)
        if c * tn < n_normed:
            sq = acc * acc
            hi = sq.astype(BF16)
            lo = (sq - hi.astype(F32)).astype(BF16)
            msq = (jnp.dot(hi, gmat_ref[...], preferred_element_type=F32)
                   + jnp.dot(lo, gmat_ref[...], preferred_element_type=F32))
            acc = acc * lax.rsqrt(msq + NORM_EPS) * qkg_ref[:, cols]
        o_ref[:, cols] = acc.astype(BF16)


def _in_proj(x, gain, w_bf16, *, tm, tn, qk_gain_row=None):
    t, d = x.shape
    n = w_bf16.shape[1]
    if qk_gain_row is None:
        n_normed = 0
        qk_gain_row = jnp.zeros((1, n), F32)
    else:
        n_normed = qk_gain_row.shape[1]
        qk_gain_row = jnp.pad(qk_gain_row, ((0, 0), (0, n - n_normed)))
    grp = jnp.arange(tn) // DIFF_HEAD_DIM
    gmat = jnp.where(grp[:, None] == grp[None, :], 1.0 / DIFF_HEAD_DIM, 0.0).astype(BF16)
    return pl.pallas_call(
        functools.partial(_in_proj_kernel, tn=tn, n_normed=n_normed),
        grid=(t // tm,),
        in_specs=[
            pl.BlockSpec((tm, d), lambda i: (i, 0)),
            pl.BlockSpec((1, d), lambda i: (0, 0)),
            pl.BlockSpec((d, n), lambda i: (0, 0)),
            pl.BlockSpec((tn, tn), lambda i: (0, 0)),
            pl.BlockSpec((1, n), lambda i: (0, 0)),
        ],
        out_specs=pl.BlockSpec((tm, n), lambda i: (i, 0)),
        out_shape=jax.ShapeDtypeStruct((t, n), BF16),
        scratch_shapes=[pltpu.VMEM((tm, d), BF16)],
        compiler_params=_params(("parallel",)),
        name="in_proj",
    )(x, gain.reshape(1, d), w_bf16, gmat, qk_gain_row)


LOG2E = math.log2(math.e)
LOG2E_BF16_PARTS = (1.4453125, -0.00262451171875, 7.063150405883789e-06)
ALIBI_LOW_BITS = 255
ALIBI_HIGH_BIT = 256


def _alibi_key_features(slopes, seq, tk):
    dj = jnp.arange(seq, dtype=I32) % tk
    lo = (dj & ALIBI_LOW_BITS).astype(F32)
    hi = (dj & ALIBI_HIGH_BIT).astype(F32)
    lane = jnp.arange(LANES)
    feat = jnp.where(lane[None, :] < 3, lo[:, None], jnp.where(lane[None, :] < 6, hi[:, None], 0.0))
    return (slopes[:, None, None] * feat[None]).astype(BF16)


def _diff_attn_kernel(slopes_ref, lam_ref, q_ref, k_ref, vt_ref, fk_ref, sg_ref, o_ref,
                      qa_scr, s_scr, mt_scr, p_scr, a_scr, m_scr, l_scr, acc_scr, *, tq, tk, out_scale):
    h = pl.program_id(1)
    qi = pl.program_id(2)
    slope2 = slopes_ref[h] * LOG2E
    lam = lam_ref[0]
    nk = k_ref.shape[0] // tk
    i0 = qi * tq
    jd = i0 // tk
    cols2 = 2 * tq

    q = q_ref[...]
    lane = lax.broadcasted_iota(I32, q.shape, 1)
    zero = jnp.zeros_like(q)
    qq = jnp.concatenate([jnp.where(lane < DIFF_HEAD_DIM, q, zero),
                          jnp.where(lane >= DIFF_HEAD_DIM, q, zero)], axis=0)
    flane = lax.broadcasted_iota(I32, (cols2, LANES), 1)
    a, b, c = LOG2E_BF16_PARTS
    feat = jnp.where(flane % 3 == 0, a, jnp.where(flane % 3 == 1, b, c))
    feat = jnp.where(flane < 6, feat, 0.0)
    qa_scr[0] = jnp.concatenate([qq, feat.astype(BF16)], axis=1)
    qa_scr[1] = jnp.concatenate([qq, (-feat).astype(BF16)], axis=1)
    col = lax.broadcasted_iota(I32, (1, cols2), 1)
    q_pos = (i0 + jnp.where(col >= tq, col - tq, col)).astype(F32)

    m_scr[...] = jnp.full_like(m_scr, -jnp.inf)
    l_scr[...] = jnp.zeros_like(l_scr)
    acc_scr[...] = jnp.zeros_like(acc_scr)
    nt = (((1,), (1,)), ((), ()))

    def key_rows(j):
        return pl.ds(pl.multiple_of(j * tk, tk), tk)

    def exponentials(slot, col_shift):
        for cb in range(cols2 // LANES):
            cs = slice(cb * LANES, (cb + 1) * LANES)
            shift = col_shift[:, cs]
            m_old = m_scr[:, cs]
            m_new = jnp.maximum(m_old, mt_scr[slot, :, cs] - shift)
            alpha = jnp.exp2(m_old - m_new)
            p = jnp.exp2(s_scr[slot, :, cs] - (m_new + shift))
            l_scr[:, cs] = alpha * l_scr[:, cs] + jnp.sum(p, axis=0, keepdims=True)
            m_scr[:, cs] = m_new
            a_scr[slot, :, cs] = alpha
            p_scr[slot, :, cs] = p.astype(BF16)

    def values(slot, j):
        acc_scr[...] = (a_scr[slot] * acc_scr[...]
                        + jnp.dot(vt_ref[:, key_rows(j)], p_scr[slot], preferred_element_type=F32))

    s = lax.dot_general(k_ref[key_rows(jd), :], qq, nt, preferred_element_type=F32)
    key_pos = (jd * tk + lax.broadcasted_iota(I32, (tk, 1), 0)).astype(F32)
    s = s - slope2 * jnp.abs(key_pos - q_pos)
    s_scr[0] = s
    mt_scr[0] = jnp.max(s, axis=0, keepdims=True)
    exponentials(0, jnp.zeros((1, cols2), F32))
    values(0, jd)

    n_lin = nk - 1

    def tile_of(t):
        return t + (t >= jd).astype(I32)

    def scores(t, slot):
        j = tile_of(jnp.minimum(t, n_lin - 1))
        k_aug = jnp.concatenate([k_ref[key_rows(j), :], fk_ref[key_rows(j), :]], axis=1)
        s = lax.dot_general(k_aug, qa_scr[(j > jd).astype(I32)], nt, preferred_element_type=F32)
        s_scr[slot] = s
        mt_scr[slot] = jnp.max(s, axis=0, keepdims=True)

    def shift_of(t):
        j = tile_of(t)
        d = slope2 * (q_pos - (j * tk).astype(F32))
        return jnp.where(j > jd, -d, d)

    if n_lin > 0:
        p_scr[1] = jnp.zeros(p_scr.shape[1:], BF16)
        a_scr[1] = jnp.ones(a_scr.shape[1:], F32)
        scores(0, 0)

        def stage(t, slot):
            scores(t + 1, 1 - slot)
            values(1 - slot, tile_of(jnp.maximum(t - 1, 0)))
            exponentials(slot, shift_of(t))

        def pair(tt, _):
            stage(2 * tt, 0)
            stage(2 * tt + 1, 1)
            return 0

        lax.fori_loop(0, n_lin // 2, pair, 0)
        if n_lin % 2:
            stage(jnp.int32(n_lin - 1), 0)
        values((n_lin - 1) % 2, tile_of(jnp.int32(n_lin - 1)))

    o = acc_scr[...] / l_scr[...]
    o = o[:, :tq] - lam * o[:, tq:]
    o = o * lax.rsqrt(jnp.mean(o * o, axis=0, keepdims=True) + NORM_EPS)
    o_ref[...] = (o.T * sg_ref[...] * out_scale).astype(BF16)


def _diff_attention(qkv, slopes, lam, subln_gain, lambda_init, *, batch, seq, tq, tk):
    assert tk % tq == 0 and tk <= 2 * ALIBI_HIGH_BIT and tq % LANES == 0
    d = DIFF_HEADS * DIFF_V_DIM
    qkv3 = qkv.reshape(batch, seq, 3 * d)
    vt = jnp.swapaxes(qkv3[:, :, 2 * d:], 1, 2)
    fk = _alibi_key_features(slopes, seq, tk)
    out = pl.pallas_call(
        functools.partial(_diff_attn_kernel, tq=tq, tk=tk, out_scale=1.0 - lambda_init),
        grid=(batch, DIFF_HEADS, seq // tq),
        in_specs=[
            pl.BlockSpec(memory_space=pltpu.SMEM),
            pl.BlockSpec(memory_space=pltpu.SMEM),
            pl.BlockSpec((None, tq, DIFF_V_DIM), lambda b, h, i: (b, i, h)),
            pl.BlockSpec((None, seq, DIFF_V_DIM), lambda b, h, i: (b, 0, DIFF_HEADS + h)),
            pl.BlockSpec((None, DIFF_V_DIM, seq), lambda b, h, i: (b, h, 0)),
            pl.BlockSpec((None, seq, LANES), lambda b, h, i: (h, 0, 0)),
            pl.BlockSpec((1, DIFF_V_DIM), lambda b, h, i: (0, 0)),
        ],
        out_specs=pl.BlockSpec((None, tq, DIFF_V_DIM), lambda b, h, i: (b, i, h)),
        out_shape=jax.ShapeDtypeStruct((batch, seq, d), BF16),
        scratch_shapes=[
            pltpu.VMEM((2, 2 * tq, 2 * LANES), BF16),
            pltpu.VMEM((2, tk, 2 * tq), F32),
            pltpu.VMEM((2, 1, 2 * tq), F32),
            pltpu.VMEM((2, tk, 2 * tq), BF16),
            pltpu.VMEM((2, 1, 2 * tq), F32),
            pltpu.VMEM((1, 2 * tq), F32),
            pltpu.VMEM((1, 2 * tq), F32),
            pltpu.VMEM((DIFF_V_DIM, 2 * tq), F32),
        ],
        compiler_params=_params(("parallel", "parallel", "parallel")),
        name="diff_attn",
    )(slopes, lam.reshape(1), qkv3, qkv3, vt, fk, subln_gain.reshape(1, DIFF_V_DIM))
    return out.reshape(batch * seq, d)


def _retention_kernel(lgf_ref, lgb_ref, q_ref, k_ref, v_ref, g_ref, o_ref, ob_scr, st_scr, *, chunk):
    h = pl.program_id(1)
    lgf = lgf_ref[h]
    lgb = lgb_ref[h]
    seq = q_ref.shape[0]
    nc = seq // chunk
    c_f = float(chunk)

    row = lax.broadcasted_iota(I32, (chunk, chunk), 0)
    col = lax.broadcasted_iota(I32, (chunk, chunk), 1)
    rel = (row - col).astype(F32)
    d_both = (jnp.where(rel >= 0, jnp.exp(lgf * jnp.maximum(rel, 0.0)), 0.0)
              + jnp.where(rel < 0, jnp.exp(lgb * jnp.maximum(-rel, 0.0)), 0.0))
    idx = lax.broadcasted_iota(I32, (chunk, 1), 0).astype(F32)
    xi_f = jnp.exp(lgf * (idx + 1.0))
    zeta_f = jnp.exp(lgf * (c_f - 1.0 - idx))
    xi_b = jnp.exp(lgb * (c_f - idx))
    zeta_b = jnp.exp(lgb * idx)
    decay_f = jnp.exp(lgf * c_f)
    decay_b = jnp.exp(lgb * c_f)

    def state_update(k, v, zeta, decay):
        kz_t = (k.astype(F32) * zeta).T.astype(BF16)
        upd = jnp.dot(kz_t, v, preferred_element_type=F32)
        st_scr[...] = st_scr[...] * decay + upd

    st_scr[...] = jnp.zeros_like(st_scr)

    def bwd_body(i, _):
        c = nc - 1 - i
        rows = pl.ds(pl.multiple_of(c * chunk, chunk), chunk)
        q = q_ref[rows, :]
        inter = jnp.dot(q, st_scr[...].astype(BF16), preferred_element_type=F32)
        ob_scr[rows, :] = inter * xi_b
        state_update(k_ref[rows, :], v_ref[rows, :], zeta_b, decay_b)
        return 0

    lax.fori_loop(0, nc, bwd_body, 0)

    st_scr[...] = jnp.zeros_like(st_scr)

    def fwd_body(c, _):
        rows = pl.ds(pl.multiple_of(c * chunk, chunk), chunk)
        q = q_ref[rows, :]
        k = k_ref[rows, :]
        v = v_ref[rows, :]
        scores = lax.dot_general(q, k, (((1,), (1,)), ((), ())), preferred_element_type=F32) * d_both
        o = jnp.dot(scores.astype(BF16), v, preferred_element_type=F32)
        o = o + jnp.dot(q, st_scr[...].astype(BF16), preferred_element_type=F32) * xi_f
        o = o + ob_scr[rows, :]
        state_update(k, v, zeta_f, decay_f)
        g = g_ref[rows, :].astype(F32)
        o_ref[rows, :] = (g * jax.nn.sigmoid(g) * _rms(o)).astype(BF16)
        return 0

    lax.fori_loop(0, nc, fwd_body, 0)


def _retention(proj, lg_fwd, lg_bwd, *, batch, seq, d_model, chunk):
    dk = d_model // RET_HEADS
    dv = 2 * dk
    vw = RET_HEADS * dv
    p3 = proj.reshape(batch, seq, 2 * d_model + 2 * vw)
    out = pl.pallas_call(
        functools.partial(_retention_kernel, chunk=chunk),
        grid=(batch, RET_HEADS),
        in_specs=[
            pl.BlockSpec(memory_space=pltpu.SMEM),
            pl.BlockSpec(memory_space=pltpu.SMEM),
            pl.BlockSpec((None, seq, dk), lambda b, h: (b, 0, h)),
            pl.BlockSpec((None, seq, dk), lambda b, h: (b, 0, RET_HEADS + h)),
            pl.BlockSpec((None, seq, dv), lambda b, h: (b, 0, RET_HEADS + h)),
            pl.BlockSpec((None, seq, dv), lambda b, h: (b, 0, 2 * RET_HEADS + h)),
        ],
        out_specs=pl.BlockSpec((None, seq, dv), lambda b, h: (b, 0, h)),
        out_shape=jax.ShapeDtypeStruct((batch, seq, vw), BF16),
        scratch_shapes=[pltpu.VMEM((seq, dv), F32), pltpu.VMEM((dk, dv), F32)],
        compiler_params=_params(("parallel", "parallel")),
        name="retention",
    )(lg_fwd, lg_bwd, p3, p3, p3, p3)
    return out.reshape(batch * seq, vw)


def _out_router_kernel(o_ref, w_ref, x_ref, g_ref, rwh_ref, rwl_ref, rb_ref,
                       xn_ref, hm_ref, idx_ref, wts_ref, rank_ref, cnt_ref, carry_scr):
    i = pl.program_id(0)
    tm = x_ref.shape[0]

    @pl.when(i == 0)
    def _():
        carry_scr[...] = jnp.zeros_like(carry_scr)

    xn = x_ref[...] + jnp.dot(o_ref[...], w_ref[...], preferred_element_type=F32)
    xn_ref[...] = xn
    hm = _rms(xn) * g_ref[...]
    hm_ref[...] = hm

    hm_hi = hm.astype(BF16)
    hm_lo = (hm - hm_hi.astype(F32)).astype(BF16)
    nt = (((1,), (1,)), ((), ()))
    logits = (lax.dot_general(rwh_ref[...], hm_hi, nt, preferred_element_type=F32)
              + lax.dot_general(rwh_ref[...], hm_lo, nt, preferred_element_type=F32)
              + lax.dot_general(rwl_ref[...], hm_hi, nt, preferred_element_type=F32)
              + rb_ref[...])

    eidx = lax.broadcasted_iota(I32, (N_EXPERTS, tm), 0)
    work = logits
    vals, sels, hots = [], [], []
    for _ in range(TOP_K):
        m = jnp.max(work, axis=0, keepdims=True)
        sel = jnp.min(jnp.where(work == m, eidx, N_EXPERTS), axis=0, keepdims=True)
        hot = eidx == sel
        vals.append(m)
        sels.append(sel)
        hots.append(hot)
        work = jnp.where(hot, -jnp.inf, work)
    exps = [jnp.exp(v - vals[0]) for v in vals]
    denom = exps[0] + exps[1] + exps[2] + exps[3]

    mask = jnp.where(hots[0] | hots[1] | hots[2] | hots[3], 1.0, 0.0)
    tri = (lax.broadcasted_iota(I32, (tm, tm), 0) < lax.broadcasted_iota(I32, (tm, tm), 1))
    excl = jnp.dot(mask.astype(BF16), jnp.where(tri, 1.0, 0.0).astype(BF16), preferred_element_type=F32)
    rank_e = excl + carry_scr[:, 0:1]
    for k in range(TOP_K):
        idx_ref[k:k + 1, :] = sels[k]
        wts_ref[k:k + 1, :] = exps[k] / denom
        rank_ref[k:k + 1, :] = jnp.sum(jnp.where(hots[k], rank_e, 0.0), axis=0, keepdims=True).astype(I32)
    carry_scr[...] = carry_scr[...] + jnp.sum(mask, axis=1, keepdims=True)
    cnt_ref[...] = carry_scr[...].astype(I32)


def _out_router(o, w_out_bf16, x, moe_gain, rw_hi_t, rw_lo_t, router_b, *, tm):
    t, d = x.shape
    dv = o.shape[1]
    tok = lambda i: (i, 0)
    fixed = lambda i: (0, 0)
    lanes_tok = lambda i: (0, i)
    return pl.pallas_call(
        _out_router_kernel,
        grid=(t // tm,),
        in_specs=[
            pl.BlockSpec((tm, dv), tok),
            pl.BlockSpec((dv, d), fixed),
            pl.BlockSpec((tm, d), tok),
            pl.BlockSpec((1, d), fixed),
            pl.BlockSpec((N_EXPERTS, d), fixed),
            pl.BlockSpec((N_EXPERTS, d), fixed),
            pl.BlockSpec((N_EXPERTS, 1), fixed),
        ],
        out_specs=[
            pl.BlockSpec((tm, d), tok),
            pl.BlockSpec((tm, d), tok),
            pl.BlockSpec((TOP_K, tm), lanes_tok),
            pl.BlockSpec((TOP_K, tm), lanes_tok),
            pl.BlockSpec((TOP_K, tm), lanes_tok),
            pl.BlockSpec((N_EXPERTS, LANES), fixed),
        ],
        out_shape=[
            jax.ShapeDtypeStruct((t, d), F32),
            jax.ShapeDtypeStruct((t, d), F32),
            jax.ShapeDtypeStruct((TOP_K, t), I32),
            jax.ShapeDtypeStruct((TOP_K, t), F32),
            jax.ShapeDtypeStruct((TOP_K, t), I32),
            jax.ShapeDtypeStruct((N_EXPERTS, LANES), I32),
        ],
        scratch_shapes=[pltpu.VMEM((N_EXPERTS, LANES), F32)],
        compiler_params=_params(("arbitrary",)),
        name="out_router",
    )(o, w_out_bf16, x, moe_gain.reshape(1, d), rw_hi_t, rw_lo_t, router_b.reshape(N_EXPERTS, 1))


SUBLANES = 8


def _idx_slot_copy(dest_hbm, idx_smem, idx_sem, step, s):
    n_idx = dest_hbm.shape[1]
    return pltpu.make_async_copy(dest_hbm.at[step], idx_smem.at[pl.ds(pl.multiple_of(s * n_idx, n_idx), n_idx)],
                                 idx_sem.at[s])


def _dispatch_kernel(ends_ref, padded_ref, dest_hbm, hm_hbm, xs_hbm, idx_smem, zero_scr, idx_sem, fill_sem, row_sem,
                     *, tm, tr, n_tiles):
    i = pl.program_id(0)
    n = pl.num_programs(0)
    slot = i % 2
    groups = tm // SUBLANES

    def idx_copy(step, s):
        return _idx_slot_copy(dest_hbm, idx_smem, idx_sem, step, s)

    @pl.when(i == 0)
    def _():
        idx_copy(0, 0).start()
        zero_scr[...] = jnp.zeros_like(zero_scr)

        def fill(start):
            return pltpu.make_async_copy(zero_scr, xs_hbm.at[pl.ds(pl.multiple_of(start, tr), tr), :], fill_sem)

        first_unused = ends_ref[N_EXPERTS - 1] // tr
        for e in range(N_EXPERTS):
            @pl.when(padded_ref[e] > 0)
            def _():
                fill(ends_ref[e] - tr).start()
        lax.fori_loop(first_unused, n_tiles, lambda c, _: (fill(c * tr).start(), 0)[1], 0)
        for e in range(N_EXPERTS):
            @pl.when(padded_ref[e] > 0)
            def _():
                fill(0).wait()
        lax.fori_loop(first_unused, n_tiles, lambda c, _: (fill(0).wait(), 0)[1], 0)

    @pl.when(i + 1 < n)
    def _():
        idx_copy(i + 1, 1 - slot).start()

    idx_copy(i, slot).wait()
    idx_base = slot * (TOP_K * tm)

    def start_group(g, _):
        for u in range(SUBLANES):
            src = hm_hbm.at[i * groups + g, pl.ds(u, 1), :]
            for k in range(TOP_K):
                row = idx_smem[idx_base + k * tm + g * SUBLANES + u]
                pltpu.make_async_copy(src, xs_hbm.at[pl.ds(row, 1), :], row_sem).start()
        return 0

    lax.fori_loop(0, groups, start_group, 0)

    def wait_step_rows():
        for _ in range(TOP_K):
            pltpu.make_async_copy(xs_hbm.at[pl.ds(0, tm), :], xs_hbm.at[pl.ds(0, tm), :], row_sem).wait()

    @pl.when(i > 0)
    def _():
        wait_step_rows()

    @pl.when(i == n - 1)
    def _():
        wait_step_rows()


def _dispatch(hm, dest_tiles, ends, padded, *, tm, tr, n_rows):
    t, d = hm.shape
    return pl.pallas_call(
        functools.partial(_dispatch_kernel, tm=tm, tr=tr, n_tiles=n_rows // tr),
        grid_spec=pltpu.PrefetchScalarGridSpec(
            num_scalar_prefetch=2,
            grid=(t // tm,),
            in_specs=[
                pl.BlockSpec(memory_space=pl.ANY),
                pl.BlockSpec(memory_space=pl.ANY),
            ],
            out_specs=pl.BlockSpec(memory_space=pl.ANY),
            scratch_shapes=[
                pltpu.SMEM((2 * TOP_K * tm,), I32),
                pltpu.VMEM((tr, d), F32),
                pltpu.SemaphoreType.DMA((2,)),
                pltpu.SemaphoreType.DMA(()),
                pltpu.SemaphoreType.DMA(()),
            ],
        ),
        out_shape=jax.ShapeDtypeStruct((n_rows, d), F32),
        compiler_params=_params(("arbitrary",)),
        name="moe_dispatch",
    )(ends, padded, dest_tiles, hm.reshape(t // SUBLANES, SUBLANES, d))


MXU_WIDTH = 256


def _split_w1_kernel(w_ref, perm_ref, g_ref, l_ref):
    half = MXU_WIDTH // 2
    for c in range(w_ref.shape[1] // MXU_WIDTH):
        y = jnp.dot(w_ref[:, c * MXU_WIDTH:(c + 1) * MXU_WIDTH].astype(BF16), perm_ref[...],
                    preferred_element_type=F32)
        g_ref[:, c * half:(c + 1) * half] = y[:, :half].astype(BF16)
        l_ref[:, c * half:(c + 1) * half] = y[:, half:].astype(BF16)


def _split_w1(w1):
    e, d, ff2 = w1.shape
    ff = ff2 // 2
    rows = min(512, d)
    src = jnp.arange(MXU_WIDTH)
    dst = jnp.where(src % 2 == 0, src // 2, MXU_WIDTH // 2 + src // 2)
    perm = (dst[:, None] == jnp.arange(MXU_WIDTH)[None, :]).astype(BF16)
    blk = lambda i, r: (i, r, 0)
    return pl.pallas_call(
        _split_w1_kernel,
        grid=(e, d // rows),
        in_specs=[pl.BlockSpec((None, rows, ff2), blk), pl.BlockSpec((MXU_WIDTH, MXU_WIDTH), lambda i, r: (0, 0))],
        out_specs=[pl.BlockSpec((None, rows, ff), blk), pl.BlockSpec((None, rows, ff), blk)],
        out_shape=[jax.ShapeDtypeStruct((e, d, ff), BF16), jax.ShapeDtypeStruct((e, d, ff), BF16)],
        compiler_params=_params(("parallel", "parallel")),
        name="split_w1",
    )(w1, perm)


def _experts_kernel(te_ref, tv_ref, xs_ref, w1g_ref, w1l_ref, b1g_ref, b1l_ref, w2_ref, b2_ref, ys_ref):
    i = pl.program_id(0)

    @pl.when(tv_ref[i] > 0)
    def _():
        x = xs_ref[...].astype(BF16)
        glu = jnp.dot(x, w1g_ref[...], preferred_element_type=F32) + b1g_ref[...]
        lin = jnp.dot(x, w1l_ref[...], preferred_element_type=F32) + b1l_ref[...]
        glu = jnp.minimum(glu, SWIGLU_LIMIT)
        lin = jnp.clip(lin, -SWIGLU_LIMIT, SWIGLU_LIMIT)
        act = glu * jax.nn.sigmoid(SWIGLU_ALPHA * glu) * (lin + 1.0)
        ys_ref[...] = jnp.dot(act.astype(BF16), w2_ref[...], preferred_element_type=F32) + b2_ref[...]

    @pl.when(tv_ref[i] == 0)
    def _():
        ys_ref[...] = jnp.zeros_like(ys_ref)


def _experts(xs, tile_expert, tile_valid, w1g, w1l, b1g, b1l, w2, b2, *, tr):
    n_rows, d = xs.shape
    ff = w1g.shape[2]
    n_tiles = n_rows // tr
    row_tile = lambda i, te, tv: (i, 0)
    by_expert = lambda i, te, tv: (te[i], 0, 0)
    return pl.pallas_call(
        _experts_kernel,
        grid_spec=pltpu.PrefetchScalarGridSpec(
            num_scalar_prefetch=2,
            grid=(n_tiles,),
            in_specs=[
                pl.BlockSpec((tr, d), row_tile),
                pl.BlockSpec((None, d, ff), by_expert),
                pl.BlockSpec((None, d, ff), by_expert),
                pl.BlockSpec((None, 1, ff), by_expert),
                pl.BlockSpec((None, 1, ff), by_expert),
                pl.BlockSpec((None, ff, d), by_expert),
                pl.BlockSpec((None, 1, d), by_expert),
            ],
            out_specs=pl.BlockSpec((tr, d), row_tile),
        ),
        out_shape=jax.ShapeDtypeStruct((n_rows, d), F32),
        compiler_params=_params(("arbitrary",)),
        name="moe_experts",
    )(tile_expert, tile_valid, xs, w1g, w1l, b1g, b1l, w2, b2)


def _combine_kernel(dest_hbm, ys_hbm, x_ref, wt_ref, o_ref, idx_smem, rows_scr, idx_sem, row_sem, *, tm):
    i = pl.program_id(0)
    n = pl.num_programs(0)
    slot = i % 2
    groups = TOP_K * tm // SUBLANES

    def idx_copy(step, s):
        return _idx_slot_copy(dest_hbm, idx_smem, idx_sem, step, s)

    def gather_rows(s):
        idx_base = s * (TOP_K * tm)

        def start_group(g, _):
            for u in range(SUBLANES):
                row = idx_smem[idx_base + g * SUBLANES + u]
                pltpu.make_async_copy(ys_hbm.at[pl.ds(row, 1), :], rows_scr.at[s, g, pl.ds(u, 1), :],
                                      row_sem.at[s]).start()
            return 0

        lax.fori_loop(0, groups, start_group, 0)

    @pl.when(i == 0)
    def _():
        idx_copy(0, 0).start()
        idx_copy(0, 0).wait()
        gather_rows(0)

        @pl.when(n > 1)
        def _():
            idx_copy(1, 1).start()

    @pl.when(i + 1 < n)
    def _():
        idx_copy(i + 1, 1 - slot).wait()
        gather_rows(1 - slot)

    @pl.when(i + 2 < n)
    def _():
        idx_copy(i + 2, slot).start()

    per_k = tm // SUBLANES
    for k in range(TOP_K):
        part = rows_scr.at[slot, pl.ds(k * per_k, per_k)]
        pltpu.make_async_copy(part, part, row_sem.at[slot]).wait()

    out = x_ref[...]
    for k in range(TOP_K):
        rows = rows_scr[slot, k * per_k:(k + 1) * per_k].reshape(tm, x_ref.shape[1])
        out = out + wt_ref[:, k:k + 1] * rows
    o_ref[...] = out


def _combine(ys, dest_tiles, x, wts_t, *, tm):
    t, d = x.shape
    return pl.pallas_call(
        functools.partial(_combine_kernel, tm=tm),
        grid=(t // tm,),
        in_specs=[
            pl.BlockSpec(memory_space=pl.ANY),
            pl.BlockSpec(memory_space=pl.ANY),
            pl.BlockSpec((tm, d), lambda i: (i, 0)),
            pl.BlockSpec((tm, TOP_K), lambda i: (i, 0)),
        ],
        out_specs=pl.BlockSpec((tm, d), lambda i: (i, 0)),
        out_shape=jax.ShapeDtypeStruct((t, d), F32),
        scratch_shapes=[
            pltpu.SMEM((2 * TOP_K * tm,), I32),
            pltpu.VMEM((2, TOP_K * tm // SUBLANES, SUBLANES, d), F32),
            pltpu.SemaphoreType.DMA((2,)),
            pltpu.SemaphoreType.DMA((2,)),
        ],
        compiler_params=_params(("arbitrary",)),
        name="moe_combine",
    )(dest_tiles, ys, x, wts_t)


def _tiles(batch, seq):
    t = batch * seq
    tm = min(512, t)
    return dict(
        tm=tm,
        tn=256,
        tq=min(256, seq),
        tk=min(512, seq),
        chunk=min(128, seq),
        tmd=min(256, t),
        tr=min(512, max(8, TOP_K * t // N_EXPERTS)),
    )


def _moe(o, w_out, x, moe_gain, router_w, router_b, w1, b1, w2, b2, *, cfg):
    t, d = x.shape
    tm, tmd, tr = cfg["tm"], cfg["tmd"], cfg["tr"]
    rw_t = router_w.T
    rw_hi = rw_t.astype(BF16)
    rw_lo = (rw_t - rw_hi.astype(F32)).astype(BF16)
    x, hm, idx, wts, rank, cnt = _out_router(o, w_out.astype(BF16), x, moe_gain, rw_hi, rw_lo, router_b, tm=tm)

    counts = cnt[:, 0]
    padded = ((counts + tr - 1) // tr) * tr
    ends = jnp.cumsum(padded).astype(I32)
    offs = ends - padded
    experts = jnp.arange(N_EXPERTS, dtype=I32)[:, None, None]
    dest = rank + jnp.sum(jnp.where(idx[None] == experts, offs[:, None, None], 0), axis=0)
    dest_tiles = dest.reshape(TOP_K, t // tmd, tmd).transpose(1, 0, 2).reshape(t // tmd, TOP_K * tmd)
    n_rows = TOP_K * t + N_EXPERTS * tr
    tile_start = jnp.arange(n_rows // tr, dtype=I32) * tr
    tile_expert = jnp.minimum(jnp.sum((tile_start[:, None] >= ends[None, :]).astype(I32), axis=1), N_EXPERTS - 1)
    tile_valid = (tile_start < ends[-1]).astype(I32)

    xs = _dispatch(hm, dest_tiles, ends, padded.astype(I32), tm=tmd, tr=tr, n_rows=n_rows)
    ff = w2.shape[1]
    w1g, w1l = _split_w1(w1)
    ys = _experts(xs, tile_expert, tile_valid, w1g, w1l,
                  b1[:, 0::2].reshape(N_EXPERTS, 1, ff), b1[:, 1::2].reshape(N_EXPERTS, 1, ff),
                  w2.astype(BF16), b2.reshape(N_EXPERTS, 1, d), tr=tr)
    return _combine(ys, dest_tiles, x, wts.T, tm=tmd)


def kernel(x, diff_norm, diff_w_in, diff_w_out, diff_lambda_q1, diff_lambda_k1, diff_lambda_q2, diff_lambda_k2, diff_q_norm, diff_k_norm, diff_subln, ret_norm, ret_w_in, ret_w_out, ret_log_decay_fwd, ret_log_decay_bwd, moe_norm, moe_router_w, moe_router_b, moe_w1, moe_b1, moe_w2, moe_b2):
    batch, seq, d = x.shape
    depth = moe_norm.shape[0]
    cfg = _tiles(batch, seq)
    x = x.reshape(batch * seq, d)
    slopes = jnp.exp2(-8.0 * jnp.arange(1, DIFF_HEADS + 1, dtype=F32) / DIFF_HEADS)
    for i in range(depth):
        j = i // N_MIXERS
        if i % N_MIXERS == 0:
            lambda_init = _lambda_init(i)
            lam = (jnp.exp(jnp.sum(diff_lambda_q1[j] * diff_lambda_k1[j]))
                   - jnp.exp(jnp.sum(diff_lambda_q2[j] * diff_lambda_k2[j])) + lambda_init)
            reps = d // DIFF_HEAD_DIM
            qk_gain = jnp.concatenate([jnp.tile(diff_q_norm[j], reps) * (DIFF_HEAD_DIM ** -0.5 * LOG2E),
                                       jnp.tile(diff_k_norm[j], reps)]).reshape(1, 2 * d)
            qkv = _in_proj(x, diff_norm[j], diff_w_in[j].astype(BF16), tm=cfg["tm"], tn=cfg["tn"],
                           qk_gain_row=qk_gain)
            o = _diff_attention(qkv, slopes, lam, diff_subln[j], lambda_init,
                                batch=batch, seq=seq, tq=cfg["tq"], tk=cfg["tk"])
            w_out = diff_w_out[j]
        else:
            dk = d // RET_HEADS
            col_scale = jnp.ones((ret_w_in.shape[2],), F32).at[d:2 * d].set(dk ** -0.5)
            proj = _in_proj(x, ret_norm[j], (ret_w_in[j] * col_scale).astype(BF16), tm=cfg["tm"], tn=cfg["tn"])
            o = _retention(proj, ret_log_decay_fwd[j], ret_log_decay_bwd[j],
                           batch=batch, seq=seq, d_model=d, chunk=cfg["chunk"])
            w_out = ret_w_out[j]
        x = _moe(o, w_out, x, moe_norm[i], moe_router_w[i], moe_router_b[i],
                 moe_w1[i], moe_b1[i], moe_w2[i], moe_b2[i], cfg=cfg)
    return x.reshape(batch, seq, d)
```

```python
import functools
import math

import jax
import jax.numpy as jnp
from jax import lax
from jax.experimental import pallas as pl
from jax.experimental.pallas import tpu as pltpu

F32 = jnp.float32
BF16 = jnp.bfloat16
I32 = jnp.int32

NORM_EPS = 1e-5
N_MIXERS = 2

DIFF_HEADS = 8
DIFF_HEAD_DIM = 64
DIFF_V_DIM = 2 * DIFF_HEAD_DIM

RET_HEADS = 4

N_EXPERTS = 32
TOP_K = 4
SWIGLU_ALPHA = 1.702
SWIGLU_LIMIT = 7.0

V7X_VMEM_LIMIT_BYTES = 56 * 1024 * 1024
LANES = 128


def _lambda_init(layer_idx):
    return 0.8 - 0.6 * math.exp(-0.3 * layer_idx)


def _params(semantics):
    return pltpu.CompilerParams(dimension_semantics=semantics, vmem_limit_bytes=V7X_VMEM_LIMIT_BYTES)


def _rms(x, eps=NORM_EPS):
    return x * lax.rsqrt(jnp.mean(x * x, axis=-1, keepdims=True) + eps)


def _in_proj_kernel(x_ref, g_ref, w_ref, gmat_ref, qkg_ref, o_ref, h_scr, *, tn, n_normed):
    h_scr[...] = (_rms(x_ref[...]) * g_ref[...]).astype(BF16)
    n_total = w_ref.shape[1]
    for c in range(n_total // tn):
        cols = slice(c * tn, (c + 1) * tn)
        acc = jnp.dot(h_scr[...], w_ref[:, cols], preferred_element_type=F32)
        if c * tn < n_normed:
            sq = acc * acc
            hi = sq.astype(BF16)
            lo = (sq - hi.astype(F32)).astype(BF16)
            msq = (jnp.dot(hi, gmat_ref[...], preferred_element_type=F32)
                   + jnp.dot(lo, gmat_ref[...], preferred_element_type=F32))
            acc = acc * lax.rsqrt(msq + NORM_EPS) * qkg_ref[:, cols]
        o_ref[:, cols] = acc.astype(BF16)


def _in_proj(x, gain, w_bf16, *, tm, tn, qk_gain_row=None):
    t, d = x.shape
    n = w_bf16.shape[1]
    if qk_gain_row is None:
        n_normed = 0
        qk_gain_row = jnp.zeros((1, n), F32)
    else:
        n_normed = qk_gain_row.shape[1]
        qk_gain_row = jnp.pad(qk_gain_row, ((0, 0), (0, n - n_normed)))
    grp = jnp.arange(tn) // DIFF_HEAD_DIM
    gmat = jnp.where(grp[:, None] == grp[None, :], 1.0 / DIFF_HEAD_DIM, 0.0).astype(BF16)
    return pl.pallas_call(
        functools.partial(_in_proj_kernel, tn=tn, n_normed=n_normed),
        grid=(t // tm,),
        in_specs=[
            pl.BlockSpec((tm, d), lambda i: (i, 0)),
            pl.BlockSpec((1, d), lambda i: (0, 0)),
            pl.BlockSpec((d, n), lambda i: (0, 0)),
            pl.BlockSpec((tn, tn), lambda i: (0, 0)),
            pl.BlockSpec((1, n), lambda i: (0, 0)),
        ],
        out_specs=pl.BlockSpec((tm, n), lambda i: (i, 0)),
        out_shape=jax.ShapeDtypeStruct((t, n), BF16),
        scratch_shapes=[pltpu.VMEM((tm, d), BF16)],
        compiler_params=_params(("parallel",)),
        name="in_proj",
    )(x, gain.reshape(1, d), w_bf16, gmat, qk_gain_row)


LOG2E = math.log2(math.e)
LOG2E_BF16_PARTS = (1.4453125, -0.00262451171875, 7.063150405883789e-06)
ALIBI_LOW_BITS = 255
ALIBI_HIGH_BIT = 256


def _alibi_key_features(slopes, seq, tk):
    dj = jnp.arange(seq, dtype=I32) % tk
    lo = (dj & ALIBI_LOW_BITS).astype(F32)
    hi = (dj & ALIBI_HIGH_BIT).astype(F32)
    lane = jnp.arange(LANES)
    feat = jnp.where(lane[None, :] < 3, lo[:, None], jnp.where(lane[None, :] < 6, hi[:, None], 0.0))
    return (slopes[:, None, None] * feat[None]).astype(BF16)


def _diff_attn_kernel(slopes_ref, lam_ref, q_ref, k_ref, vt_ref, fk_ref, sg_ref, o_ref,
                      qa_scr, s_scr, mt_scr, p_scr, a_scr, m_scr, l_scr, acc_scr, *, tq, tk, out_scale):
    h = pl.program_id(1)
    qi = pl.program_id(2)
    slope2 = slopes_ref[h] * LOG2E
    lam = lam_ref[0]
    nk = k_ref.shape[0] // tk
    i0 = qi * tq
    jd = i0 // tk
    cols2 = 2 * tq

    q = q_ref[...]
    lane = lax.broadcasted_iota(I32, q.shape, 1)
    zero = jnp.zeros_like(q)
    qq = jnp.concatenate([jnp.where(lane < DIFF_HEAD_DIM, q, zero),
                          jnp.where(lane >= DIFF_HEAD_DIM, q, zero)], axis=0)
    flane = lax.broadcasted_iota(I32, (cols2, LANES), 1)
    a, b, c = LOG2E_BF16_PARTS
    feat = jnp.where(flane % 3 == 0, a, jnp.where(flane % 3 == 1, b, c))
    feat = jnp.where(flane < 6, feat, 0.0)
    qa_scr[0] = jnp.concatenate([qq, feat.astype(BF16)], axis=1)
    qa_scr[1] = jnp.concatenate([qq, (-feat).astype(BF16)], axis=1)
    col = lax.broadcasted_iota(I32, (1, cols2), 1)
    q_pos = (i0 + jnp.where(col >= tq, col - tq, col)).astype(F32)

    m_scr[...] = jnp.full_like(m_scr, -jnp.inf)
    l_scr[...] = jnp.zeros_like(l_scr)
    acc_scr[...] = jnp.zeros_like(acc_scr)
    nt = (((1,), (1,)), ((), ()))

    def key_rows(j):
        return pl.ds(pl.multiple_of(j * tk, tk), tk)

    def exponentials(slot, col_shift):
        for cb in range(cols2 // LANES):
            cs = slice(cb * LANES, (cb + 1) * LANES)
            shift = col_shift[:, cs]
            m_old = m_scr[:, cs]
            m_new = jnp.maximum(m_old, mt_scr[slot, :, cs] - shift)
            alpha = jnp.exp2(m_old - m_new)
            p = jnp.exp2(s_scr[slot, :, cs] - (m_new + shift))
            l_scr[:, cs] = alpha * l_scr[:, cs] + jnp.sum(p, axis=0, keepdims=True)
            m_scr[:, cs] = m_new
            a_scr[slot, :, cs] = alpha
            p_scr[slot, :, cs] = p.astype(BF16)

    def values(slot, j):
        acc_scr[...] = (a_scr[slot] * acc_scr[...]
                        + jnp.dot(vt_ref[:, key_rows(j)], p_scr[slot], preferred_element_type=F32))

    s = lax.dot_general(k_ref[key_rows(jd), :], qq, nt, preferred_element_type=F32)
    key_pos = (jd * tk + lax.broadcasted_iota(I32, (tk, 1), 0)).astype(F32)
    s = s - slope2 * jnp.abs(key_pos - q_pos)
    s_scr[0] = s
    mt_scr[0] = jnp.max(s, axis=0, keepdims=True)
    exponentials(0, jnp.zeros((1, cols2), F32))
    values(0, jd)

    n_lin = nk - 1

    def tile_of(t):
        return t + (t >= jd).astype(I32)

    def scores(t, slot):
        j = tile_of(jnp.minimum(t, n_lin - 1))
        k_aug = jnp.concatenate([k_ref[key_rows(j), :], fk_ref[key_rows(j), :]], axis=1)
        s = lax.dot_general(k_aug, qa_scr[(j > jd).astype(I32)], nt, preferred_element_type=F32)
        s_scr[slot] = s
        mt_scr[slot] = jnp.max(s, axis=0, keepdims=True)

    def shift_of(t):
        j = tile_of(t)
        d = slope2 * (q_pos - (j * tk).astype(F32))
        return jnp.where(j > jd, -d, d)

    if n_lin > 0:
        p_scr[1] = jnp.zeros(p_scr.shape[1:], BF16)
        a_scr[1] = jnp.ones(a_scr.shape[1:], F32)
        scores(0, 0)

        def stage(t, slot):
            scores(t + 1, 1 - slot)
            values(1 - slot, tile_of(jnp.maximum(t - 1, 0)))
            exponentials(slot, shift_of(t))

        def pair(tt, _):
            stage(2 * tt, 0)
            stage(2 * tt + 1, 1)
            return 0

        lax.fori_loop(0, n_lin // 2, pair, 0)
        if n_lin % 2:
            stage(jnp.int32(n_lin - 1), 0)
        values((n_lin - 1) % 2, tile_of(jnp.int32(n_lin - 1)))

    o = acc_scr[...] / l_scr[...]
    o = o[:, :tq] - lam * o[:, tq:]
    o = o * lax.rsqrt(jnp.mean(o * o, axis=0, keepdims=True) + NORM_EPS)
    o_ref[...] = (o.T * sg_ref[...] * out_scale).astype(BF16)


def _diff_attention(qkv, slopes, lam, subln_gain, lambda_init, *, batch, seq, tq, tk):
    assert tk % tq == 0 and tk <= 2 * ALIBI_HIGH_BIT and tq % LANES == 0
    d = DIFF_HEADS * DIFF_V_DIM
    qkv3 = qkv.reshape(batch, seq, 3 * d)
    vt = jnp.swapaxes(qkv3[:, :, 2 * d:], 1, 2)
    fk = _alibi_key_features(slopes, seq, tk)
    out = pl.pallas_call(
        functools.partial(_diff_attn_kernel, tq=tq, tk=tk, out_scale=1.0 - lambda_init),
        grid=(batch, DIFF_HEADS, seq // tq),
        in_specs=[
            pl.BlockSpec(memory_space=pltpu.SMEM),
            pl.BlockSpec(memory_space=pltpu.SMEM),
            pl.BlockSpec((None, tq, DIFF_V_DIM), lambda b, h, i: (b, i, h)),
            pl.BlockSpec((None, seq, DIFF_V_DIM), lambda b, h, i: (b, 0, DIFF_HEADS + h)),
            pl.BlockSpec((None, DIFF_V_DIM, seq), lambda b, h, i: (b, h, 0)),
            pl.BlockSpec((None, seq, LANES), lambda b, h, i: (h, 0, 0)),
            pl.BlockSpec((1, DIFF_V_DIM), lambda b, h, i: (0, 0)),
        ],
        out_specs=pl.BlockSpec((None, tq, DIFF_V_DIM), lambda b, h, i: (b, i, h)),
        out_shape=jax.ShapeDtypeStruct((batch, seq, d), BF16),
        scratch_shapes=[
            pltpu.VMEM((2, 2 * tq, 2 * LANES), BF16),
            pltpu.VMEM((2, tk, 2 * tq), F32),
            pltpu.VMEM((2, 1, 2 * tq), F32),
            pltpu.VMEM((2, tk, 2 * tq), BF16),
            pltpu.VMEM((2, 1, 2 * tq), F32),
            pltpu.VMEM((1, 2 * tq), F32),
            pltpu.VMEM((1, 2 * tq), F32),
            pltpu.VMEM((DIFF_V_DIM, 2 * tq), F32),
        ],
        compiler_params=_params(("parallel", "parallel", "parallel")),
        name="diff_attn",
    )(slopes, lam.reshape(1), qkv3, qkv3, vt, fk, subln_gain.reshape(1, DIFF_V_DIM))
    return out.reshape(batch * seq, d)


def _retention_kernel(lgf_ref, lgb_ref, q_ref, k_ref, v_ref, g_ref, o_ref, ob_scr, st_scr, *, chunk):
    h = pl.program_id(1)
    lgf = lgf_ref[h]
    lgb = lgb_ref[h]
    seq = q_ref.shape[0]
    nc = seq // chunk
    c_f = float(chunk)

    row = lax.broadcasted_iota(I32, (chunk, chunk), 0)
    col = lax.broadcasted_iota(I32, (chunk, chunk), 1)
    rel = (row - col).astype(F32)
    d_both = (jnp.where(rel >= 0, jnp.exp(lgf * jnp.maximum(rel, 0.0)), 0.0)
              + jnp.where(rel < 0, jnp.exp(lgb * jnp.maximum(-rel, 0.0)), 0.0))
    idx = lax.broadcasted_iota(I32, (chunk, 1), 0).astype(F32)
    xi_f = jnp.exp(lgf * (idx + 1.0))
    zeta_f = jnp.exp(lgf * (c_f - 1.0 - idx))
    xi_b = jnp.exp(lgb * (c_f - idx))
    zeta_b = jnp.exp(lgb * idx)
    decay_f = jnp.exp(lgf * c_f)
    decay_b = jnp.exp(lgb * c_f)

    def state_update(k, v, zeta, decay):
        kz_t = (k.astype(F32) * zeta).T.astype(BF16)
        upd = jnp.dot(kz_t, v, preferred_element_type=F32)
        st_scr[...] = st_scr[...] * decay + upd

    st_scr[...] = jnp.zeros_like(st_scr)

    def bwd_body(i, _):
        c = nc - 1 - i
        rows = pl.ds(pl.multiple_of(c * chunk, chunk), chunk)
        q = q_ref[rows, :]
        inter = jnp.dot(q, st_scr[...].astype(BF16), preferred_element_type=F32)
        ob_scr[rows, :] = inter * xi_b
        state_update(k_ref[rows, :], v_ref[rows, :], zeta_b, decay_b)
        return 0

    lax.fori_loop(0, nc, bwd_body, 0)

    st_scr[...] = jnp.zeros_like(st_scr)

    def fwd_body(c, _):
        rows = pl.ds(pl.multiple_of(c * chunk, chunk), chunk)
        q = q_ref[rows, :]
        k = k_ref[rows, :]
        v = v_ref[rows, :]
        scores = lax.dot_general(q, k, (((1,), (1,)), ((), ())), preferred_element_type=F32) * d_both
        o = jnp.dot(scores.astype(BF16), v, preferred_element_type=F32)
        o = o + jnp.dot(q, st_scr[...].astype(BF16), preferred_element_type=F32) * xi_f
        o = o + ob_scr[rows, :]
        state_update(k, v, zeta_f, decay_f)
        g = g_ref[rows, :].astype(F32)
        o_ref[rows, :] = (g * jax.nn.sigmoid(g) * _rms(o)).astype(BF16)
        return 0

    lax.fori_loop(0, nc, fwd_body, 0)


def _retention(proj, lg_fwd, lg_bwd, *, batch, seq, d_model, chunk):
    dk = d_model // RET_HEADS
    dv = 2 * dk
    vw = RET_HEADS * dv
    p3 = proj.reshape(batch, seq, 2 * d_model + 2 * vw)
    out = pl.pallas_call(
        functools.partial(_retention_kernel, chunk=chunk),
        grid=(batch, RET_HEADS),
        in_specs=[
            pl.BlockSpec(memory_space=pltpu.SMEM),
            pl.BlockSpec(memory_space=pltpu.SMEM),
            pl.BlockSpec((None, seq, dk), lambda b, h: (b, 0, h)),
            pl.BlockSpec((None, seq, dk), lambda b, h: (b, 0, RET_HEADS + h)),
            pl.BlockSpec((None, seq, dv), lambda b, h: (b, 0, RET_HEADS + h)),
            pl.BlockSpec((None, seq, dv), lambda b, h: (b, 0, 2 * RET_HEADS + h)),
        ],
        out_specs=pl.BlockSpec((None, seq, dv), lambda b, h: (b, 0, h)),
        out_shape=jax.ShapeDtypeStruct((batch, seq, vw), BF16),
        scratch_shapes=[pltpu.VMEM((seq, dv), F32), pltpu.VMEM((dk, dv), F32)],
        compiler_params=_params(("parallel", "parallel")),
        name="retention",
    )(lg_fwd, lg_bwd, p3, p3, p3, p3)
    return out.reshape(batch * seq, vw)


def _out_router_kernel(o_ref, w_ref, x_ref, g_ref, rwh_ref, rwl_ref, rb_ref,
                       xn_ref, hm_ref, idx_ref, wts_ref, rank_ref, cnt_ref, carry_scr):
    i = pl.program_id(0)
    tm = x_ref.shape[0]

    @pl.when(i == 0)
    def _():
        carry_scr[...] = jnp.zeros_like(carry_scr)

    xn = x_ref[...] + jnp.dot(o_ref[...], w_ref[...], preferred_element_type=F32)
    xn_ref[...] = xn
    hm = _rms(xn) * g_ref[...]
    hm_ref[...] = hm

    hm_hi = hm.astype(BF16)
    hm_lo = (hm - hm_hi.astype(F32)).astype(BF16)
    nt = (((1,), (1,)), ((), ()))
    logits = (lax.dot_general(rwh_ref[...], hm_hi, nt, preferred_element_type=F32)
              + lax.dot_general(rwh_ref[...], hm_lo, nt, preferred_element_type=F32)
              + lax.dot_general(rwl_ref[...], hm_hi, nt, preferred_element_type=F32)
              + rb_ref[...])

    eidx = lax.broadcasted_iota(I32, (N_EXPERTS, tm), 0)
    work = logits
    vals, sels, hots = [], [], []
    for _ in range(TOP_K):
        m = jnp.max(work, axis=0, keepdims=True)
        sel = jnp.min(jnp.where(work == m, eidx, N_EXPERTS), axis=0, keepdims=True)
        hot = eidx == sel
        vals.append(m)
        sels.append(sel)
        hots.append(hot)
        work = jnp.where(hot, -jnp.inf, work)
    exps = [jnp.exp(v - vals[0]) for v in vals]
    denom = exps[0] + exps[1] + exps[2] + exps[3]

    mask = jnp.where(hots[0] | hots[1] | hots[2] | hots[3], 1.0, 0.0)
    tri = (lax.broadcasted_iota(I32, (tm, tm), 0) < lax.broadcasted_iota(I32, (tm, tm), 1))
    excl = jnp.dot(mask.astype(BF16), jnp.where(tri, 1.0, 0.0).astype(BF16), preferred_element_type=F32)
    rank_e = excl + carry_scr[:, 0:1]
    for k in range(TOP_K):
        idx_ref[k:k + 1, :] = sels[k]
        wts_ref[k:k + 1, :] = exps[k] / denom
        rank_ref[k:k + 1, :] = jnp.sum(jnp.where(hots[k], rank_e, 0.0), axis=0, keepdims=True).astype(I32)
    carry_scr[...] = carry_scr[...] + jnp.sum(mask, axis=1, keepdims=True)
    cnt_ref[...] = carry_scr[...].astype(I32)


def _out_router(o, w_out_bf16, x, moe_gain, rw_hi_t, rw_lo_t, router_b, *, tm):
    t, d = x.shape
    dv = o.shape[1]
    tok = lambda i: (i, 0)
    fixed = lambda i: (0, 0)
    lanes_tok = lambda i: (0, i)
    return pl.pallas_call(
        _out_router_kernel,
        grid=(t // tm,),
        in_specs=[
            pl.BlockSpec((tm, dv), tok),
            pl.BlockSpec((dv, d), fixed),
            pl.BlockSpec((tm, d), tok),
            pl.BlockSpec((1, d), fixed),
            pl.BlockSpec((N_EXPERTS, d), fixed),
            pl.BlockSpec((N_EXPERTS, d), fixed),
            pl.BlockSpec((N_EXPERTS, 1), fixed),
        ],
        out_specs=[
            pl.BlockSpec((tm, d), tok),
            pl.BlockSpec((tm, d), tok),
            pl.BlockSpec((TOP_K, tm), lanes_tok),
            pl.BlockSpec((TOP_K, tm), lanes_tok),
            pl.BlockSpec((TOP_K, tm), lanes_tok),
            pl.BlockSpec((N_EXPERTS, LANES), fixed),
        ],
        out_shape=[
            jax.ShapeDtypeStruct((t, d), F32),
            jax.ShapeDtypeStruct((t, d), F32),
            jax.ShapeDtypeStruct((TOP_K, t), I32),
            jax.ShapeDtypeStruct((TOP_K, t), F32),
            jax.ShapeDtypeStruct((TOP_K, t), I32),
            jax.ShapeDtypeStruct((N_EXPERTS, LANES), I32),
        ],
        scratch_shapes=[pltpu.VMEM((N_EXPERTS, LANES), F32)],
        compiler_params=_params(("arbitrary",)),
        name="out_router",
    )(o, w_out_bf16, x, moe_gain.reshape(1, d), rw_hi_t, rw_lo_t, router_b.reshape(N_EXPERTS, 1))


SUBLANES = 8


def _idx_slot_copy(dest_hbm, idx_smem, idx_sem, step, s):
    n_idx = dest_hbm.shape[1]
    return pltpu.make_async_copy(dest_hbm.at[step], idx_smem.at[pl.ds(pl.multiple_of(s * n_idx, n_idx), n_idx)],
                                 idx_sem.at[s])


def _dispatch_kernel(ends_ref, padded_ref, dest_hbm, hm_ref, xs_hbm, idx_smem, zero_scr, idx_sem, fill_sem, row_sem,
                     *, tm, tr, n_tiles):
    i = pl.program_id(0)
    n = pl.num_programs(0)
    slot = i % 2
    groups = tm // SUBLANES

    def idx_copy(step, s):
        return _idx_slot_copy(dest_hbm, idx_smem, idx_sem, step, s)

    @pl.when(i == 0)
    def _():
        idx_copy(0, 0).start()
        zero_scr[...] = jnp.zeros_like(zero_scr)

        def fill(start):
            return pltpu.make_async_copy(zero_scr, xs_hbm.at[pl.ds(pl.multiple_of(start, tr), tr), :], fill_sem)

        first_unused = ends_ref[N_EXPERTS - 1] // tr
        for e in range(N_EXPERTS):
            @pl.when(padded_ref[e] > 0)
            def _():
                fill(ends_ref[e] - tr).start()
        lax.fori_loop(first_unused, n_tiles, lambda c, _: (fill(c * tr).start(), 0)[1], 0)
        for e in range(N_EXPERTS):
            @pl.when(padded_ref[e] > 0)
            def _():
                fill(0).wait()
        lax.fori_loop(first_unused, n_tiles, lambda c, _: (fill(0).wait(), 0)[1], 0)

    @pl.when(i + 1 < n)
    def _():
        idx_copy(i + 1, 1 - slot).start()

    idx_copy(i, slot).wait()
    idx_base = slot * (TOP_K * tm)

    def start_group(g, _):
        for u in range(SUBLANES):
            src = hm_ref.at[g, pl.ds(u, 1), :]
            for k in range(TOP_K):
                row = idx_smem[idx_base + k * tm + g * SUBLANES + u]
                pltpu.make_async_copy(src, xs_hbm.at[pl.ds(row, 1), :], row_sem).start()
        return 0

    lax.fori_loop(0, groups, start_group, 0)

    for _ in range(TOP_K):
        pltpu.make_async_copy(xs_hbm.at[pl.ds(0, tm), :], xs_hbm.at[pl.ds(0, tm), :], row_sem).wait()


def _dispatch(hm, dest_tiles, ends, padded, *, tm, tr, n_rows):
    t, d = hm.shape
    return pl.pallas_call(
        functools.partial(_dispatch_kernel, tm=tm, tr=tr, n_tiles=n_rows // tr),
        grid_spec=pltpu.PrefetchScalarGridSpec(
            num_scalar_prefetch=2,
            grid=(t // tm,),
            in_specs=[
                pl.BlockSpec(memory_space=pl.ANY),
                pl.BlockSpec((tm // SUBLANES, SUBLANES, d), lambda i, *_: (i, 0, 0)),
            ],
            out_specs=pl.BlockSpec(memory_space=pl.ANY),
            scratch_shapes=[
                pltpu.SMEM((2 * TOP_K * tm,), I32),
                pltpu.VMEM((tr, d), F32),
                pltpu.SemaphoreType.DMA((2,)),
                pltpu.SemaphoreType.DMA(()),
                pltpu.SemaphoreType.DMA(()),
            ],
        ),
        out_shape=jax.ShapeDtypeStruct((n_rows, d), F32),
        compiler_params=_params(("arbitrary",)),
        name="moe_dispatch",
    )(ends, padded, dest_tiles, hm.reshape(t // SUBLANES, SUBLANES, d))


MXU_WIDTH = 256


def _split_w1_kernel(w_ref, perm_ref, g_ref, l_ref):
    half = MXU_WIDTH // 2
    for c in range(w_ref.shape[1] // MXU_WIDTH):
        y = jnp.dot(w_ref[:, c * MXU_WIDTH:(c + 1) * MXU_WIDTH].astype(BF16), perm_ref[...],
                    preferred_element_type=F32)
        g_ref[:, c * half:(c + 1) * half] = y[:, :half].astype(BF16)
        l_ref[:, c * half:(c + 1) * half] = y[:, half:].astype(BF16)


def _split_w1(w1):
    e, d, ff2 = w1.shape
    ff = ff2 // 2
    rows = min(512, d)
    src = jnp.arange(MXU_WIDTH)
    dst = jnp.where(src % 2 == 0, src // 2, MXU_WIDTH // 2 + src // 2)
    perm = (dst[:, None] == jnp.arange(MXU_WIDTH)[None, :]).astype(BF16)
    blk = lambda i, r: (i, r, 0)
    return pl.pallas_call(
        _split_w1_kernel,
        grid=(e, d // rows),
        in_specs=[pl.BlockSpec((None, rows, ff2), blk), pl.BlockSpec((MXU_WIDTH, MXU_WIDTH), lambda i, r: (0, 0))],
        out_specs=[pl.BlockSpec((None, rows, ff), blk), pl.BlockSpec((None, rows, ff), blk)],
        out_shape=[jax.ShapeDtypeStruct((e, d, ff), BF16), jax.ShapeDtypeStruct((e, d, ff), BF16)],
        compiler_params=_params(("parallel", "parallel")),
        name="split_w1",
    )(w1, perm)


def _experts_kernel(te_ref, tv_ref, xs_ref, w1g_ref, w1l_ref, b1g_ref, b1l_ref, w2_ref, b2_ref, ys_ref):
    i = pl.program_id(0)

    @pl.when(tv_ref[i] > 0)
    def _():
        x = xs_ref[...].astype(BF16)
        glu = jnp.dot(x, w1g_ref[...], preferred_element_type=F32) + b1g_ref[...]
        lin = jnp.dot(x, w1l_ref[...], preferred_element_type=F32) + b1l_ref[...]
        glu = jnp.minimum(glu, SWIGLU_LIMIT)
        lin = jnp.clip(lin, -SWIGLU_LIMIT, SWIGLU_LIMIT)
        act = glu * jax.nn.sigmoid(SWIGLU_ALPHA * glu) * (lin + 1.0)
        ys_ref[...] = jnp.dot(act.astype(BF16), w2_ref[...], preferred_element_type=F32) + b2_ref[...]

    @pl.when(tv_ref[i] == 0)
    def _():
        ys_ref[...] = jnp.zeros_like(ys_ref)


def _experts(xs, tile_expert, tile_valid, w1g, w1l, b1g, b1l, w2, b2, *, tr):
    n_rows, d = xs.shape
    ff = w1g.shape[2]
    n_tiles = n_rows // tr
    row_tile = lambda i, te, tv: (i, 0)
    by_expert = lambda i, te, tv: (te[i], 0, 0)
    return pl.pallas_call(
        _experts_kernel,
        grid_spec=pltpu.PrefetchScalarGridSpec(
            num_scalar_prefetch=2,
            grid=(n_tiles,),
            in_specs=[
                pl.BlockSpec((tr, d), row_tile),
                pl.BlockSpec((None, d, ff), by_expert),
                pl.BlockSpec((None, d, ff), by_expert),
                pl.BlockSpec((None, 1, ff), by_expert),
                pl.BlockSpec((None, 1, ff), by_expert),
                pl.BlockSpec((None, ff, d), by_expert),
                pl.BlockSpec((None, 1, d), by_expert),
            ],
            out_specs=pl.BlockSpec((tr, d), row_tile),
        ),
        out_shape=jax.ShapeDtypeStruct((n_rows, d), F32),
        compiler_params=_params(("arbitrary",)),
        name="moe_experts",
    )(tile_expert, tile_valid, xs, w1g, w1l, b1g, b1l, w2, b2)


def _combine_kernel(dest_hbm, ys_hbm, x_ref, wt_ref, o_ref, idx_smem, rows_scr, idx_sem, row_sem, *, tm):
    i = pl.program_id(0)
    n = pl.num_programs(0)
    slot = i % 2
    groups = TOP_K * tm // SUBLANES

    def idx_copy(step, s):
        return _idx_slot_copy(dest_hbm, idx_smem, idx_sem, step, s)

    def gather_rows(s):
        idx_base = s * (TOP_K * tm)

        def start_group(g, _):
            for u in range(SUBLANES):
                row = idx_smem[idx_base + g * SUBLANES + u]
                pltpu.make_async_copy(ys_hbm.at[pl.ds(row, 1), :], rows_scr.at[s, g, pl.ds(u, 1), :],
                                      row_sem.at[s]).start()
            return 0

        lax.fori_loop(0, groups, start_group, 0)

    @pl.when(i == 0)
    def _():
        idx_copy(0, 0).start()
        idx_copy(0, 0).wait()
        gather_rows(0)

        @pl.when(n > 1)
        def _():
            idx_copy(1, 1).start()

    @pl.when(i + 1 < n)
    def _():
        idx_copy(i + 1, 1 - slot).wait()
        gather_rows(1 - slot)

    @pl.when(i + 2 < n)
    def _():
        idx_copy(i + 2, slot).start()

    per_k = tm // SUBLANES
    for k in range(TOP_K):
        part = rows_scr.at[slot, pl.ds(k * per_k, per_k)]
        pltpu.make_async_copy(part, part, row_sem.at[slot]).wait()

    out = x_ref[...]
    for k in range(TOP_K):
        rows = rows_scr[slot, k * per_k:(k + 1) * per_k].reshape(tm, x_ref.shape[1])
        out = out + wt_ref[:, k:k + 1] * rows
    o_ref[...] = out


def _combine(ys, dest_tiles, x, wts_t, *, tm):
    t, d = x.shape
    return pl.pallas_call(
        functools.partial(_combine_kernel, tm=tm),
        grid=(t // tm,),
        in_specs=[
            pl.BlockSpec(memory_space=pl.ANY),
            pl.BlockSpec(memory_space=pl.ANY),
            pl.BlockSpec((tm, d), lambda i: (i, 0)),
            pl.BlockSpec((tm, TOP_K), lambda i: (i, 0)),
        ],
        out_specs=pl.BlockSpec((tm, d), lambda i: (i, 0)),
        out_shape=jax.ShapeDtypeStruct((t, d), F32),
        scratch_shapes=[
            pltpu.SMEM((2 * TOP_K * tm,), I32),
            pltpu.VMEM((2, TOP_K * tm // SUBLANES, SUBLANES, d), F32),
            pltpu.SemaphoreType.DMA((2,)),
            pltpu.SemaphoreType.DMA((2,)),
        ],
        compiler_params=_params(("arbitrary",)),
        name="moe_combine",
    )(dest_tiles, ys, x, wts_t)


def _tiles(batch, seq):
    t = batch * seq
    tm = min(512, t)
    return dict(
        tm=tm,
        tn=256,
        tq=min(512, seq),
        tk=min(512, seq),
        chunk=min(128, seq),
        tmd=min(256, t),
        tr=min(512, max(8, TOP_K * t // N_EXPERTS)),
    )


def _moe(o, w_out, x, moe_gain, router_w, router_b, w1, b1, w2, b2, *, cfg):
    t, d = x.shape
    tm, tmd, tr = cfg["tm"], cfg["tmd"], cfg["tr"]
    rw_t = router_w.T
    rw_hi = rw_t.astype(BF16)
    rw_lo = (rw_t - rw_hi.astype(F32)).astype(BF16)
    x, hm, idx, wts, rank, cnt = _out_router(o, w_out.astype(BF16), x, moe_gain, rw_hi, rw_lo, router_b, tm=tm)

    counts = cnt[:, 0]
    padded = ((counts + tr - 1) // tr) * tr
    ends = jnp.cumsum(padded).astype(I32)
    offs = ends - padded
    experts = jnp.arange(N_EXPERTS, dtype=I32)[:, None, None]
    dest = rank + jnp.sum(jnp.where(idx[None] == experts, offs[:, None, None], 0), axis=0)
    dest_tiles = dest.reshape(TOP_K, t // tmd, tmd).transpose(1, 0, 2).reshape(t // tmd, TOP_K * tmd)
    n_rows = TOP_K * t + N_EXPERTS * tr
    tile_start = jnp.arange(n_rows // tr, dtype=I32) * tr
    tile_expert = jnp.minimum(jnp.sum((tile_start[:, None] >= ends[None, :]).astype(I32), axis=1), N_EXPERTS - 1)
    tile_valid = (tile_start < ends[-1]).astype(I32)

    xs = _dispatch(hm, dest_tiles, ends, padded.astype(I32), tm=tmd, tr=tr, n_rows=n_rows)
    ff = w2.shape[1]
    w1g, w1l = _split_w1(w1)
    ys = _experts(xs, tile_expert, tile_valid, w1g, w1l,
                  b1[:, 0::2].reshape(N_EXPERTS, 1, ff), b1[:, 1::2].reshape(N_EXPERTS, 1, ff),
                  w2.astype(BF16), b2.reshape(N_EXPERTS, 1, d), tr=tr)
    return _combine(ys, dest_tiles, x, wts.T, tm=tmd)


def kernel(x, diff_norm, diff_w_in, diff_w_out, diff_lambda_q1, diff_lambda_k1, diff_lambda_q2, diff_lambda_k2, diff_q_norm, diff_k_norm, diff_subln, ret_norm, ret_w_in, ret_w_out, ret_log_decay_fwd, ret_log_decay_bwd, moe_norm, moe_router_w, moe_router_b, moe_w1, moe_b1, moe_w2, moe_b2):
    batch, seq, d = x.shape
    depth = moe_norm.shape[0]
    cfg = _tiles(batch, seq)
    x = x.reshape(batch * seq, d)
    slopes = jnp.exp2(-8.0 * jnp.arange(1, DIFF_HEADS + 1, dtype=F32) / DIFF_HEADS)
    for i in range(depth):
        j = i // N_MIXERS
        if i % N_MIXERS == 0:
            lambda_init = _lambda_init(i)
            lam = (jnp.exp(jnp.sum(diff_lambda_q1[j] * diff_lambda_k1[j]))
                   - jnp.exp(jnp.sum(diff_lambda_q2[j] * diff_lambda_k2[j])) + lambda_init)
            reps = d // DIFF_HEAD_DIM
            qk_gain = jnp.concatenate([jnp.tile(diff_q_norm[j], reps) * (DIFF_HEAD_DIM ** -0.5 * LOG2E),
                                       jnp.tile(diff_k_norm[j], reps)]).reshape(1, 2 * d)
            qkv = _in_proj(x, diff_norm[j], diff_w_in[j].astype(BF16), tm=cfg["tm"], tn=cfg["tn"],
                           qk_gain_row=qk_gain)
            o = _diff_attention(qkv, slopes, lam, diff_subln[j], lambda_init,
                                batch=batch, seq=seq, tq=cfg["tq"], tk=cfg["tk"])
            w_out = diff_w_out[j]
        else:
            dk = d // RET_HEADS
            col_scale = jnp.ones((ret_w_in.shape[2],), F32).at[d:2 * d].set(dk ** -0.5)
            proj = _in_proj(x, ret_norm[j], (ret_w_in[j] * col_scale).astype(BF16), tm=cfg["tm"], tn=cfg["tn"])
            o = _retention(proj, ret_log_decay_fwd[j], ret_log_decay_bwd[j],
                           batch=batch, seq=seq, d_model=d, chunk=cfg["chunk"])
            w_out = ret_w_out[j]
        x = _moe(o, w_out, x, moe_norm[i], moe_router_w[i], moe_router_b[i],
                 moe_w1[i], moe_b1[i], moe_w2[i], moe_b2[i], cfg=cfg)
    return x.reshape(batch, seq, d)
```

```python
import functools
import math

import jax
import jax.numpy as jnp
from jax import lax
from jax.experimental import pallas as pl
from jax.experimental.pallas import tpu as pltpu

F32 = jnp.float32
BF16 = jnp.bfloat16
I32 = jnp.int32

NORM_EPS = 1e-5
N_MIXERS = 2

DIFF_HEADS = 8
DIFF_HEAD_DIM = 64
DIFF_V_DIM = 2 * DIFF_HEAD_DIM

RET_HEADS = 4

N_EXPERTS = 32
TOP_K = 4
SWIGLU_ALPHA = 1.702
SWIGLU_LIMIT = 7.0

V7X_VMEM_LIMIT_BYTES = 56 * 1024 * 1024
LANES = 128


def _lambda_init(layer_idx):
    return 0.8 - 0.6 * math.exp(-0.3 * layer_idx)


def _params(semantics):
    return pltpu.CompilerParams(dimension_semantics=semantics, vmem_limit_bytes=V7X_VMEM_LIMIT_BYTES)


def _rms(x, eps=NORM_EPS):
    return x * lax.rsqrt(jnp.mean(x * x, axis=-1, keepdims=True) + eps)


def _in_proj_kernel(x_ref, g_ref, w_ref, gmat_ref, qkg_ref, o_ref, h_scr, *, tn, n_normed):
    h_scr[...] = (_rms(x_ref[...]) * g_ref[...]).astype(BF16)
    n_total = w_ref.shape[1]
    for c in range(n_total // tn):
        cols = slice(c * tn, (c + 1) * tn)
        acc = jnp.dot(h_scr[...], w_ref[:, cols], preferred_element_type=F32)
        if c * tn < n_normed:
            sq = acc * acc
            hi = sq.astype(BF16)
            lo = (sq - hi.astype(F32)).astype(BF16)
            msq = (jnp.dot(hi, gmat_ref[...], preferred_element_type=F32)
                   + jnp.dot(lo, gmat_ref[...], preferred_element_type=F32))
            acc = acc * lax.rsqrt(msq + NORM_EPS) * qkg_ref[:, cols]
        o_ref[:, cols] = acc.astype(BF16)


def _in_proj(x, gain, w_bf16, *, tm, tn, qk_gain_row=None):
    t, d = x.shape
    n = w_bf16.shape[1]
    if qk_gain_row is None:
        n_normed = 0
        qk_gain_row = jnp.zeros((1, n), F32)
    else:
        n_normed = qk_gain_row.shape[1]
        qk_gain_row = jnp.pad(qk_gain_row, ((0, 0), (0, n - n_normed)))
    grp = jnp.arange(tn) // DIFF_HEAD_DIM
    gmat = jnp.where(grp[:, None] == grp[None, :], 1.0 / DIFF_HEAD_DIM, 0.0).astype(BF16)
    return pl.pallas_call(
        functools.partial(_in_proj_kernel, tn=tn, n_normed=n_normed),
        grid=(t // tm,),
        in_specs=[
            pl.BlockSpec((tm, d), lambda i: (i, 0)),
            pl.BlockSpec((1, d), lambda i: (0, 0)),
            pl.BlockSpec((d, n), lambda i: (0, 0)),
            pl.BlockSpec((tn, tn), lambda i: (0, 0)),
            pl.BlockSpec((1, n), lambda i: (0, 0)),
        ],
        out_specs=pl.BlockSpec((tm, n), lambda i: (i, 0)),
        out_shape=jax.ShapeDtypeStruct((t, n), BF16),
        scratch_shapes=[pltpu.VMEM((tm, d), BF16)],
        compiler_params=_params(("parallel",)),
        name="in_proj",
    )(x, gain.reshape(1, d), w_bf16, gmat, qk_gain_row)


LOG2E = math.log2(math.e)
LOG2E_BF16_PARTS = (1.4453125, -0.00262451171875, 7.063150405883789e-06)
ALIBI_LOW_BITS = 255
ALIBI_HIGH_BIT = 256
V_ONES_ROWS = 16


def _alibi_key_features(slopes, seq, tk):
    dj = jnp.arange(seq, dtype=I32) % tk
    lo = (dj & ALIBI_LOW_BITS).astype(F32)
    hi = (dj & ALIBI_HIGH_BIT).astype(F32)
    lane = jnp.arange(LANES)
    feat = jnp.where(lane[None, :] < 3, lo[:, None], jnp.where(lane[None, :] < 6, hi[:, None], 0.0))
    return (slopes[:, None, None] * feat[None]).astype(BF16)


def _diff_attn_kernel(slopes_ref, lam_ref, q_ref, k_ref, vt_ref, fk_ref, sg_ref, o_ref,
                      qa_scr, s0_scr, s1_scr, mt0_scr, mt1_scr, p0_scr, p1_scr, a0_scr, a1_scr,
                      m_scr, acc_scr, *, tq, tk, out_scale):
    h = pl.program_id(1)
    qi = pl.program_id(2)
    slope2 = slopes_ref[h] * LOG2E
    lam = lam_ref[0]
    nk = k_ref.shape[0] // tk
    i0 = qi * tq
    jd = i0 // tk
    cols2 = 2 * tq

    q = q_ref[...]
    lane = lax.broadcasted_iota(I32, q.shape, 1)
    zero = jnp.zeros_like(q)
    qq = jnp.concatenate([jnp.where(lane < DIFF_HEAD_DIM, q, zero),
                          jnp.where(lane >= DIFF_HEAD_DIM, q, zero)], axis=0)
    flane = lax.broadcasted_iota(I32, (cols2, LANES), 1)
    a, b, c = LOG2E_BF16_PARTS
    feat = jnp.where(flane % 3 == 0, a, jnp.where(flane % 3 == 1, b, c))
    feat = jnp.where(flane < 6, feat, 0.0)
    qa_scr[0] = jnp.concatenate([qq, feat.astype(BF16)], axis=1)
    qa_scr[1] = jnp.concatenate([qq, (-feat).astype(BF16)], axis=1)
    col = lax.broadcasted_iota(I32, (1, cols2), 1)
    q_pos = (i0 + jnp.where(col >= tq, col - tq, col)).astype(F32)

    m_scr[...] = jnp.full_like(m_scr, -jnp.inf)
    acc_scr[...] = jnp.zeros_like(acc_scr)
    nt = (((1,), (1,)), ((), ()))

    s_scr, mt_scr, p_scr, a_scr = (s0_scr, s1_scr), (mt0_scr, mt1_scr), (p0_scr, p1_scr), (a0_scr, a1_scr)

    def key_rows(j):
        return pl.ds(pl.multiple_of(j * tk, tk), tk)

    def exponentials(slot, col_shift):
        for cb in range(cols2 // LANES):
            cs = slice(cb * LANES, (cb + 1) * LANES)
            shift = col_shift[:, cs]
            m_old = m_scr[:, cs]
            m_new = jnp.maximum(m_old, mt_scr[slot][:, cs] - shift)
            alpha = jnp.exp2(m_old - m_new)
            p = jnp.exp2((s_scr[slot][:, cs] - (m_new + shift)).astype(BF16))
            m_scr[:, cs] = m_new
            a_scr[slot][:, cs] = alpha
            p_scr[slot][:, cs] = p

    def values(slot, j):
        acc_scr[...] = (a_scr[slot][...] * acc_scr[...]
                        + jnp.dot(vt_ref[:, key_rows(j)], p_scr[slot][...], preferred_element_type=F32))

    s = lax.dot_general(k_ref[key_rows(jd), :], qq, nt, preferred_element_type=F32)
    key_pos = (jd * tk + lax.broadcasted_iota(I32, (tk, 1), 0)).astype(F32)
    s = s - slope2 * jnp.abs(key_pos - q_pos)
    s_scr[0][...] = s
    mt_scr[0][...] = jnp.max(s, axis=0, keepdims=True)
    exponentials(0, jnp.zeros((1, cols2), F32))
    values(0, jd)

    n_lin = nk - 1

    def tile_of(t):
        return t + (t >= jd).astype(I32)

    def scores(t, slot):
        j = tile_of(jnp.minimum(t, n_lin - 1))
        k_aug = jnp.concatenate([k_ref[key_rows(j), :], fk_ref[key_rows(j), :]], axis=1)
        s = lax.dot_general(k_aug, qa_scr[(j > jd).astype(I32)], nt, preferred_element_type=F32)
        s_scr[slot][...] = s
        mt_scr[slot][...] = jnp.max(s, axis=0, keepdims=True)

    def shift_of(t):
        j = tile_of(t)
        d = slope2 * (q_pos - (j * tk).astype(F32))
        return jnp.where(j > jd, -d, d)

    if n_lin > 0:
        p_scr[1][...] = jnp.zeros_like(p_scr[1])
        a_scr[1][...] = jnp.ones_like(a_scr[1])
        scores(0, 0)

        def stage(t, slot):
            scores(t + 1, 1 - slot)
            values(1 - slot, tile_of(jnp.maximum(t - 1, 0)))
            exponentials(slot, shift_of(t))

        def pair(tt, _):
            stage(2 * tt, 0)
            stage(2 * tt + 1, 1)
            return 0

        lax.fori_loop(0, n_lin // 2, pair, 0)
        if n_lin % 2:
            stage(jnp.int32(n_lin - 1), 0)
        values((n_lin - 1) % 2, tile_of(jnp.int32(n_lin - 1)))

    o = acc_scr[:DIFF_V_DIM, :] / acc_scr[DIFF_V_DIM:DIFF_V_DIM + 1, :]
    o = o[:, :tq] - lam * o[:, tq:]
    o = o * lax.rsqrt(jnp.mean(o * o, axis=0, keepdims=True) + NORM_EPS)
    o_ref[...] = (o.T * sg_ref[...] * out_scale).astype(BF16)


def _diff_attention(qkv, slopes, lam, subln_gain, lambda_init, *, batch, seq, tq, tk):
    assert tk % tq == 0 and tk <= 2 * ALIBI_HIGH_BIT and tq % LANES == 0
    d = DIFF_HEADS * DIFF_V_DIM
    qkv3 = qkv.reshape(batch, seq, 3 * d)
    vt = jnp.swapaxes(qkv3[:, :, 2 * d:], 1, 2).reshape(batch, DIFF_HEADS, DIFF_V_DIM, seq)
    vt = jnp.concatenate([vt, jnp.ones((batch, DIFF_HEADS, V_ONES_ROWS, seq), BF16)], axis=2)
    fk = _alibi_key_features(slopes, seq, tk)
    out = pl.pallas_call(
        functools.partial(_diff_attn_kernel, tq=tq, tk=tk, out_scale=1.0 - lambda_init),
        grid=(batch, DIFF_HEADS, seq // tq),
        in_specs=[
            pl.BlockSpec(memory_space=pltpu.SMEM),
            pl.BlockSpec(memory_space=pltpu.SMEM),
            pl.BlockSpec((None, tq, DIFF_V_DIM), lambda b, h, i: (b, i, h)),
            pl.BlockSpec((None, seq, DIFF_V_DIM), lambda b, h, i: (b, 0, DIFF_HEADS + h)),
            pl.BlockSpec((None, None, DIFF_V_DIM + V_ONES_ROWS, seq), lambda b, h, i: (b, h, 0, 0)),
            pl.BlockSpec((None, seq, LANES), lambda b, h, i: (h, 0, 0)),
            pl.BlockSpec((1, DIFF_V_DIM), lambda b, h, i: (0, 0)),
        ],
        out_specs=pl.BlockSpec((None, tq, DIFF_V_DIM), lambda b, h, i: (b, i, h)),
        out_shape=jax.ShapeDtypeStruct((batch, seq, d), BF16),
        scratch_shapes=[
            pltpu.VMEM((2, 2 * tq, 2 * LANES), BF16),
            pltpu.VMEM((tk, 2 * tq), F32), pltpu.VMEM((tk, 2 * tq), F32),
            pltpu.VMEM((1, 2 * tq), F32), pltpu.VMEM((1, 2 * tq), F32),
            pltpu.VMEM((tk, 2 * tq), BF16), pltpu.VMEM((tk, 2 * tq), BF16),
            pltpu.VMEM((1, 2 * tq), F32), pltpu.VMEM((1, 2 * tq), F32),
            pltpu.VMEM((1, 2 * tq), F32),
            pltpu.VMEM((DIFF_V_DIM + V_ONES_ROWS, 2 * tq), F32),
        ],
        compiler_params=_params(("parallel", "parallel", "parallel")),
        name="diff_attn",
    )(slopes, lam.reshape(1), qkv3, qkv3, vt, fk, subln_gain.reshape(1, DIFF_V_DIM))
    return out.reshape(batch * seq, d)


def _retention_kernel(lgf_ref, lgb_ref, q_ref, k_ref, v_ref, g_ref, o_ref, ob_scr, st_scr, *, chunk):
    h = pl.program_id(1)
    lgf = lgf_ref[h]
    lgb = lgb_ref[h]
    seq = q_ref.shape[0]
    nc = seq // chunk
    c_f = float(chunk)

    row = lax.broadcasted_iota(I32, (chunk, chunk), 0)
    col = lax.broadcasted_iota(I32, (chunk, chunk), 1)
    rel = (row - col).astype(F32)
    d_both = (jnp.where(rel >= 0, jnp.exp(lgf * jnp.maximum(rel, 0.0)), 0.0)
              + jnp.where(rel < 0, jnp.exp(lgb * jnp.maximum(-rel, 0.0)), 0.0))
    idx = lax.broadcasted_iota(I32, (chunk, 1), 0).astype(F32)
    xi_f = jnp.exp(lgf * (idx + 1.0))
    zeta_f = jnp.exp(lgf * (c_f - 1.0 - idx))
    xi_b = jnp.exp(lgb * (c_f - idx))
    zeta_b = jnp.exp(lgb * idx)
    decay_f = jnp.exp(lgf * c_f)
    decay_b = jnp.exp(lgb * c_f)

    def state_update(k, v, zeta, decay):
        kz_t = (k.astype(F32) * zeta).T.astype(BF16)
        upd = jnp.dot(kz_t, v, preferred_element_type=F32)
        st_scr[...] = st_scr[...] * decay + upd

    st_scr[...] = jnp.zeros_like(st_scr)

    def bwd_body(i, _):
        c = nc - 1 - i
        rows = pl.ds(pl.multiple_of(c * chunk, chunk), chunk)
        q = q_ref[rows, :]
        inter = jnp.dot(q, st_scr[...].astype(BF16), preferred_element_type=F32)
        ob_scr[rows, :] = inter * xi_b
        state_update(k_ref[rows, :], v_ref[rows, :], zeta_b, decay_b)
        return 0

    lax.fori_loop(0, nc, bwd_body, 0)

    st_scr[...] = jnp.zeros_like(st_scr)

    def fwd_body(c, _):
        rows = pl.ds(pl.multiple_of(c * chunk, chunk), chunk)
        q = q_ref[rows, :]
        k = k_ref[rows, :]
        v = v_ref[rows, :]
        scores = lax.dot_general(q, k, (((1,), (1,)), ((), ())), preferred_element_type=F32) * d_both
        o = jnp.dot(scores.astype(BF16), v, preferred_element_type=F32)
        o = o + jnp.dot(q, st_scr[...].astype(BF16), preferred_element_type=F32) * xi_f
        o = o + ob_scr[rows, :]
        state_update(k, v, zeta_f, decay_f)
        g = g_ref[rows, :].astype(F32)
        o_ref[rows, :] = (g * jax.nn.sigmoid(g) * _rms(o)).astype(BF16)
        return 0

    lax.fori_loop(0, nc, fwd_body, 0)


def _retention(proj, lg_fwd, lg_bwd, *, batch, seq, d_model, chunk):
    dk = d_model // RET_HEADS
    dv = 2 * dk
    vw = RET_HEADS * dv
    p3 = proj.reshape(batch, seq, 2 * d_model + 2 * vw)
    out = pl.pallas_call(
        functools.partial(_retention_kernel, chunk=chunk),
        grid=(batch, RET_HEADS),
        in_specs=[
            pl.BlockSpec(memory_space=pltpu.SMEM),
            pl.BlockSpec(memory_space=pltpu.SMEM),
            pl.BlockSpec((None, seq, dk), lambda b, h: (b, 0, h)),
            pl.BlockSpec((None, seq, dk), lambda b, h: (b, 0, RET_HEADS + h)),
            pl.BlockSpec((None, seq, dv), lambda b, h: (b, 0, RET_HEADS + h)),
            pl.BlockSpec((None, seq, dv), lambda b, h: (b, 0, 2 * RET_HEADS + h)),
        ],
        out_specs=pl.BlockSpec((None, seq, dv), lambda b, h: (b, 0, h)),
        out_shape=jax.ShapeDtypeStruct((batch, seq, vw), BF16),
        scratch_shapes=[pltpu.VMEM((seq, dv), F32), pltpu.VMEM((dk, dv), F32)],
        compiler_params=_params(("parallel", "parallel")),
        name="retention",
    )(lg_fwd, lg_bwd, p3, p3, p3, p3)
    return out.reshape(batch * seq, vw)


def _out_router_kernel(o_ref, w_ref, x_ref, g_ref, rwh_ref, rwl_ref, rb_ref,
                       xn_ref, hm_ref, idx_ref, wts_ref, rank_ref, cnt_ref, carry_scr):
    i = pl.program_id(0)
    tm = x_ref.shape[0]

    @pl.when(i == 0)
    def _():
        carry_scr[...] = jnp.zeros_like(carry_scr)

    xn = x_ref[...] + jnp.dot(o_ref[...], w_ref[...], preferred_element_type=F32)
    xn_ref[...] = xn
    hm = _rms(xn) * g_ref[...]
    hm_ref[...] = hm

    hm_hi = hm.astype(BF16)
    hm_lo = (hm - hm_hi.astype(F32)).astype(BF16)
    nt = (((1,), (1,)), ((), ()))
    logits = (lax.dot_general(rwh_ref[...], hm_hi, nt, preferred_element_type=F32)
              + lax.dot_general(rwh_ref[...], hm_lo, nt, preferred_element_type=F32)
              + lax.dot_general(rwl_ref[...], hm_hi, nt, preferred_element_type=F32)
              + rb_ref[...])

    eidx = lax.broadcasted_iota(I32, (N_EXPERTS, tm), 0)
    work = logits
    vals, sels, hots = [], [], []
    for _ in range(TOP_K):
        m = jnp.max(work, axis=0, keepdims=True)
        sel = jnp.min(jnp.where(work == m, eidx, N_EXPERTS), axis=0, keepdims=True)
        hot = eidx == sel
        vals.append(m)
        sels.append(sel)
        hots.append(hot)
        work = jnp.where(hot, -jnp.inf, work)
    exps = [jnp.exp(v - vals[0]) for v in vals]
    denom = exps[0] + exps[1] + exps[2] + exps[3]

    mask = jnp.where(hots[0] | hots[1] | hots[2] | hots[3], 1.0, 0.0)
    tri = (lax.broadcasted_iota(I32, (tm, tm), 0) < lax.broadcasted_iota(I32, (tm, tm), 1))
    excl = jnp.dot(mask.astype(BF16), jnp.where(tri, 1.0, 0.0).astype(BF16), preferred_element_type=F32)
    rank_e = excl + carry_scr[:, 0:1]
    for k in range(TOP_K):
        idx_ref[k:k + 1, :] = sels[k]
        wts_ref[k:k + 1, :] = exps[k] / denom
        rank_ref[k:k + 1, :] = jnp.sum(jnp.where(hots[k], rank_e, 0.0), axis=0, keepdims=True).astype(I32)
    carry_scr[...] = carry_scr[...] + jnp.sum(mask, axis=1, keepdims=True)
    cnt_ref[...] = carry_scr[...].astype(I32)


def _out_router(o, w_out_bf16, x, moe_gain, rw_hi_t, rw_lo_t, router_b, *, tm):
    t, d = x.shape
    dv = o.shape[1]
    tok = lambda i: (i, 0)
    fixed = lambda i: (0, 0)
    lanes_tok = lambda i: (0, i)
    return pl.pallas_call(
        _out_router_kernel,
        grid=(t // tm,),
        in_specs=[
            pl.BlockSpec((tm, dv), tok),
            pl.BlockSpec((dv, d), fixed),
            pl.BlockSpec((tm, d), tok),
            pl.BlockSpec((1, d), fixed),
            pl.BlockSpec((N_EXPERTS, d), fixed),
            pl.BlockSpec((N_EXPERTS, d), fixed),
            pl.BlockSpec((N_EXPERTS, 1), fixed),
        ],
        out_specs=[
            pl.BlockSpec((tm, d), tok),
            pl.BlockSpec((tm, d), tok),
            pl.BlockSpec((TOP_K, tm), lanes_tok),
            pl.BlockSpec((TOP_K, tm), lanes_tok),
            pl.BlockSpec((TOP_K, tm), lanes_tok),
            pl.BlockSpec((N_EXPERTS, LANES), fixed),
        ],
        out_shape=[
            jax.ShapeDtypeStruct((t, d), F32),
            jax.ShapeDtypeStruct((t, d), F32),
            jax.ShapeDtypeStruct((TOP_K, t), I32),
            jax.ShapeDtypeStruct((TOP_K, t), F32),
            jax.ShapeDtypeStruct((TOP_K, t), I32),
            jax.ShapeDtypeStruct((N_EXPERTS, LANES), I32),
        ],
        scratch_shapes=[pltpu.VMEM((N_EXPERTS, LANES), F32)],
        compiler_params=_params(("arbitrary",)),
        name="out_router",
    )(o, w_out_bf16, x, moe_gain.reshape(1, d), rw_hi_t, rw_lo_t, router_b.reshape(N_EXPERTS, 1))


SUBLANES = 8


def _idx_slot_copy(dest_hbm, idx_smem, idx_sem, step, s):
    n_idx = dest_hbm.shape[1]
    return pltpu.make_async_copy(dest_hbm.at[step], idx_smem.at[pl.ds(pl.multiple_of(s * n_idx, n_idx), n_idx)],
                                 idx_sem.at[s])


def _dispatch_kernel(ends_ref, padded_ref, dest_hbm, hm_ref, xs_hbm, idx_smem, zero_scr, idx_sem, fill_sem, row_sem,
                     *, tm, tr, n_tiles):
    i = pl.program_id(0)
    n = pl.num_programs(0)
    slot = i % 2
    groups = tm // SUBLANES

    def idx_copy(step, s):
        return _idx_slot_copy(dest_hbm, idx_smem, idx_sem, step, s)

    @pl.when(i == 0)
    def _():
        idx_copy(0, 0).start()
        zero_scr[...] = jnp.zeros_like(zero_scr)

        def fill(start):
            return pltpu.make_async_copy(zero_scr, xs_hbm.at[pl.ds(pl.multiple_of(start, tr), tr), :], fill_sem)

        first_unused = ends_ref[N_EXPERTS - 1] // tr
        for e in range(N_EXPERTS):
            @pl.when(padded_ref[e] > 0)
            def _():
                fill(ends_ref[e] - tr).start()
        lax.fori_loop(first_unused, n_tiles, lambda c, _: (fill(c * tr).start(), 0)[1], 0)
        for e in range(N_EXPERTS):
            @pl.when(padded_ref[e] > 0)
            def _():
                fill(0).wait()
        lax.fori_loop(first_unused, n_tiles, lambda c, _: (fill(0).wait(), 0)[1], 0)

    @pl.when(i + 1 < n)
    def _():
        idx_copy(i + 1, 1 - slot).start()

    idx_copy(i, slot).wait()
    idx_base = slot * (TOP_K * tm)

    def start_group(g, _):
        for u in range(SUBLANES):
            src = hm_ref.at[g, pl.ds(u, 1), :]
            for k in range(TOP_K):
                row = idx_smem[idx_base + k * tm + g * SUBLANES + u]
                pltpu.make_async_copy(src, xs_hbm.at[pl.ds(row, 1), :], row_sem).start()
        return 0

    lax.fori_loop(0, groups, start_group, 0)

    for _ in range(TOP_K):
        pltpu.make_async_copy(xs_hbm.at[pl.ds(0, tm), :], xs_hbm.at[pl.ds(0, tm), :], row_sem).wait()


def _dispatch(hm, dest_tiles, ends, padded, *, tm, tr, n_rows):
    t, d = hm.shape
    return pl.pallas_call(
        functools.partial(_dispatch_kernel, tm=tm, tr=tr, n_tiles=n_rows // tr),
        grid_spec=pltpu.PrefetchScalarGridSpec(
            num_scalar_prefetch=2,
            grid=(t // tm,),
            in_specs=[
                pl.BlockSpec(memory_space=pl.ANY),
                pl.BlockSpec((tm // SUBLANES, SUBLANES, d), lambda i, *_: (i, 0, 0)),
            ],
            out_specs=pl.BlockSpec(memory_space=pl.ANY),
            scratch_shapes=[
                pltpu.SMEM((2 * TOP_K * tm,), I32),
                pltpu.VMEM((tr, d), F32),
                pltpu.SemaphoreType.DMA((2,)),
                pltpu.SemaphoreType.DMA(()),
                pltpu.SemaphoreType.DMA(()),
            ],
        ),
        out_shape=jax.ShapeDtypeStruct((n_rows, d), F32),
        compiler_params=_params(("arbitrary",)),
        name="moe_dispatch",
    )(ends, padded, dest_tiles, hm.reshape(t // SUBLANES, SUBLANES, d))


MXU_WIDTH = 256


def _split_w1_kernel(w_ref, perm_ref, g_ref, l_ref):
    half = MXU_WIDTH // 2
    for c in range(w_ref.shape[1] // MXU_WIDTH):
        y = jnp.dot(w_ref[:, c * MXU_WIDTH:(c + 1) * MXU_WIDTH].astype(BF16), perm_ref[...],
                    preferred_element_type=F32)
        g_ref[:, c * half:(c + 1) * half] = y[:, :half].astype(BF16)
        l_ref[:, c * half:(c + 1) * half] = y[:, half:].astype(BF16)


def _split_w1(w1):
    e, d, ff2 = w1.shape
    ff = ff2 // 2
    rows = min(512, d)
    src = jnp.arange(MXU_WIDTH)
    dst = jnp.where(src % 2 == 0, src // 2, MXU_WIDTH // 2 + src // 2)
    perm = (dst[:, None] == jnp.arange(MXU_WIDTH)[None, :]).astype(BF16)
    blk = lambda i, r: (i, r, 0)
    return pl.pallas_call(
        _split_w1_kernel,
        grid=(e, d // rows),
        in_specs=[pl.BlockSpec((None, rows, ff2), blk), pl.BlockSpec((MXU_WIDTH, MXU_WIDTH), lambda i, r: (0, 0))],
        out_specs=[pl.BlockSpec((None, rows, ff), blk), pl.BlockSpec((None, rows, ff), blk)],
        out_shape=[jax.ShapeDtypeStruct((e, d, ff), BF16), jax.ShapeDtypeStruct((e, d, ff), BF16)],
        compiler_params=_params(("parallel", "parallel")),
        name="split_w1",
    )(w1, perm)


def _experts_kernel(te_ref, tv_ref, xs_ref, w1g_ref, w1l_ref, b1g_ref, b1l_ref, w2_ref, b2_ref, ys_ref):
    i = pl.program_id(0)

    @pl.when(tv_ref[i] > 0)
    def _():
        x = xs_ref[...].astype(BF16)
        glu = jnp.dot(x, w1g_ref[...], preferred_element_type=F32) + b1g_ref[...]
        lin = jnp.dot(x, w1l_ref[...], preferred_element_type=F32) + b1l_ref[...]
        glu = jnp.minimum(glu, SWIGLU_LIMIT)
        lin = jnp.clip(lin, -SWIGLU_LIMIT, SWIGLU_LIMIT)
        act = glu * jax.nn.sigmoid(SWIGLU_ALPHA * glu) * (lin + 1.0)
        ys_ref[...] = jnp.dot(act.astype(BF16), w2_ref[...], preferred_element_type=F32) + b2_ref[...]

    @pl.when(tv_ref[i] == 0)
    def _():
        ys_ref[...] = jnp.zeros_like(ys_ref)


def _experts(xs, tile_expert, tile_valid, w1g, w1l, b1g, b1l, w2, b2, *, tr):
    n_rows, d = xs.shape
    ff = w1g.shape[2]
    n_tiles = n_rows // tr
    row_tile = lambda i, te, tv: (i, 0)
    by_expert = lambda i, te, tv: (te[i], 0, 0)
    return pl.pallas_call(
        _experts_kernel,
        grid_spec=pltpu.PrefetchScalarGridSpec(
            num_scalar_prefetch=2,
            grid=(n_tiles,),
            in_specs=[
                pl.BlockSpec((tr, d), row_tile),
                pl.BlockSpec((None, d, ff), by_expert),
                pl.BlockSpec((None, d, ff), by_expert),
                pl.BlockSpec((None, 1, ff), by_expert),
                pl.BlockSpec((None, 1, ff), by_expert),
                pl.BlockSpec((None, ff, d), by_expert),
                pl.BlockSpec((None, 1, d), by_expert),
            ],
            out_specs=pl.BlockSpec((tr, d), row_tile),
        ),
        out_shape=jax.ShapeDtypeStruct((n_rows, d), F32),
        compiler_params=_params(("arbitrary",)),
        name="moe_experts",
    )(tile_expert, tile_valid, xs, w1g, w1l, b1g, b1l, w2, b2)


def _combine_kernel(dest_hbm, ys_hbm, x_ref, wt_ref, o_ref, idx_smem, rows_scr, idx_sem, row_sem, *, tm):
    i = pl.program_id(0)
    n = pl.num_programs(0)
    slot = i % 2
    groups = TOP_K * tm // SUBLANES

    def idx_copy(step, s):
        return _idx_slot_copy(dest_hbm, idx_smem, idx_sem, step, s)

    def gather_rows(s):
        idx_base = s * (TOP_K * tm)

        def start_group(g, _):
            for u in range(SUBLANES):
                row = idx_smem[idx_base + g * SUBLANES + u]
                pltpu.make_async_copy(ys_hbm.at[pl.ds(row, 1), :], rows_scr.at[s, g, pl.ds(u, 1), :],
                                      row_sem.at[s]).start()
            return 0

        lax.fori_loop(0, groups, start_group, 0)

    @pl.when(i == 0)
    def _():
        idx_copy(0, 0).start()
        idx_copy(0, 0).wait()
        gather_rows(0)

        @pl.when(n > 1)
        def _():
            idx_copy(1, 1).start()

    @pl.when(i + 1 < n)
    def _():
        idx_copy(i + 1, 1 - slot).wait()
        gather_rows(1 - slot)

    @pl.when(i + 2 < n)
    def _():
        idx_copy(i + 2, slot).start()

    per_k = tm // SUBLANES
    for k in range(TOP_K):
        part = rows_scr.at[slot, pl.ds(k * per_k, per_k)]
        pltpu.make_async_copy(part, part, row_sem.at[slot]).wait()

    out = x_ref[...]
    for k in range(TOP_K):
        rows = rows_scr[slot, k * per_k:(k + 1) * per_k].reshape(tm, x_ref.shape[1])
        out = out + wt_ref[:, k:k + 1] * rows
    o_ref[...] = out


def _combine(ys, dest_tiles, x, wts_t, *, tm):
    t, d = x.shape
    return pl.pallas_call(
        functools.partial(_combine_kernel, tm=tm),
        grid=(t // tm,),
        in_specs=[
            pl.BlockSpec(memory_space=pl.ANY),
            pl.BlockSpec(memory_space=pl.ANY),
            pl.BlockSpec((tm, d), lambda i: (i, 0)),
            pl.BlockSpec((tm, TOP_K), lambda i: (i, 0)),
        ],
        out_specs=pl.BlockSpec((tm, d), lambda i: (i, 0)),
        out_shape=jax.ShapeDtypeStruct((t, d), F32),
        scratch_shapes=[
            pltpu.SMEM((2 * TOP_K * tm,), I32),
            pltpu.VMEM((2, TOP_K * tm // SUBLANES, SUBLANES, d), F32),
            pltpu.SemaphoreType.DMA((2,)),
            pltpu.SemaphoreType.DMA((2,)),
        ],
        compiler_params=_params(("arbitrary",)),
        name="moe_combine",
    )(dest_tiles, ys, x, wts_t)


def _tiles(batch, seq):
    t = batch * seq
    tm = min(512, t)
    return dict(
        tm=tm,
        tn=256,
        tq=min(512, seq),
        tk=min(512, seq),
        chunk=min(256, seq),
        tmd=min(256, t),
        tr=min(512, max(8, TOP_K * t // N_EXPERTS)),
    )


def _moe(o, w_out, x, moe_gain, router_w, router_b, w1, b1, w2, b2, *, cfg):
    t, d = x.shape
    tm, tmd, tr = cfg["tm"], cfg["tmd"], cfg["tr"]
    rw_t = router_w.T
    rw_hi = rw_t.astype(BF16)
    rw_lo = (rw_t - rw_hi.astype(F32)).astype(BF16)
    x, hm, idx, wts, rank, cnt = _out_router(o, w_out.astype(BF16), x, moe_gain, rw_hi, rw_lo, router_b, tm=tm)

    counts = cnt[:, 0]
    padded = ((counts + tr - 1) // tr) * tr
    ends = jnp.cumsum(padded).astype(I32)
    offs = ends - padded
    experts = jnp.arange(N_EXPERTS, dtype=I32)[:, None, None]
    dest = rank + jnp.sum(jnp.where(idx[None] == experts, offs[:, None, None], 0), axis=0)
    dest_tiles = dest.reshape(TOP_K, t // tmd, tmd).transpose(1, 0, 2).reshape(t // tmd, TOP_K * tmd)
    n_rows = TOP_K * t + N_EXPERTS * tr
    tile_start = jnp.arange(n_rows // tr, dtype=I32) * tr
    tile_expert = jnp.minimum(jnp.sum((tile_start[:, None] >= ends[None, :]).astype(I32), axis=1), N_EXPERTS - 1)
    tile_valid = (tile_start < ends[-1]).astype(I32)

    xs = _dispatch(hm, dest_tiles, ends, padded.astype(I32), tm=tmd, tr=tr, n_rows=n_rows)
    ff = w2.shape[1]
    w1g, w1l = _split_w1(w1)
    ys = _experts(xs, tile_expert, tile_valid, w1g, w1l,
                  b1[:, 0::2].reshape(N_EXPERTS, 1, ff), b1[:, 1::2].reshape(N_EXPERTS, 1, ff),
                  w2.astype(BF16), b2.reshape(N_EXPERTS, 1, d), tr=tr)
    return _combine(ys, dest_tiles, x, wts.T, tm=tmd)


def kernel(x, diff_norm, diff_w_in, diff_w_out, diff_lambda_q1, diff_lambda_k1, diff_lambda_q2, diff_lambda_k2, diff_q_norm, diff_k_norm, diff_subln, ret_norm, ret_w_in, ret_w_out, ret_log_decay_fwd, ret_log_decay_bwd, moe_norm, moe_router_w, moe_router_b, moe_w1, moe_b1, moe_w2, moe_b2):
    batch, seq, d = x.shape
    depth = moe_norm.shape[0]
    cfg = _tiles(batch, seq)
    x = x.reshape(batch * seq, d)
    slopes = jnp.exp2(-8.0 * jnp.arange(1, DIFF_HEADS + 1, dtype=F32) / DIFF_HEADS)
    for i in range(depth):
        j = i // N_MIXERS
        if i % N_MIXERS == 0:
            lambda_init = _lambda_init(i)
            lam = (jnp.exp(jnp.sum(diff_lambda_q1[j] * diff_lambda_k1[j]))
                   - jnp.exp(jnp.sum(diff_lambda_q2[j] * diff_lambda_k2[j])) + lambda_init)
            reps = d // DIFF_HEAD_DIM
            qk_gain = jnp.concatenate([jnp.tile(diff_q_norm[j], reps) * (DIFF_HEAD_DIM ** -0.5 * LOG2E),
                                       jnp.tile(diff_k_norm[j], reps)]).reshape(1, 2 * d)
            qkv = _in_proj(x, diff_norm[j], diff_w_in[j].astype(BF16), tm=cfg["tm"], tn=cfg["tn"],
                           qk_gain_row=qk_gain)
            o = _diff_attention(qkv, slopes, lam, diff_subln[j], lambda_init,
                                batch=batch, seq=seq, tq=cfg["tq"], tk=cfg["tk"])
            w_out = diff_w_out[j]
        else:
            dk = d // RET_HEADS
            col_scale = jnp.ones((ret_w_in.shape[2],), F32).at[d:2 * d].set(dk ** -0.5)
            proj = _in_proj(x, ret_norm[j], (ret_w_in[j] * col_scale).astype(BF16), tm=cfg["tm"], tn=cfg["tn"])
            o = _retention(proj, ret_log_decay_fwd[j], ret_log_decay_bwd[j],
                           batch=batch, seq=seq, d_model=d, chunk=cfg["chunk"])
            w_out = ret_w_out[j]
        x = _moe(o, w_out, x, moe_norm[i], moe_router_w[i], moe_router_b[i],
                 moe_w1[i], moe_b1[i], moe_w2[i], moe_b2[i], cfg=cfg)
    return x.reshape(batch, seq, d)
```

```python
import functools
import math

import jax
import jax.numpy as jnp
from jax import lax
from jax.experimental import pallas as pl
from jax.experimental.pallas import tpu as pltpu

F32 = jnp.float32
BF16 = jnp.bfloat16
I32 = jnp.int32

NORM_EPS = 1e-5
N_MIXERS = 2

DIFF_HEADS = 8
DIFF_HEAD_DIM = 64
DIFF_V_DIM = 2 * DIFF_HEAD_DIM

RET_HEADS = 4

N_EXPERTS = 32
TOP_K = 4
SWIGLU_ALPHA = 1.702
SWIGLU_LIMIT = 7.0

V7X_VMEM_LIMIT_BYTES = 56 * 1024 * 1024
LANES = 128


def _lambda_init(layer_idx):
    return 0.8 - 0.6 * math.exp(-0.3 * layer_idx)


def _params(semantics):
    return pltpu.CompilerParams(dimension_semantics=semantics, vmem_limit_bytes=V7X_VMEM_LIMIT_BYTES)


def _rms(x, eps=NORM_EPS):
    return x * lax.rsqrt(jnp.mean(x * x, axis=-1, keepdims=True) + eps)


def _in_proj_kernel(x_ref, g_ref, w_ref, gmat_ref, qkg_ref, o_ref, *rest, tn, n_normed, n_transposed):
    vt_ref, h_scr = rest if n_transposed else (None, rest[0])
    h_scr[...] = (_rms(x_ref[...]) * g_ref[...]).astype(BF16)
    n_total = w_ref.shape[1]
    n_chunks = n_total // tn
    n_rowmajor = n_total - n_transposed
    if n_transposed:
        vt_ref[:, DIFF_V_DIM:, :] = jnp.ones((vt_ref.shape[0], V_ONES_ROWS, vt_ref.shape[2]), BF16)

    def project(c):
        return jnp.dot(h_scr[...], w_ref[:, c * tn:(c + 1) * tn], preferred_element_type=F32)

    def finish(c, acc):
        cols = slice(c * tn, (c + 1) * tn)
        if c * tn < n_normed:
            sq = acc * acc
            hi = sq.astype(BF16)
            lo = (sq - hi.astype(F32)).astype(BF16)
            msq = (jnp.dot(hi, gmat_ref[...], preferred_element_type=F32)
                   + jnp.dot(lo, gmat_ref[...], preferred_element_type=F32))
            acc = acc * lax.rsqrt(msq + NORM_EPS) * qkg_ref[:, cols]
        if c * tn < n_rowmajor:
            o_ref[:, cols] = acc.astype(BF16)
        else:
            for part in range(tn // DIFF_V_DIM):
                head = (c * tn - n_rowmajor) // DIFF_V_DIM + part
                vt_ref[head, :DIFF_V_DIM, :] = acc[:, part * DIFF_V_DIM:(part + 1) * DIFF_V_DIM].T.astype(BF16)

    acc = project(0)
    for c in range(1, n_chunks):
        nxt = project(c)
        finish(c - 1, acc)
        acc = nxt
    finish(n_chunks - 1, acc)


def _in_proj(x, gain, w_bf16, *, tm, tn, qk_gain_row=None, values_layout=None):
    t, d = x.shape
    n = w_bf16.shape[1]
    if qk_gain_row is None:
        n_normed = 0
        qk_gain_row = jnp.zeros((1, n), F32)
    else:
        n_normed = qk_gain_row.shape[1]
        qk_gain_row = jnp.pad(qk_gain_row, ((0, 0), (0, n - n_normed)))
    grp = jnp.arange(tn) // DIFF_HEAD_DIM
    gmat = jnp.where(grp[:, None] == grp[None, :], 1.0 / DIFF_HEAD_DIM, 0.0).astype(BF16)
    n_transposed = DIFF_HEADS * DIFF_V_DIM if values_layout else 0
    n_rowmajor = n - n_transposed
    out_specs = [pl.BlockSpec((tm, n_rowmajor), lambda i: (i, 0))]
    out_shape = [jax.ShapeDtypeStruct((t, n_rowmajor), BF16)]
    if values_layout:
        batch, seq = values_layout
        per_seq = seq // tm
        rows = DIFF_V_DIM + V_ONES_ROWS
        out_specs.append(pl.BlockSpec((None, DIFF_HEADS, rows, tm), lambda i: (i // per_seq, 0, 0, i % per_seq)))
        out_shape.append(jax.ShapeDtypeStruct((batch, DIFF_HEADS, rows, seq), BF16))
    return pl.pallas_call(
        functools.partial(_in_proj_kernel, tn=tn, n_normed=n_normed, n_transposed=n_transposed),
        grid=(t // tm,),
        in_specs=[
            pl.BlockSpec((tm, d), lambda i: (i, 0)),
            pl.BlockSpec((1, d), lambda i: (0, 0)),
            pl.BlockSpec((d, n), lambda i: (0, 0)),
            pl.BlockSpec((tn, tn), lambda i: (0, 0)),
            pl.BlockSpec((1, n), lambda i: (0, 0)),
        ],
        out_specs=out_specs,
        out_shape=out_shape,
        scratch_shapes=[pltpu.VMEM((tm, d), BF16)],
        compiler_params=_params(("parallel",)),
        name="in_proj",
    )(x, gain.reshape(1, d), w_bf16, gmat, qk_gain_row)


LOG2E = math.log2(math.e)
LOG2E_BF16_PARTS = (1.4453125, -0.00262451171875, 7.063150405883789e-06)
ALIBI_LOW_BITS = 255
ALIBI_HIGH_BIT = 256
V_ONES_ROWS = 16


def _alibi_key_features(slopes, seq, tk):
    dj = jnp.arange(seq, dtype=I32) % tk
    lo = (dj & ALIBI_LOW_BITS).astype(F32)
    hi = (dj & ALIBI_HIGH_BIT).astype(F32)
    lane = jnp.arange(LANES)
    feat = jnp.where(lane[None, :] < 3, lo[:, None], jnp.where(lane[None, :] < 6, hi[:, None], 0.0))
    return (slopes[:, None, None] * feat[None]).astype(BF16)


def _diff_attn_kernel(slopes_ref, lam_ref, q_ref, k_ref, vt_ref, fk_ref, sg_ref, o_ref,
                      qa_scr, s0_scr, s1_scr, mt0_scr, mt1_scr, p0_scr, p1_scr, a0_scr, a1_scr,
                      m_scr, acc_scr, *, tq, tk, out_scale):
    h = pl.program_id(1)
    qi = pl.program_id(2)
    slope2 = slopes_ref[h] * LOG2E
    lam = lam_ref[0]
    nk = k_ref.shape[0] // tk
    i0 = qi * tq
    jd = i0 // tk
    cols2 = 2 * tq

    q = q_ref[...]
    lane = lax.broadcasted_iota(I32, q.shape, 1)
    zero = jnp.zeros_like(q)
    qq = jnp.concatenate([jnp.where(lane < DIFF_HEAD_DIM, q, zero),
                          jnp.where(lane >= DIFF_HEAD_DIM, q, zero)], axis=0)
    flane = lax.broadcasted_iota(I32, (cols2, LANES), 1)
    a, b, c = LOG2E_BF16_PARTS
    feat = jnp.where(flane % 3 == 0, a, jnp.where(flane % 3 == 1, b, c))
    feat = jnp.where(flane < 6, feat, 0.0)
    qa_scr[0] = jnp.concatenate([qq, feat.astype(BF16)], axis=1)
    qa_scr[1] = jnp.concatenate([qq, (-feat).astype(BF16)], axis=1)
    col = lax.broadcasted_iota(I32, (1, cols2), 1)
    q_pos = (i0 + jnp.where(col >= tq, col - tq, col)).astype(F32)

    m_scr[...] = jnp.full_like(m_scr, -jnp.inf)
    acc_scr[...] = jnp.zeros_like(acc_scr)
    nt = (((1,), (1,)), ((), ()))

    s_scr, mt_scr, p_scr, a_scr = (s0_scr, s1_scr), (mt0_scr, mt1_scr), (p0_scr, p1_scr), (a0_scr, a1_scr)

    def key_rows(j):
        return pl.ds(pl.multiple_of(j * tk, tk), tk)

    def exponentials(slot, col_shift):
        for cb in range(cols2 // LANES):
            cs = slice(cb * LANES, (cb + 1) * LANES)
            shift = col_shift[:, cs]
            m_old = m_scr[:, cs]
            m_new = jnp.maximum(m_old, mt_scr[slot][:, cs] - shift)
            alpha = jnp.exp2(m_old - m_new)
            p = jnp.exp2((s_scr[slot][:, cs] - (m_new + shift)).astype(BF16))
            m_scr[:, cs] = m_new
            a_scr[slot][:, cs] = alpha
            p_scr[slot][:, cs] = p

    def values(slot, j):
        acc_scr[...] = (a_scr[slot][...] * acc_scr[...]
                        + jnp.dot(vt_ref[:, key_rows(j)], p_scr[slot][...], preferred_element_type=F32))

    s = lax.dot_general(k_ref[key_rows(jd), :], qq, nt, preferred_element_type=F32)
    key_pos = (jd * tk + lax.broadcasted_iota(I32, (tk, 1), 0)).astype(F32)
    s = s - slope2 * jnp.abs(key_pos - q_pos)
    s_scr[0][...] = s
    mt_scr[0][...] = jnp.max(s, axis=0, keepdims=True)
    exponentials(0, jnp.zeros((1, cols2), F32))
    values(0, jd)

    n_lin = nk - 1

    def tile_of(t):
        return t + (t >= jd).astype(I32)

    def scores(t, slot):
        j = tile_of(jnp.minimum(t, n_lin - 1))
        k_aug = jnp.concatenate([k_ref[key_rows(j), :], fk_ref[key_rows(j), :]], axis=1)
        s = lax.dot_general(k_aug, qa_scr[(j > jd).astype(I32)], nt, preferred_element_type=F32)
        s_scr[slot][...] = s
        mt_scr[slot][...] = jnp.max(s, axis=0, keepdims=True)

    def shift_of(t):
        j = tile_of(t)
        d = slope2 * (q_pos - (j * tk).astype(F32))
        return jnp.where(j > jd, -d, d)

    if n_lin > 0:
        p_scr[1][...] = jnp.zeros_like(p_scr[1])
        a_scr[1][...] = jnp.ones_like(a_scr[1])
        scores(0, 0)

        def stage(t, slot):
            scores(t + 1, 1 - slot)
            values(1 - slot, tile_of(jnp.maximum(t - 1, 0)))
            exponentials(slot, shift_of(t))

        def pair(tt, _):
            stage(2 * tt, 0)
            stage(2 * tt + 1, 1)
            return 0

        lax.fori_loop(0, n_lin // 2, pair, 0)
        if n_lin % 2:
            stage(jnp.int32(n_lin - 1), 0)
        values((n_lin - 1) % 2, tile_of(jnp.int32(n_lin - 1)))

    o = acc_scr[:DIFF_V_DIM, :] / acc_scr[DIFF_V_DIM:DIFF_V_DIM + 1, :]
    o = o[:, :tq] - lam * o[:, tq:]
    o = o * lax.rsqrt(jnp.mean(o * o, axis=0, keepdims=True) + NORM_EPS)
    o_ref[...] = (o.T * sg_ref[...] * out_scale).astype(BF16)


def _diff_attention(qk, vt, slopes, lam, subln_gain, lambda_init, *, batch, seq, tq, tk):
    assert tk % tq == 0 and tk <= 2 * ALIBI_HIGH_BIT and tq % LANES == 0
    d = DIFF_HEADS * DIFF_V_DIM
    qkv3 = qk.reshape(batch, seq, 2 * d)
    fk = _alibi_key_features(slopes, seq, tk)
    out = pl.pallas_call(
        functools.partial(_diff_attn_kernel, tq=tq, tk=tk, out_scale=1.0 - lambda_init),
        grid=(batch, DIFF_HEADS, seq // tq),
        in_specs=[
            pl.BlockSpec(memory_space=pltpu.SMEM),
            pl.BlockSpec(memory_space=pltpu.SMEM),
            pl.BlockSpec((None, tq, DIFF_V_DIM), lambda b, h, i: (b, i, h)),
            pl.BlockSpec((None, seq, DIFF_V_DIM), lambda b, h, i: (b, 0, DIFF_HEADS + h)),
            pl.BlockSpec((None, None, DIFF_V_DIM + V_ONES_ROWS, seq), lambda b, h, i: (b, h, 0, 0)),
            pl.BlockSpec((None, seq, LANES), lambda b, h, i: (h, 0, 0)),
            pl.BlockSpec((1, DIFF_V_DIM), lambda b, h, i: (0, 0)),
        ],
        out_specs=pl.BlockSpec((None, tq, DIFF_V_DIM), lambda b, h, i: (b, i, h)),
        out_shape=jax.ShapeDtypeStruct((batch, seq, d), BF16),
        scratch_shapes=[
            pltpu.VMEM((2, 2 * tq, 2 * LANES), BF16),
            pltpu.VMEM((tk, 2 * tq), F32), pltpu.VMEM((tk, 2 * tq), F32),
            pltpu.VMEM((1, 2 * tq), F32), pltpu.VMEM((1, 2 * tq), F32),
            pltpu.VMEM((tk, 2 * tq), BF16), pltpu.VMEM((tk, 2 * tq), BF16),
            pltpu.VMEM((1, 2 * tq), F32), pltpu.VMEM((1, 2 * tq), F32),
            pltpu.VMEM((1, 2 * tq), F32),
            pltpu.VMEM((DIFF_V_DIM + V_ONES_ROWS, 2 * tq), F32),
        ],
        compiler_params=_params(("parallel", "parallel", "parallel")),
        name="diff_attn",
    )(slopes, lam.reshape(1), qkv3, qkv3, vt, fk, subln_gain.reshape(1, DIFF_V_DIM))
    return out.reshape(batch * seq, d)


def _retention_kernel(lgf_ref, lgb_ref, q_ref, k_ref, v_ref, g_ref, o_ref, ob_scr, st_scr, *, chunk):
    h = pl.program_id(1)
    lgf = lgf_ref[h]
    lgb = lgb_ref[h]
    seq = q_ref.shape[0]
    nc = seq // chunk
    c_f = float(chunk)

    row = lax.broadcasted_iota(I32, (chunk, chunk), 0)
    col = lax.broadcasted_iota(I32, (chunk, chunk), 1)
    rel = (row - col).astype(F32)
    d_both = (jnp.where(rel >= 0, jnp.exp(lgf * jnp.maximum(rel, 0.0)), 0.0)
              + jnp.where(rel < 0, jnp.exp(lgb * jnp.maximum(-rel, 0.0)), 0.0))
    idx = lax.broadcasted_iota(I32, (chunk, 1), 0).astype(F32)
    xi_f = jnp.exp(lgf * (idx + 1.0))
    zeta_f = jnp.exp(lgf * (c_f - 1.0 - idx))
    xi_b = jnp.exp(lgb * (c_f - idx))
    zeta_b = jnp.exp(lgb * idx)
    decay_f = jnp.exp(lgf * c_f)
    decay_b = jnp.exp(lgb * c_f)

    def state_update(k, v, zeta, decay):
        kz_t = (k.astype(F32) * zeta).T.astype(BF16)
        upd = jnp.dot(kz_t, v, preferred_element_type=F32)
        st_scr[...] = st_scr[...] * decay + upd

    st_scr[...] = jnp.zeros_like(st_scr)

    def bwd_body(i, _):
        c = nc - 1 - i
        rows = pl.ds(pl.multiple_of(c * chunk, chunk), chunk)
        q = q_ref[rows, :]
        inter = jnp.dot(q, st_scr[...].astype(BF16), preferred_element_type=F32)
        ob_scr[rows, :] = inter * xi_b
        state_update(k_ref[rows, :], v_ref[rows, :], zeta_b, decay_b)
        return 0

    lax.fori_loop(0, nc, bwd_body, 0)

    st_scr[...] = jnp.zeros_like(st_scr)

    def fwd_body(c, _):
        rows = pl.ds(pl.multiple_of(c * chunk, chunk), chunk)
        q = q_ref[rows, :]
        k = k_ref[rows, :]
        v = v_ref[rows, :]
        scores = lax.dot_general(q, k, (((1,), (1,)), ((), ())), preferred_element_type=F32) * d_both
        o = jnp.dot(scores.astype(BF16), v, preferred_element_type=F32)
        o = o + jnp.dot(q, st_scr[...].astype(BF16), preferred_element_type=F32) * xi_f
        o = o + ob_scr[rows, :]
        state_update(k, v, zeta_f, decay_f)
        g = g_ref[rows, :].astype(F32)
        o_ref[rows, :] = (g * jax.nn.sigmoid(g) * _rms(o)).astype(BF16)
        return 0

    lax.fori_loop(0, nc, fwd_body, 0)


def _retention(proj, lg_fwd, lg_bwd, *, batch, seq, d_model, chunk):
    dk = d_model // RET_HEADS
    dv = 2 * dk
    vw = RET_HEADS * dv
    p3 = proj.reshape(batch, seq, 2 * d_model + 2 * vw)
    out = pl.pallas_call(
        functools.partial(_retention_kernel, chunk=chunk),
        grid=(batch, RET_HEADS),
        in_specs=[
            pl.BlockSpec(memory_space=pltpu.SMEM),
            pl.BlockSpec(memory_space=pltpu.SMEM),
            pl.BlockSpec((None, seq, dk), lambda b, h: (b, 0, h)),
            pl.BlockSpec((None, seq, dk), lambda b, h: (b, 0, RET_HEADS + h)),
            pl.BlockSpec((None, seq, dv), lambda b, h: (b, 0, RET_HEADS + h)),
            pl.BlockSpec((None, seq, dv), lambda b, h: (b, 0, 2 * RET_HEADS + h)),
        ],
        out_specs=pl.BlockSpec((None, seq, dv), lambda b, h: (b, 0, h)),
        out_shape=jax.ShapeDtypeStruct((batch, seq, vw), BF16),
        scratch_shapes=[pltpu.VMEM((seq, dv), F32), pltpu.VMEM((dk, dv), F32)],
        compiler_params=_params(("parallel", "parallel")),
        name="retention",
    )(lg_fwd, lg_bwd, p3, p3, p3, p3)
    return out.reshape(batch * seq, vw)


def _out_router_kernel(o_ref, w_ref, x_ref, g_ref, rwh_ref, rwl_ref, rb_ref,
                       xn_ref, hm_ref, idx_ref, wts_ref, rank_ref, cnt_ref, carry_scr):
    i = pl.program_id(0)
    tm = x_ref.shape[0]

    @pl.when(i == 0)
    def _():
        carry_scr[...] = jnp.zeros_like(carry_scr)

    xn = x_ref[...] + jnp.dot(o_ref[...], w_ref[...], preferred_element_type=F32)
    xn_ref[...] = xn
    hm = _rms(xn) * g_ref[...]
    hm_ref[...] = hm

    hm_hi = hm.astype(BF16)
    hm_lo = (hm - hm_hi.astype(F32)).astype(BF16)
    nt = (((1,), (1,)), ((), ()))
    logits = (lax.dot_general(rwh_ref[...], hm_hi, nt, preferred_element_type=F32)
              + lax.dot_general(rwh_ref[...], hm_lo, nt, preferred_element_type=F32)
              + lax.dot_general(rwl_ref[...], hm_hi, nt, preferred_element_type=F32)
              + rb_ref[...])

    eidx = lax.broadcasted_iota(I32, (N_EXPERTS, tm), 0)
    work = logits
    vals, sels, hots = [], [], []
    for _ in range(TOP_K):
        m = jnp.max(work, axis=0, keepdims=True)
        sel = jnp.min(jnp.where(work == m, eidx, N_EXPERTS), axis=0, keepdims=True)
        hot = eidx == sel
        vals.append(m)
        sels.append(sel)
        hots.append(hot)
        work = jnp.where(hot, -jnp.inf, work)
    exps = [jnp.exp(v - vals[0]) for v in vals]
    denom = exps[0] + exps[1] + exps[2] + exps[3]

    mask = jnp.where(hots[0] | hots[1] | hots[2] | hots[3], 1.0, 0.0)
    tri = (lax.broadcasted_iota(I32, (tm, tm), 0) < lax.broadcasted_iota(I32, (tm, tm), 1))
    excl = jnp.dot(mask.astype(BF16), jnp.where(tri, 1.0, 0.0).astype(BF16), preferred_element_type=F32)
    rank_e = excl + carry_scr[:, 0:1]
    for k in range(TOP_K):
        idx_ref[k:k + 1, :] = sels[k]
        wts_ref[k:k + 1, :] = exps[k] / denom
        rank_ref[k:k + 1, :] = jnp.sum(jnp.where(hots[k], rank_e, 0.0), axis=0, keepdims=True).astype(I32)
    carry_scr[...] = carry_scr[...] + jnp.sum(mask, axis=1, keepdims=True)
    cnt_ref[...] = carry_scr[...].astype(I32)


def _out_router(o, w_out_bf16, x, moe_gain, rw_hi_t, rw_lo_t, router_b, *, tm):
    t, d = x.shape
    dv = o.shape[1]
    tok = lambda i: (i, 0)
    fixed = lambda i: (0, 0)
    lanes_tok = lambda i: (0, i)
    return pl.pallas_call(
        _out_router_kernel,
        grid=(t // tm,),
        in_specs=[
            pl.BlockSpec((tm, dv), tok),
            pl.BlockSpec((dv, d), fixed),
            pl.BlockSpec((tm, d), tok),
            pl.BlockSpec((1, d), fixed),
            pl.BlockSpec((N_EXPERTS, d), fixed),
            pl.BlockSpec((N_EXPERTS, d), fixed),
            pl.BlockSpec((N_EXPERTS, 1), fixed),
        ],
        out_specs=[
            pl.BlockSpec((tm, d), tok),
            pl.BlockSpec((tm, d), tok),
            pl.BlockSpec((TOP_K, tm), lanes_tok),
            pl.BlockSpec((TOP_K, tm), lanes_tok),
            pl.BlockSpec((TOP_K, tm), lanes_tok),
            pl.BlockSpec((N_EXPERTS, LANES), fixed),
        ],
        out_shape=[
            jax.ShapeDtypeStruct((t, d), F32),
            jax.ShapeDtypeStruct((t, d), F32),
            jax.ShapeDtypeStruct((TOP_K, t), I32),
            jax.ShapeDtypeStruct((TOP_K, t), F32),
            jax.ShapeDtypeStruct((TOP_K, t), I32),
            jax.ShapeDtypeStruct((N_EXPERTS, LANES), I32),
        ],
        scratch_shapes=[pltpu.VMEM((N_EXPERTS, LANES), F32)],
        compiler_params=_params(("arbitrary",)),
        name="out_router",
    )(o, w_out_bf16, x, moe_gain.reshape(1, d), rw_hi_t, rw_lo_t, router_b.reshape(N_EXPERTS, 1))


SUBLANES = 8


def _idx_slot_copy(dest_hbm, idx_smem, idx_sem, step, s):
    n_idx = dest_hbm.shape[1]
    return pltpu.make_async_copy(dest_hbm.at[step], idx_smem.at[pl.ds(pl.multiple_of(s * n_idx, n_idx), n_idx)],
                                 idx_sem.at[s])


def _dispatch_kernel(ends_ref, padded_ref, dest_hbm, hm_ref, xs_hbm, idx_smem, zero_scr, idx_sem, fill_sem, row_sem,
                     *, tm, tr, n_tiles):
    i = pl.program_id(0)
    n = pl.num_programs(0)
    slot = i % 2
    groups = tm // SUBLANES

    def idx_copy(step, s):
        return _idx_slot_copy(dest_hbm, idx_smem, idx_sem, step, s)

    @pl.when(i == 0)
    def _():
        idx_copy(0, 0).start()
        zero_scr[...] = jnp.zeros_like(zero_scr)

        def fill(start):
            return pltpu.make_async_copy(zero_scr, xs_hbm.at[pl.ds(pl.multiple_of(start, tr), tr), :], fill_sem)

        first_unused = ends_ref[N_EXPERTS - 1] // tr
        for e in range(N_EXPERTS):
            @pl.when(padded_ref[e] > 0)
            def _():
                fill(ends_ref[e] - tr).start()
        lax.fori_loop(first_unused, n_tiles, lambda c, _: (fill(c * tr).start(), 0)[1], 0)
        for e in range(N_EXPERTS):
            @pl.when(padded_ref[e] > 0)
            def _():
                fill(0).wait()
        lax.fori_loop(first_unused, n_tiles, lambda c, _: (fill(0).wait(), 0)[1], 0)

    @pl.when(i + 1 < n)
    def _():
        idx_copy(i + 1, 1 - slot).start()

    idx_copy(i, slot).wait()
    idx_base = slot * (TOP_K * tm)

    def start_group(g, _):
        for u in range(SUBLANES):
            src = hm_ref.at[g, pl.ds(u, 1), :]
            for k in range(TOP_K):
                row = idx_smem[idx_base + k * tm + g * SUBLANES + u]
                pltpu.make_async_copy(src, xs_hbm.at[pl.ds(row, 1), :], row_sem).start()
        return 0

    lax.fori_loop(0, groups, start_group, 0)

    for _ in range(TOP_K):
        pltpu.make_async_copy(xs_hbm.at[pl.ds(0, tm), :], xs_hbm.at[pl.ds(0, tm), :], row_sem).wait()


def _dispatch(hm, dest_tiles, ends, padded, *, tm, tr, n_rows):
    t, d = hm.shape
    return pl.pallas_call(
        functools.partial(_dispatch_kernel, tm=tm, tr=tr, n_tiles=n_rows // tr),
        grid_spec=pltpu.PrefetchScalarGridSpec(
            num_scalar_prefetch=2,
            grid=(t // tm,),
            in_specs=[
                pl.BlockSpec(memory_space=pl.ANY),
                pl.BlockSpec((tm // SUBLANES, SUBLANES, d), lambda i, *_: (i, 0, 0)),
            ],
            out_specs=pl.BlockSpec(memory_space=pl.ANY),
            scratch_shapes=[
                pltpu.SMEM((2 * TOP_K * tm,), I32),
                pltpu.VMEM((tr, d), F32),
                pltpu.SemaphoreType.DMA((2,)),
                pltpu.SemaphoreType.DMA(()),
                pltpu.SemaphoreType.DMA(()),
            ],
        ),
        out_shape=jax.ShapeDtypeStruct((n_rows, d), F32),
        compiler_params=_params(("arbitrary",)),
        name="moe_dispatch",
    )(ends, padded, dest_tiles, hm.reshape(t // SUBLANES, SUBLANES, d))


MXU_WIDTH = 256


def _split_w1_kernel(w_ref, perm_ref, g_ref, l_ref):
    half = MXU_WIDTH // 2
    for c in range(w_ref.shape[1] // MXU_WIDTH):
        y = jnp.dot(w_ref[:, c * MXU_WIDTH:(c + 1) * MXU_WIDTH].astype(BF16), perm_ref[...],
                    preferred_element_type=F32)
        g_ref[:, c * half:(c + 1) * half] = y[:, :half].astype(BF16)
        l_ref[:, c * half:(c + 1) * half] = y[:, half:].astype(BF16)


def _split_w1(w1):
    e, d, ff2 = w1.shape
    ff = ff2 // 2
    rows = min(512, d)
    src = jnp.arange(MXU_WIDTH)
    dst = jnp.where(src % 2 == 0, src // 2, MXU_WIDTH // 2 + src // 2)
    perm = (dst[:, None] == jnp.arange(MXU_WIDTH)[None, :]).astype(BF16)
    blk = lambda i, r: (i, r, 0)
    return pl.pallas_call(
        _split_w1_kernel,
        grid=(e, d // rows),
        in_specs=[pl.BlockSpec((None, rows, ff2), blk), pl.BlockSpec((MXU_WIDTH, MXU_WIDTH), lambda i, r: (0, 0))],
        out_specs=[pl.BlockSpec((None, rows, ff), blk), pl.BlockSpec((None, rows, ff), blk)],
        out_shape=[jax.ShapeDtypeStruct((e, d, ff), BF16), jax.ShapeDtypeStruct((e, d, ff), BF16)],
        compiler_params=_params(("parallel", "parallel")),
        name="split_w1",
    )(w1, perm)


def _experts_kernel(te_ref, tv_ref, xs_ref, w1g_ref, w1l_ref, b1g_ref, b1l_ref, w2_ref, b2_ref, ys_ref):
    i = pl.program_id(0)

    @pl.when(tv_ref[i] > 0)
    def _():
        x = xs_ref[...].astype(BF16)
        glu = jnp.dot(x, w1g_ref[...], preferred_element_type=F32) + b1g_ref[...]
        lin = jnp.dot(x, w1l_ref[...], preferred_element_type=F32) + b1l_ref[...]
        glu = jnp.minimum(glu, SWIGLU_LIMIT)
        lin = jnp.clip(lin, -SWIGLU_LIMIT, SWIGLU_LIMIT)
        act = glu * jax.nn.sigmoid(SWIGLU_ALPHA * glu) * (lin + 1.0)
        ys_ref[...] = (jnp.dot(act.astype(BF16), w2_ref[...].astype(BF16), preferred_element_type=F32)
                       + b2_ref[...])

    @pl.when(tv_ref[i] == 0)
    def _():
        ys_ref[...] = jnp.zeros_like(ys_ref)


def _experts(xs, tile_expert, tile_valid, w1g, w1l, b1g, b1l, w2, b2, *, tr):
    n_rows, d = xs.shape
    ff = w1g.shape[2]
    n_tiles = n_rows // tr
    row_tile = lambda i, te, tv: (i, 0)
    by_expert = lambda i, te, tv: (te[i], 0, 0)
    return pl.pallas_call(
        _experts_kernel,
        grid_spec=pltpu.PrefetchScalarGridSpec(
            num_scalar_prefetch=2,
            grid=(n_tiles,),
            in_specs=[
                pl.BlockSpec((tr, d), row_tile),
                pl.BlockSpec((None, d, ff), by_expert),
                pl.BlockSpec((None, d, ff), by_expert),
                pl.BlockSpec((None, 1, ff), by_expert),
                pl.BlockSpec((None, 1, ff), by_expert),
                pl.BlockSpec((None, ff, d), by_expert),
                pl.BlockSpec((None, 1, d), by_expert),
            ],
            out_specs=pl.BlockSpec((tr, d), row_tile),
        ),
        out_shape=jax.ShapeDtypeStruct((n_rows, d), F32),
        compiler_params=_params(("arbitrary",)),
        name="moe_experts",
    )(tile_expert, tile_valid, xs, w1g, w1l, b1g, b1l, w2, b2)


def _combine_kernel(dest_hbm, ys_hbm, x_ref, wt_ref, o_ref, idx_smem, rows_scr, idx_sem, row_sem, *, tm):
    i = pl.program_id(0)
    n = pl.num_programs(0)
    slot = i % 2
    groups = TOP_K * tm // SUBLANES

    def idx_copy(step, s):
        return _idx_slot_copy(dest_hbm, idx_smem, idx_sem, step, s)

    def gather_rows(s):
        idx_base = s * (TOP_K * tm)

        def start_group(g, _):
            for u in range(SUBLANES):
                row = idx_smem[idx_base + g * SUBLANES + u]
                pltpu.make_async_copy(ys_hbm.at[pl.ds(row, 1), :], rows_scr.at[s, g, pl.ds(u, 1), :],
                                      row_sem.at[s]).start()
            return 0

        lax.fori_loop(0, groups, start_group, 0)

    @pl.when(i == 0)
    def _():
        idx_copy(0, 0).start()
        idx_copy(0, 0).wait()
        gather_rows(0)

        @pl.when(n > 1)
        def _():
            idx_copy(1, 1).start()

    @pl.when(i + 1 < n)
    def _():
        idx_copy(i + 1, 1 - slot).wait()
        gather_rows(1 - slot)

    @pl.when(i + 2 < n)
    def _():
        idx_copy(i + 2, slot).start()

    per_k = tm // SUBLANES
    for k in range(TOP_K):
        part = rows_scr.at[slot, pl.ds(k * per_k, per_k)]
        pltpu.make_async_copy(part, part, row_sem.at[slot]).wait()

    out = x_ref[...]
    for k in range(TOP_K):
        rows = rows_scr[slot, k * per_k:(k + 1) * per_k].reshape(tm, x_ref.shape[1])
        out = out + wt_ref[:, k:k + 1] * rows
    o_ref[...] = out


def _combine(ys, dest_tiles, x, wts_t, *, tm):
    t, d = x.shape
    return pl.pallas_call(
        functools.partial(_combine_kernel, tm=tm),
        grid=(t // tm,),
        in_specs=[
            pl.BlockSpec(memory_space=pl.ANY),
            pl.BlockSpec(memory_space=pl.ANY),
            pl.BlockSpec((tm, d), lambda i: (i, 0)),
            pl.BlockSpec((tm, TOP_K), lambda i: (i, 0)),
        ],
        out_specs=pl.BlockSpec((tm, d), lambda i: (i, 0)),
        out_shape=jax.ShapeDtypeStruct((t, d), F32),
        scratch_shapes=[
            pltpu.SMEM((2 * TOP_K * tm,), I32),
            pltpu.VMEM((2, TOP_K * tm // SUBLANES, SUBLANES, d), F32),
            pltpu.SemaphoreType.DMA((2,)),
            pltpu.SemaphoreType.DMA((2,)),
        ],
        compiler_params=_params(("arbitrary",)),
        name="moe_combine",
    )(dest_tiles, ys, x, wts_t)


def _tiles(batch, seq):
    t = batch * seq
    tm = min(512, t)
    return dict(
        tm=tm,
        tn=256,
        tq=min(512, seq),
        tk=min(512, seq),
        chunk=min(256, seq),
        tmd=min(256, t),
        tr=min(512, max(8, TOP_K * t // N_EXPERTS)),
    )


def _moe(o, w_out, x, moe_gain, router_w, router_b, w1, b1, w2, b2, *, cfg):
    t, d = x.shape
    tm, tmd, tr = cfg["tm"], cfg["tmd"], cfg["tr"]
    rw_t = router_w.T
    rw_hi = rw_t.astype(BF16)
    rw_lo = (rw_t - rw_hi.astype(F32)).astype(BF16)
    x, hm, idx, wts, rank, cnt = _out_router(o, w_out.astype(BF16), x, moe_gain, rw_hi, rw_lo, router_b, tm=tm)

    counts = cnt[:, 0]
    padded = ((counts + tr - 1) // tr) * tr
    ends = jnp.cumsum(padded).astype(I32)
    offs = ends - padded
    experts = jnp.arange(N_EXPERTS, dtype=I32)[:, None, None]
    dest = rank + jnp.sum(jnp.where(idx[None] == experts, offs[:, None, None], 0), axis=0)
    dest_tiles = dest.reshape(TOP_K, t // tmd, tmd).transpose(1, 0, 2).reshape(t // tmd, TOP_K * tmd)
    n_rows = TOP_K * t + N_EXPERTS * tr
    tile_start = jnp.arange(n_rows // tr, dtype=I32) * tr
    tile_expert = jnp.minimum(jnp.sum((tile_start[:, None] >= ends[None, :]).astype(I32), axis=1), N_EXPERTS - 1)
    tile_valid = (tile_start < ends[-1]).astype(I32)

    xs = _dispatch(hm, dest_tiles, ends, padded.astype(I32), tm=tmd, tr=tr, n_rows=n_rows)
    ff = w2.shape[1]
    w1g, w1l = _split_w1(w1)
    ys = _experts(xs, tile_expert, tile_valid, w1g, w1l,
                  b1[:, 0::2].reshape(N_EXPERTS, 1, ff), b1[:, 1::2].reshape(N_EXPERTS, 1, ff),
                  w2, b2.reshape(N_EXPERTS, 1, d), tr=tr)
    return _combine(ys, dest_tiles, x, wts.T, tm=tmd)


def kernel(x, diff_norm, diff_w_in, diff_w_out, diff_lambda_q1, diff_lambda_k1, diff_lambda_q2, diff_lambda_k2, diff_q_norm, diff_k_norm, diff_subln, ret_norm, ret_w_in, ret_w_out, ret_log_decay_fwd, ret_log_decay_bwd, moe_norm, moe_router_w, moe_router_b, moe_w1, moe_b1, moe_w2, moe_b2):
    batch, seq, d = x.shape
    depth = moe_norm.shape[0]
    cfg = _tiles(batch, seq)
    x = x.reshape(batch * seq, d)
    slopes = jnp.exp2(-8.0 * jnp.arange(1, DIFF_HEADS + 1, dtype=F32) / DIFF_HEADS)
    for i in range(depth):
        j = i // N_MIXERS
        if i % N_MIXERS == 0:
            lambda_init = _lambda_init(i)
            lam = (jnp.exp(jnp.sum(diff_lambda_q1[j] * diff_lambda_k1[j]))
                   - jnp.exp(jnp.sum(diff_lambda_q2[j] * diff_lambda_k2[j])) + lambda_init)
            reps = d // DIFF_HEAD_DIM
            qk_gain = jnp.concatenate([jnp.tile(diff_q_norm[j], reps) * (DIFF_HEAD_DIM ** -0.5 * LOG2E),
                                       jnp.tile(diff_k_norm[j], reps)]).reshape(1, 2 * d)
            qk, vt = _in_proj(x, diff_norm[j], diff_w_in[j].astype(BF16), tm=cfg["tm"], tn=cfg["tn"],
                              qk_gain_row=qk_gain, values_layout=(batch, seq))
            o = _diff_attention(qk, vt, slopes, lam, diff_subln[j], lambda_init,
                                batch=batch, seq=seq, tq=cfg["tq"], tk=cfg["tk"])
            w_out = diff_w_out[j]
        else:
            dk = d // RET_HEADS
            col_scale = jnp.ones((ret_w_in.shape[2],), F32).at[d:2 * d].set(dk ** -0.5)
            proj, = _in_proj(x, ret_norm[j], (ret_w_in[j] * col_scale).astype(BF16), tm=cfg["tm"], tn=cfg["tn"])
            o = _retention(proj, ret_log_decay_fwd[j], ret_log_decay_bwd[j],
                           batch=batch, seq=seq, d_model=d, chunk=cfg["chunk"])
            w_out = ret_w_out[j]
        x = _moe(o, w_out, x, moe_norm[i], moe_router_w[i], moe_router_b[i],
                 moe_w1[i], moe_b1[i], moe_w2[i], moe_b2[i], cfg=cfg)
    return x.reshape(batch, seq, d)
```

```python
import functools
import math

import jax
import jax.numpy as jnp
from jax import lax
from jax.experimental import pallas as pl
from jax.experimental.pallas import tpu as pltpu

F32 = jnp.float32
BF16 = jnp.bfloat16
I32 = jnp.int32

NORM_EPS = 1e-5
N_MIXERS = 2

DIFF_HEADS = 8
DIFF_HEAD_DIM = 64
DIFF_V_DIM = 2 * DIFF_HEAD_DIM

RET_HEADS = 4

N_EXPERTS = 32
TOP_K = 4
SWIGLU_ALPHA = 1.702
SWIGLU_LIMIT = 7.0

V7X_VMEM_LIMIT_BYTES = 56 * 1024 * 1024
LANES = 128


def _lambda_init(layer_idx):
    return 0.8 - 0.6 * math.exp(-0.3 * layer_idx)


def _params(semantics):
    return pltpu.CompilerParams(dimension_semantics=semantics, vmem_limit_bytes=V7X_VMEM_LIMIT_BYTES)


def _rms(x, eps=NORM_EPS):
    return x * lax.rsqrt(jnp.mean(x * x, axis=-1, keepdims=True) + eps)


def _in_proj_kernel(x_ref, g_ref, w_ref, gmat_ref, qkg_ref, o_ref, *rest, tn, n_normed, n_transposed):
    vt_ref, h_scr = rest if n_transposed else (None, rest[0])
    h_scr[...] = (_rms(x_ref[...]) * g_ref[...]).astype(BF16)
    n_total = w_ref.shape[1]
    n_chunks = n_total // tn
    n_rowmajor = n_total - n_transposed
    if n_transposed:
        vt_ref[:, DIFF_V_DIM:, :] = jnp.ones((vt_ref.shape[0], V_ONES_ROWS, vt_ref.shape[2]), BF16)

    def project(c):
        return jnp.dot(h_scr[...], w_ref[:, c * tn:(c + 1) * tn], preferred_element_type=F32)

    def finish(c, acc):
        cols = slice(c * tn, (c + 1) * tn)
        if c * tn < n_normed:
            sq = acc * acc
            hi = sq.astype(BF16)
            lo = (sq - hi.astype(F32)).astype(BF16)
            msq = (jnp.dot(hi, gmat_ref[...], preferred_element_type=F32)
                   + jnp.dot(lo, gmat_ref[...], preferred_element_type=F32))
            acc = acc * lax.rsqrt(msq + NORM_EPS) * qkg_ref[:, cols]
        if c * tn < n_rowmajor:
            o_ref[:, cols] = acc.astype(BF16)
        else:
            for part in range(tn // DIFF_V_DIM):
                head = (c * tn - n_rowmajor) // DIFF_V_DIM + part
                vt_ref[head, :DIFF_V_DIM, :] = acc[:, part * DIFF_V_DIM:(part + 1) * DIFF_V_DIM].T.astype(BF16)

    acc = project(0)
    for c in range(1, n_chunks):
        nxt = project(c)
        finish(c - 1, acc)
        acc = nxt
    finish(n_chunks - 1, acc)


def _in_proj(x, gain, w_bf16, *, tm, tn, qk_gain_row=None, values_layout=None):
    t, d = x.shape
    n = w_bf16.shape[1]
    if qk_gain_row is None:
        n_normed = 0
        qk_gain_row = jnp.zeros((1, n), F32)
    else:
        n_normed = qk_gain_row.shape[1]
        qk_gain_row = jnp.pad(qk_gain_row, ((0, 0), (0, n - n_normed)))
    grp = jnp.arange(tn) // DIFF_HEAD_DIM
    gmat = jnp.where(grp[:, None] == grp[None, :], 1.0 / DIFF_HEAD_DIM, 0.0).astype(BF16)
    n_transposed = DIFF_HEADS * DIFF_V_DIM if values_layout else 0
    n_rowmajor = n - n_transposed
    out_specs = [pl.BlockSpec((tm, n_rowmajor), lambda i: (i, 0))]
    out_shape = [jax.ShapeDtypeStruct((t, n_rowmajor), BF16)]
    if values_layout:
        batch, seq = values_layout
        per_seq = seq // tm
        rows = DIFF_V_DIM + V_ONES_ROWS
        out_specs.append(pl.BlockSpec((None, DIFF_HEADS, rows, tm), lambda i: (i // per_seq, 0, 0, i % per_seq)))
        out_shape.append(jax.ShapeDtypeStruct((batch, DIFF_HEADS, rows, seq), BF16))
    return pl.pallas_call(
        functools.partial(_in_proj_kernel, tn=tn, n_normed=n_normed, n_transposed=n_transposed),
        grid=(t // tm,),
        in_specs=[
            pl.BlockSpec((tm, d), lambda i: (i, 0)),
            pl.BlockSpec((1, d), lambda i: (0, 0)),
            pl.BlockSpec((d, n), lambda i: (0, 0)),
            pl.BlockSpec((tn, tn), lambda i: (0, 0)),
            pl.BlockSpec((1, n), lambda i: (0, 0)),
        ],
        out_specs=out_specs,
        out_shape=out_shape,
        scratch_shapes=[pltpu.VMEM((tm, d), BF16)],
        compiler_params=_params(("parallel",)),
        name="in_proj",
    )(x, gain.reshape(1, d), w_bf16, gmat, qk_gain_row)


LOG2E = math.log2(math.e)
LOG2E_BF16_PARTS = (1.4453125, -0.00262451171875, 7.063150405883789e-06)
ALIBI_LOW_BITS = 255
ALIBI_HIGH_BIT = 256
V_ONES_ROWS = 16


def _alibi_key_features(slopes, seq, tk):
    dj = jnp.arange(seq, dtype=I32) % tk
    lo = (dj & ALIBI_LOW_BITS).astype(F32)
    hi = (dj & ALIBI_HIGH_BIT).astype(F32)
    lane = jnp.arange(LANES)
    feat = jnp.where(lane[None, :] < 3, lo[:, None], jnp.where(lane[None, :] < 6, hi[:, None], 0.0))
    return (slopes[:, None, None] * feat[None]).astype(BF16)


def _diff_attn_kernel(slopes_ref, lam_ref, q_ref, k_ref, vt_ref, fk_ref, sg_ref, o_ref,
                      qa_scr, s0_scr, s1_scr, mt0_scr, mt1_scr, p0_scr, p1_scr, a0_scr, a1_scr,
                      m_scr, acc_scr, *, tq, tk, out_scale):
    h = pl.program_id(1)
    qi = pl.program_id(2)
    slope2 = slopes_ref[h] * LOG2E
    lam = lam_ref[0]
    nk = k_ref.shape[0] // tk
    i0 = qi * tq
    jd = i0 // tk
    cols2 = 2 * tq

    q = q_ref[...]
    lane = lax.broadcasted_iota(I32, q.shape, 1)
    zero = jnp.zeros_like(q)
    qq = jnp.concatenate([jnp.where(lane < DIFF_HEAD_DIM, q, zero),
                          jnp.where(lane >= DIFF_HEAD_DIM, q, zero)], axis=0)
    flane = lax.broadcasted_iota(I32, (cols2, LANES), 1)
    a, b, c = LOG2E_BF16_PARTS
    feat = jnp.where(flane % 3 == 0, a, jnp.where(flane % 3 == 1, b, c))
    feat = jnp.where(flane < 6, feat, 0.0)
    qa_scr[0] = jnp.concatenate([qq, feat.astype(BF16)], axis=1)
    qa_scr[1] = jnp.concatenate([qq, (-feat).astype(BF16)], axis=1)
    col = lax.broadcasted_iota(I32, (1, cols2), 1)
    q_pos = (i0 + jnp.where(col >= tq, col - tq, col)).astype(F32)

    m_scr[...] = jnp.full_like(m_scr, -jnp.inf)
    acc_scr[...] = jnp.zeros_like(acc_scr)
    nt = (((1,), (1,)), ((), ()))

    s_scr, mt_scr, p_scr, a_scr = (s0_scr, s1_scr), (mt0_scr, mt1_scr), (p0_scr, p1_scr), (a0_scr, a1_scr)

    def key_rows(j):
        return pl.ds(pl.multiple_of(j * tk, tk), tk)

    def exponentials(slot, col_shift):
        for cb in range(cols2 // LANES):
            cs = slice(cb * LANES, (cb + 1) * LANES)
            shift = col_shift[:, cs]
            m_old = m_scr[:, cs]
            m_new = jnp.maximum(m_old, mt_scr[slot][:, cs] - shift)
            alpha = jnp.exp2(m_old - m_new)
            p = jnp.exp2((s_scr[slot][:, cs] - (m_new + shift)).astype(BF16))
            m_scr[:, cs] = m_new
            a_scr[slot][:, cs] = alpha
            p_scr[slot][:, cs] = p

    def values(slot, j):
        acc_scr[...] = (a_scr[slot][...] * acc_scr[...]
                        + jnp.dot(vt_ref[:, key_rows(j)], p_scr[slot][...], preferred_element_type=F32))

    s = lax.dot_general(k_ref[key_rows(jd), :], qq, nt, preferred_element_type=F32)
    key_pos = (jd * tk + lax.broadcasted_iota(I32, (tk, 1), 0)).astype(F32)
    s = s - slope2 * jnp.abs(key_pos - q_pos)
    s_scr[0][...] = s
    mt_scr[0][...] = jnp.max(s, axis=0, keepdims=True)
    exponentials(0, jnp.zeros((1, cols2), F32))
    values(0, jd)

    n_lin = nk - 1

    def tile_of(t):
        return t + (t >= jd).astype(I32)

    def scores(t, slot):
        j = tile_of(jnp.minimum(t, n_lin - 1))
        k_aug = jnp.concatenate([k_ref[key_rows(j), :], fk_ref[key_rows(j), :]], axis=1)
        s = lax.dot_general(k_aug, qa_scr[(j > jd).astype(I32)], nt, preferred_element_type=F32)
        s_scr[slot][...] = s
        mt_scr[slot][...] = jnp.max(s, axis=0, keepdims=True)

    def shift_of(t):
        j = tile_of(t)
        d = slope2 * (q_pos - (j * tk).astype(F32))
        return jnp.where(j > jd, -d, d)

    if n_lin > 0:
        p_scr[1][...] = jnp.zeros_like(p_scr[1])
        a_scr[1][...] = jnp.ones_like(a_scr[1])
        scores(0, 0)

        def stage(t, slot):
            scores(t + 1, 1 - slot)
            values(1 - slot, tile_of(jnp.maximum(t - 1, 0)))
            exponentials(slot, shift_of(t))

        def pair(tt, _):
            stage(2 * tt, 0)
            stage(2 * tt + 1, 1)
            return 0

        lax.fori_loop(0, n_lin // 2, pair, 0)
        if n_lin % 2:
            stage(jnp.int32(n_lin - 1), 0)
        values((n_lin - 1) % 2, tile_of(jnp.int32(n_lin - 1)))

    o = acc_scr[:DIFF_V_DIM, :] / acc_scr[DIFF_V_DIM:DIFF_V_DIM + 1, :]
    o = o[:, :tq] - lam * o[:, tq:]
    o = o * lax.rsqrt(jnp.mean(o * o, axis=0, keepdims=True) + NORM_EPS)
    o_ref[...] = (o.T * sg_ref[...] * out_scale).astype(BF16)


def _diff_attention(qk, vt, slopes, lam, subln_gain, lambda_init, *, batch, seq, tq, tk):
    assert tk % tq == 0 and tk <= 2 * ALIBI_HIGH_BIT and tq % LANES == 0
    d = DIFF_HEADS * DIFF_V_DIM
    qkv3 = qk.reshape(batch, seq, 2 * d)
    fk = _alibi_key_features(slopes, seq, tk)
    out = pl.pallas_call(
        functools.partial(_diff_attn_kernel, tq=tq, tk=tk, out_scale=1.0 - lambda_init),
        grid=(batch, DIFF_HEADS, seq // tq),
        in_specs=[
            pl.BlockSpec(memory_space=pltpu.SMEM),
            pl.BlockSpec(memory_space=pltpu.SMEM),
            pl.BlockSpec((None, tq, DIFF_V_DIM), lambda b, h, i: (b, i, h)),
            pl.BlockSpec((None, seq, DIFF_V_DIM), lambda b, h, i: (b, 0, DIFF_HEADS + h)),
            pl.BlockSpec((None, None, DIFF_V_DIM + V_ONES_ROWS, seq), lambda b, h, i: (b, h, 0, 0)),
            pl.BlockSpec((None, seq, LANES), lambda b, h, i: (h, 0, 0)),
            pl.BlockSpec((1, DIFF_V_DIM), lambda b, h, i: (0, 0)),
        ],
        out_specs=pl.BlockSpec((None, tq, DIFF_V_DIM), lambda b, h, i: (b, i, h)),
        out_shape=jax.ShapeDtypeStruct((batch, seq, d), BF16),
        scratch_shapes=[
            pltpu.VMEM((2, 2 * tq, 2 * LANES), BF16),
            pltpu.VMEM((tk, 2 * tq), F32), pltpu.VMEM((tk, 2 * tq), F32),
            pltpu.VMEM((1, 2 * tq), F32), pltpu.VMEM((1, 2 * tq), F32),
            pltpu.VMEM((tk, 2 * tq), BF16), pltpu.VMEM((tk, 2 * tq), BF16),
            pltpu.VMEM((1, 2 * tq), F32), pltpu.VMEM((1, 2 * tq), F32),
            pltpu.VMEM((1, 2 * tq), F32),
            pltpu.VMEM((DIFF_V_DIM + V_ONES_ROWS, 2 * tq), F32),
        ],
        compiler_params=_params(("parallel", "parallel", "parallel")),
        name="diff_attn",
    )(slopes, lam.reshape(1), qkv3, qkv3, vt, fk, subln_gain.reshape(1, DIFF_V_DIM))
    return out.reshape(batch * seq, d)


def _retention_kernel(lgf_ref, lgb_ref, q_ref, k_ref, v_ref, g_ref, o_ref, ob_scr, st_scr, *, chunk):
    h = pl.program_id(1)
    lgf = lgf_ref[h]
    lgb = lgb_ref[h]
    seq = q_ref.shape[0]
    nc = seq // chunk
    c_f = float(chunk)

    row = lax.broadcasted_iota(I32, (chunk, chunk), 0)
    col = lax.broadcasted_iota(I32, (chunk, chunk), 1)
    rel = (row - col).astype(F32)
    d_both = (jnp.where(rel >= 0, jnp.exp(lgf * jnp.maximum(rel, 0.0)), 0.0)
              + jnp.where(rel < 0, jnp.exp(lgb * jnp.maximum(-rel, 0.0)), 0.0))
    idx = lax.broadcasted_iota(I32, (chunk, 1), 0).astype(F32)
    xi_f = jnp.exp(lgf * (idx + 1.0))
    zeta_f = jnp.exp(lgf * (c_f - 1.0 - idx))
    xi_b = jnp.exp(lgb * (c_f - idx))
    zeta_b = jnp.exp(lgb * idx)
    decay_f = jnp.exp(lgf * c_f)
    decay_b = jnp.exp(lgb * c_f)

    def state_update(k, v, zeta, decay):
        kz_t = (k.astype(F32) * zeta).T.astype(BF16)
        upd = jnp.dot(kz_t, v, preferred_element_type=F32)
        st_scr[...] = st_scr[...] * decay + upd

    st_scr[...] = jnp.zeros_like(st_scr)

    def bwd_body(i, _):
        c = nc - 1 - i
        rows = pl.ds(pl.multiple_of(c * chunk, chunk), chunk)
        q = q_ref[rows, :]
        inter = jnp.dot(q, st_scr[...].astype(BF16), preferred_element_type=F32)
        ob_scr[rows, :] = inter * xi_b
        state_update(k_ref[rows, :], v_ref[rows, :], zeta_b, decay_b)
        return 0

    lax.fori_loop(0, nc, bwd_body, 0)

    st_scr[...] = jnp.zeros_like(st_scr)

    def fwd_body(c, _):
        rows = pl.ds(pl.multiple_of(c * chunk, chunk), chunk)
        q = q_ref[rows, :]
        k = k_ref[rows, :]
        v = v_ref[rows, :]
        scores = lax.dot_general(q, k, (((1,), (1,)), ((), ())), preferred_element_type=F32) * d_both
        o = jnp.dot(scores.astype(BF16), v, preferred_element_type=F32)
        o = o + jnp.dot(q, st_scr[...].astype(BF16), preferred_element_type=F32) * xi_f
        o = o + ob_scr[rows, :]
        state_update(k, v, zeta_f, decay_f)
        g = g_ref[rows, :].astype(F32)
        o_ref[rows, :] = (g * jax.nn.sigmoid(g) * _rms(o)).astype(BF16)
        return 0

    lax.fori_loop(0, nc, fwd_body, 0)


def _retention(proj, lg_fwd, lg_bwd, *, batch, seq, d_model, chunk):
    dk = d_model // RET_HEADS
    dv = 2 * dk
    vw = RET_HEADS * dv
    p3 = proj.reshape(batch, seq, 2 * d_model + 2 * vw)
    out = pl.pallas_call(
        functools.partial(_retention_kernel, chunk=chunk),
        grid=(batch, RET_HEADS),
        in_specs=[
            pl.BlockSpec(memory_space=pltpu.SMEM),
            pl.BlockSpec(memory_space=pltpu.SMEM),
            pl.BlockSpec((None, seq, dk), lambda b, h: (b, 0, h)),
            pl.BlockSpec((None, seq, dk), lambda b, h: (b, 0, RET_HEADS + h)),
            pl.BlockSpec((None, seq, dv), lambda b, h: (b, 0, RET_HEADS + h)),
            pl.BlockSpec((None, seq, dv), lambda b, h: (b, 0, 2 * RET_HEADS + h)),
        ],
        out_specs=pl.BlockSpec((None, seq, dv), lambda b, h: (b, 0, h)),
        out_shape=jax.ShapeDtypeStruct((batch, seq, vw), BF16),
        scratch_shapes=[pltpu.VMEM((seq, dv), F32), pltpu.VMEM((dk, dv), F32)],
        compiler_params=_params(("parallel", "parallel")),
        name="retention",
    )(lg_fwd, lg_bwd, p3, p3, p3, p3)
    return out.reshape(batch * seq, vw)


def _out_router_kernel(o_ref, w_ref, x_ref, g_ref, rwh_ref, rwl_ref, rb_ref,
                       xn_ref, hm_ref, idx_ref, wts_ref, rank_ref, cnt_ref, carry_scr):
    i = pl.program_id(0)
    tm = x_ref.shape[0]

    @pl.when(i == 0)
    def _():
        carry_scr[...] = jnp.zeros_like(carry_scr)

    xn = x_ref[...] + jnp.dot(o_ref[...], w_ref[...], preferred_element_type=F32)
    xn_ref[...] = xn
    hm = _rms(xn) * g_ref[...]
    hm_ref[...] = hm

    hm_hi = hm.astype(BF16)
    hm_lo = (hm - hm_hi.astype(F32)).astype(BF16)
    nt = (((1,), (1,)), ((), ()))
    logits = (lax.dot_general(rwh_ref[...], hm_hi, nt, preferred_element_type=F32)
              + lax.dot_general(rwh_ref[...], hm_lo, nt, preferred_element_type=F32)
              + lax.dot_general(rwl_ref[...], hm_hi, nt, preferred_element_type=F32)
              + rb_ref[...])

    eidx = lax.broadcasted_iota(I32, (N_EXPERTS, tm), 0)
    work = logits
    vals, sels, hots = [], [], []
    for _ in range(TOP_K):
        m = jnp.max(work, axis=0, keepdims=True)
        sel = jnp.min(jnp.where(work == m, eidx, N_EXPERTS), axis=0, keepdims=True)
        hot = eidx == sel
        vals.append(m)
        sels.append(sel)
        hots.append(hot)
        work = jnp.where(hot, -jnp.inf, work)
    exps = [jnp.exp(v - vals[0]) for v in vals]
    denom = exps[0] + exps[1] + exps[2] + exps[3]

    mask = jnp.where(hots[0] | hots[1] | hots[2] | hots[3], 1.0, 0.0)
    tri = (lax.broadcasted_iota(I32, (tm, tm), 0) < lax.broadcasted_iota(I32, (tm, tm), 1))
    excl = jnp.dot(mask.astype(BF16), jnp.where(tri, 1.0, 0.0).astype(BF16), preferred_element_type=F32)
    rank_e = excl + carry_scr[:, 0:1]
    for k in range(TOP_K):
        idx_ref[k:k + 1, :] = sels[k]
        wts_ref[k:k + 1, :] = exps[k] / denom
        rank_ref[k:k + 1, :] = jnp.sum(jnp.where(hots[k], rank_e, 0.0), axis=0, keepdims=True).astype(I32)
    carry_scr[...] = carry_scr[...] + jnp.sum(mask, axis=1, keepdims=True)
    cnt_ref[...] = carry_scr[...].astype(I32)


def _out_router(o, w_out_bf16, x, moe_gain, rw_hi_t, rw_lo_t, router_b, *, tm):
    t, d = x.shape
    dv = o.shape[1]
    tok = lambda i: (i, 0)
    fixed = lambda i: (0, 0)
    lanes_tok = lambda i: (0, i)
    return pl.pallas_call(
        _out_router_kernel,
        grid=(t // tm,),
        in_specs=[
            pl.BlockSpec((tm, dv), tok),
            pl.BlockSpec((dv, d), fixed),
            pl.BlockSpec((tm, d), tok),
            pl.BlockSpec((1, d), fixed),
            pl.BlockSpec((N_EXPERTS, d), fixed),
            pl.BlockSpec((N_EXPERTS, d), fixed),
            pl.BlockSpec((N_EXPERTS, 1), fixed),
        ],
        out_specs=[
            pl.BlockSpec((tm, d), tok),
            pl.BlockSpec((tm, d), tok),
            pl.BlockSpec((TOP_K, tm), lanes_tok),
            pl.BlockSpec((TOP_K, tm), lanes_tok),
            pl.BlockSpec((TOP_K, tm), lanes_tok),
            pl.BlockSpec((N_EXPERTS, LANES), fixed),
        ],
        out_shape=[
            jax.ShapeDtypeStruct((t, d), F32),
            jax.ShapeDtypeStruct((t, d), F32),
            jax.ShapeDtypeStruct((TOP_K, t), I32),
            jax.ShapeDtypeStruct((TOP_K, t), F32),
            jax.ShapeDtypeStruct((TOP_K, t), I32),
            jax.ShapeDtypeStruct((N_EXPERTS, LANES), I32),
        ],
        scratch_shapes=[pltpu.VMEM((N_EXPERTS, LANES), F32)],
        compiler_params=_params(("arbitrary",)),
        name="out_router",
    )(o, w_out_bf16, x, moe_gain.reshape(1, d), rw_hi_t, rw_lo_t, router_b.reshape(N_EXPERTS, 1))


SUBLANES = 8


def _idx_slot_copy(dest_hbm, idx_smem, idx_sem, step, s):
    n_idx = dest_hbm.shape[1]
    return pltpu.make_async_copy(dest_hbm.at[step], idx_smem.at[pl.ds(pl.multiple_of(s * n_idx, n_idx), n_idx)],
                                 idx_sem.at[s])


def _dispatch_kernel(ends_ref, padded_ref, dest_hbm, hm_ref, xs_hbm, idx_smem, zero_scr, idx_sem, fill_sem, row_sem,
                     *, tm, tr, n_tiles):
    i = pl.program_id(0)
    n = pl.num_programs(0)
    slot = i % 2
    groups = tm // SUBLANES

    def idx_copy(step, s):
        return _idx_slot_copy(dest_hbm, idx_smem, idx_sem, step, s)

    @pl.when(i == 0)
    def _():
        idx_copy(0, 0).start()
        zero_scr[...] = jnp.zeros_like(zero_scr)

        def fill(start):
            return pltpu.make_async_copy(zero_scr, xs_hbm.at[pl.ds(pl.multiple_of(start, tr), tr), :], fill_sem)

        first_unused = ends_ref[N_EXPERTS - 1] // tr
        for e in range(N_EXPERTS):
            @pl.when(padded_ref[e] > 0)
            def _():
                fill(ends_ref[e] - tr).start()
        lax.fori_loop(first_unused, n_tiles, lambda c, _: (fill(c * tr).start(), 0)[1], 0)
        for e in range(N_EXPERTS):
            @pl.when(padded_ref[e] > 0)
            def _():
                fill(0).wait()
        lax.fori_loop(first_unused, n_tiles, lambda c, _: (fill(0).wait(), 0)[1], 0)

    @pl.when(i + 1 < n)
    def _():
        idx_copy(i + 1, 1 - slot).start()

    idx_copy(i, slot).wait()
    idx_base = slot * (TOP_K * tm)

    def start_group(g, _):
        for u in range(SUBLANES):
            src = hm_ref.at[g, pl.ds(u, 1), :]
            for k in range(TOP_K):
                row = idx_smem[idx_base + k * tm + g * SUBLANES + u]
                pltpu.make_async_copy(src, xs_hbm.at[pl.ds(row, 1), :], row_sem).start(priority=k % 2)
        return 0

    lax.fori_loop(0, groups, start_group, 0)

    for _ in range(TOP_K):
        pltpu.make_async_copy(xs_hbm.at[pl.ds(0, tm), :], xs_hbm.at[pl.ds(0, tm), :], row_sem).wait()


def _dispatch(hm, dest_tiles, ends, padded, *, tm, tr, n_rows):
    t, d = hm.shape
    return pl.pallas_call(
        functools.partial(_dispatch_kernel, tm=tm, tr=tr, n_tiles=n_rows // tr),
        grid_spec=pltpu.PrefetchScalarGridSpec(
            num_scalar_prefetch=2,
            grid=(t // tm,),
            in_specs=[
                pl.BlockSpec(memory_space=pl.ANY),
                pl.BlockSpec((tm // SUBLANES, SUBLANES, d), lambda i, *_: (i, 0, 0)),
            ],
            out_specs=pl.BlockSpec(memory_space=pl.ANY),
            scratch_shapes=[
                pltpu.SMEM((2 * TOP_K * tm,), I32),
                pltpu.VMEM((tr, d), F32),
                pltpu.SemaphoreType.DMA((2,)),
                pltpu.SemaphoreType.DMA(()),
                pltpu.SemaphoreType.DMA(()),
            ],
        ),
        out_shape=jax.ShapeDtypeStruct((n_rows, d), F32),
        compiler_params=_params(("arbitrary",)),
        name="moe_dispatch",
    )(ends, padded, dest_tiles, hm.reshape(t // SUBLANES, SUBLANES, d))


MXU_WIDTH = 256


def _split_w1_kernel(w_ref, perm_ref, g_ref, l_ref):
    half = MXU_WIDTH // 2
    for c in range(w_ref.shape[1] // MXU_WIDTH):
        y = jnp.dot(w_ref[:, c * MXU_WIDTH:(c + 1) * MXU_WIDTH].astype(BF16), perm_ref[...],
                    preferred_element_type=F32)
        g_ref[:, c * half:(c + 1) * half] = y[:, :half].astype(BF16)
        l_ref[:, c * half:(c + 1) * half] = y[:, half:].astype(BF16)


def _split_w1(w1_layers, layer):
    _, e, d, ff2 = w1_layers.shape
    ff = ff2 // 2
    rows = min(512, d)
    src = jnp.arange(MXU_WIDTH)
    dst = jnp.where(src % 2 == 0, src // 2, MXU_WIDTH // 2 + src // 2)
    perm = (dst[:, None] == jnp.arange(MXU_WIDTH)[None, :]).astype(BF16)
    blk = lambda i, r: (i, r, 0)
    return pl.pallas_call(
        _split_w1_kernel,
        grid=(e, d // rows),
        in_specs=[pl.BlockSpec((None, None, rows, ff2), lambda i, r: (layer, i, r, 0)),
                  pl.BlockSpec((MXU_WIDTH, MXU_WIDTH), lambda i, r: (0, 0))],
        out_specs=[pl.BlockSpec((None, rows, ff), blk), pl.BlockSpec((None, rows, ff), blk)],
        out_shape=[jax.ShapeDtypeStruct((e, d, ff), BF16), jax.ShapeDtypeStruct((e, d, ff), BF16)],
        compiler_params=_params(("parallel", "parallel")),
        name="split_w1",
    )(w1_layers, perm)


def _experts_kernel(te_ref, tv_ref, xs_ref, w1g_ref, w1l_ref, b1g_ref, b1l_ref, w2_ref, b2_ref, ys_ref):
    i = pl.program_id(0)

    @pl.when(tv_ref[i] > 0)
    def _():
        x = xs_ref[...].astype(BF16)
        glu = jnp.dot(x, w1g_ref[...], preferred_element_type=F32) + b1g_ref[...]
        lin = jnp.dot(x, w1l_ref[...], preferred_element_type=F32) + b1l_ref[...]
        glu = jnp.minimum(glu, SWIGLU_LIMIT)
        lin = jnp.clip(lin, -SWIGLU_LIMIT, SWIGLU_LIMIT)
        act = glu * jax.nn.sigmoid(SWIGLU_ALPHA * glu) * (lin + 1.0)
        ys_ref[...] = (jnp.dot(act.astype(BF16), w2_ref[...].astype(BF16), preferred_element_type=F32)
                       + b2_ref[...])

    @pl.when(tv_ref[i] == 0)
    def _():
        ys_ref[...] = jnp.zeros_like(ys_ref)


def _experts(xs, tile_expert, tile_valid, w1g, w1l, b1g, b1l, w2_layers, layer, b2, *, tr):
    n_rows, d = xs.shape
    ff = w1g.shape[2]
    n_tiles = n_rows // tr
    row_tile = lambda i, te, tv: (i, 0)
    by_expert = lambda i, te, tv: (te[i], 0, 0)
    return pl.pallas_call(
        _experts_kernel,
        grid_spec=pltpu.PrefetchScalarGridSpec(
            num_scalar_prefetch=2,
            grid=(n_tiles,),
            in_specs=[
                pl.BlockSpec((tr, d), row_tile),
                pl.BlockSpec((None, d, ff), by_expert),
                pl.BlockSpec((None, d, ff), by_expert),
                pl.BlockSpec((None, 1, ff), by_expert),
                pl.BlockSpec((None, 1, ff), by_expert),
                pl.BlockSpec((None, None, ff, d), lambda i, te, tv: (layer, te[i], 0, 0)),
                pl.BlockSpec((None, 1, d), by_expert),
            ],
            out_specs=pl.BlockSpec((tr, d), row_tile),
        ),
        out_shape=jax.ShapeDtypeStruct((n_rows, d), F32),
        compiler_params=_params(("arbitrary",)),
        name="moe_experts",
    )(tile_expert, tile_valid, xs, w1g, w1l, b1g, b1l, w2_layers, b2)


def _combine_kernel(dest_hbm, ys_hbm, x_ref, wt_ref, o_ref, idx_smem, rows_scr, idx_sem, row_sem, *, tm):
    i = pl.program_id(0)
    n = pl.num_programs(0)
    slot = i % 2
    groups = TOP_K * tm // SUBLANES

    def idx_copy(step, s):
        return _idx_slot_copy(dest_hbm, idx_smem, idx_sem, step, s)

    def gather_rows(s):
        idx_base = s * (TOP_K * tm)

        def start_group(g, _):
            for u in range(SUBLANES):
                row = idx_smem[idx_base + g * SUBLANES + u]
                pltpu.make_async_copy(ys_hbm.at[pl.ds(row, 1), :], rows_scr.at[s, g, pl.ds(u, 1), :],
                                      row_sem.at[s]).start(priority=u % 2)
            return 0

        lax.fori_loop(0, groups, start_group, 0)

    @pl.when(i == 0)
    def _():
        idx_copy(0, 0).start()
        idx_copy(0, 0).wait()
        gather_rows(0)

        @pl.when(n > 1)
        def _():
            idx_copy(1, 1).start()

    @pl.when(i + 1 < n)
    def _():
        idx_copy(i + 1, 1 - slot).wait()
        gather_rows(1 - slot)

    @pl.when(i + 2 < n)
    def _():
        idx_copy(i + 2, slot).start()

    per_k = tm // SUBLANES
    for k in range(TOP_K):
        part = rows_scr.at[slot, pl.ds(k * per_k, per_k)]
        pltpu.make_async_copy(part, part, row_sem.at[slot]).wait()

    out = x_ref[...]
    for k in range(TOP_K):
        rows = rows_scr[slot, k * per_k:(k + 1) * per_k].reshape(tm, x_ref.shape[1])
        out = out + wt_ref[:, k:k + 1] * rows
    o_ref[...] = out


def _combine(ys, dest_tiles, x, wts_t, *, tm):
    t, d = x.shape
    return pl.pallas_call(
        functools.partial(_combine_kernel, tm=tm),
        grid=(t // tm,),
        in_specs=[
            pl.BlockSpec(memory_space=pl.ANY),
            pl.BlockSpec(memory_space=pl.ANY),
            pl.BlockSpec((tm, d), lambda i: (i, 0)),
            pl.BlockSpec((tm, TOP_K), lambda i: (i, 0)),
        ],
        out_specs=pl.BlockSpec((tm, d), lambda i: (i, 0)),
        out_shape=jax.ShapeDtypeStruct((t, d), F32),
        scratch_shapes=[
            pltpu.SMEM((2 * TOP_K * tm,), I32),
            pltpu.VMEM((2, TOP_K * tm // SUBLANES, SUBLANES, d), F32),
            pltpu.SemaphoreType.DMA((2,)),
            pltpu.SemaphoreType.DMA((2,)),
        ],
        compiler_params=_params(("arbitrary",)),
        name="moe_combine",
    )(dest_tiles, ys, x, wts_t)


def _tiles(batch, seq):
    t = batch * seq
    tm = min(512, t)
    return dict(
        tm=tm,
        tn=256,
        tq=min(512, seq),
        tk=min(512, seq),
        chunk=min(256, seq),
        tmd=min(256, t),
        tr=min(512, max(8, TOP_K * t // N_EXPERTS)),
    )


def _moe(o, w_out, x, moe_gain, router_w, router_b, w1_layers, b1, w2_layers, b2, layer, *, cfg):
    t, d = x.shape
    tm, tmd, tr = cfg["tm"], cfg["tmd"], cfg["tr"]
    rw_t = router_w.T
    rw_hi = rw_t.astype(BF16)
    rw_lo = (rw_t - rw_hi.astype(F32)).astype(BF16)
    x, hm, idx, wts, rank, cnt = _out_router(o, w_out.astype(BF16), x, moe_gain, rw_hi, rw_lo, router_b, tm=tm)

    counts = cnt[:, 0]
    padded = ((counts + tr - 1) // tr) * tr
    ends = jnp.cumsum(padded).astype(I32)
    offs = ends - padded
    experts = jnp.arange(N_EXPERTS, dtype=I32)[:, None, None]
    dest = rank + jnp.sum(jnp.where(idx[None] == experts, offs[:, None, None], 0), axis=0)
    dest_tiles = dest.reshape(TOP_K, t // tmd, tmd).transpose(1, 0, 2).reshape(t // tmd, TOP_K * tmd)
    n_rows = TOP_K * t + N_EXPERTS * tr
    tile_start = jnp.arange(n_rows // tr, dtype=I32) * tr
    tile_expert = jnp.minimum(jnp.sum((tile_start[:, None] >= ends[None, :]).astype(I32), axis=1), N_EXPERTS - 1)
    tile_valid = (tile_start < ends[-1]).astype(I32)

    xs = _dispatch(hm, dest_tiles, ends, padded.astype(I32), tm=tmd, tr=tr, n_rows=n_rows)
    ff = w2_layers.shape[2]
    w1g, w1l = _split_w1(w1_layers, layer)
    ys = _experts(xs, tile_expert, tile_valid, w1g, w1l,
                  b1[:, 0::2].reshape(N_EXPERTS, 1, ff), b1[:, 1::2].reshape(N_EXPERTS, 1, ff),
                  w2_layers, layer, b2.reshape(N_EXPERTS, 1, d), tr=tr)
    return _combine(ys, dest_tiles, x, wts.T, tm=tmd)


def kernel(x, diff_norm, diff_w_in, diff_w_out, diff_lambda_q1, diff_lambda_k1, diff_lambda_q2, diff_lambda_k2, diff_q_norm, diff_k_norm, diff_subln, ret_norm, ret_w_in, ret_w_out, ret_log_decay_fwd, ret_log_decay_bwd, moe_norm, moe_router_w, moe_router_b, moe_w1, moe_b1, moe_w2, moe_b2):
    batch, seq, d = x.shape
    depth = moe_norm.shape[0]
    cfg = _tiles(batch, seq)
    x = x.reshape(batch * seq, d)
    slopes = jnp.exp2(-8.0 * jnp.arange(1, DIFF_HEADS + 1, dtype=F32) / DIFF_HEADS)
    for i in range(depth):
        j = i // N_MIXERS
        if i % N_MIXERS == 0:
            lambda_init = _lambda_init(i)
            lam = (jnp.exp(jnp.sum(diff_lambda_q1[j] * diff_lambda_k1[j]))
                   - jnp.exp(jnp.sum(diff_lambda_q2[j] * diff_lambda_k2[j])) + lambda_init)
            reps = d // DIFF_HEAD_DIM
            qk_gain = jnp.concatenate([jnp.tile(diff_q_norm[j], reps) * (DIFF_HEAD_DIM ** -0.5 * LOG2E),
                                       jnp.tile(diff_k_norm[j], reps)]).reshape(1, 2 * d)
            qk, vt = _in_proj(x, diff_norm[j], diff_w_in[j].astype(BF16), tm=cfg["tm"], tn=cfg["tn"],
                              qk_gain_row=qk_gain, values_layout=(batch, seq))
            o = _diff_attention(qk, vt, slopes, lam, diff_subln[j], lambda_init,
                                batch=batch, seq=seq, tq=cfg["tq"], tk=cfg["tk"])
            w_out = diff_w_out[j]
        else:
            dk = d // RET_HEADS
            col_scale = jnp.ones((ret_w_in.shape[2],), F32).at[d:2 * d].set(dk ** -0.5)
            proj, = _in_proj(x, ret_norm[j], (ret_w_in[j] * col_scale).astype(BF16), tm=cfg["tm"], tn=cfg["tn"])
            o = _retention(proj, ret_log_decay_fwd[j], ret_log_decay_bwd[j],
                           batch=batch, seq=seq, d_model=d, chunk=cfg["chunk"])
            w_out = ret_w_out[j]
        x = _moe(o, w_out, x, moe_norm[i], moe_router_w[i], moe_router_b[i],
                 moe_w1, moe_b1[i], moe_w2, moe_b2[i], i, cfg=cfg)
    return x.reshape(batch, seq, d)
```

```python
import functools
import math

import jax
import jax.numpy as jnp
from jax import lax
from jax.experimental import pallas as pl
from jax.experimental.pallas import tpu as pltpu

F32 = jnp.float32
BF16 = jnp.bfloat16
I32 = jnp.int32

NORM_EPS = 1e-5
N_MIXERS = 2

DIFF_HEADS = 8
DIFF_HEAD_DIM = 64
DIFF_V_DIM = 2 * DIFF_HEAD_DIM

RET_HEADS = 4

N_EXPERTS = 32
TOP_K = 4
SWIGLU_ALPHA = 1.702
SWIGLU_LIMIT = 7.0

V7X_VMEM_LIMIT_BYTES = 56 * 1024 * 1024
LANES = 128


def _lambda_init(layer_idx):
    return 0.8 - 0.6 * math.exp(-0.3 * layer_idx)


def _params(semantics):
    return pltpu.CompilerParams(dimension_semantics=semantics, vmem_limit_bytes=V7X_VMEM_LIMIT_BYTES)


def _rms(x, eps=NORM_EPS):
    return x * lax.rsqrt(jnp.mean(x * x, axis=-1, keepdims=True) + eps)


def _in_proj_kernel(x_ref, g_ref, w_ref, gmat_ref, qkg_ref, o_ref, *rest, tn, n_normed, n_transposed):
    vt_ref, h_scr = rest if n_transposed else (None, rest[0])
    h_scr[...] = (_rms(x_ref[...]) * g_ref[...]).astype(BF16)
    n_total = w_ref.shape[1]
    n_chunks = n_total // tn
    n_rowmajor = n_total - n_transposed
    if n_transposed:
        vt_ref[:, DIFF_V_DIM:, :] = jnp.ones((vt_ref.shape[0], V_ONES_ROWS, vt_ref.shape[2]), BF16)

    def project(c):
        return jnp.dot(h_scr[...], w_ref[:, c * tn:(c + 1) * tn], preferred_element_type=F32)

    def finish(c, acc):
        cols = slice(c * tn, (c + 1) * tn)
        if c * tn < n_normed:
            sq = acc * acc
            hi = sq.astype(BF16)
            lo = (sq - hi.astype(F32)).astype(BF16)
            msq = (jnp.dot(hi, gmat_ref[...], preferred_element_type=F32)
                   + jnp.dot(lo, gmat_ref[...], preferred_element_type=F32))
            acc = acc * lax.rsqrt(msq + NORM_EPS) * qkg_ref[:, cols]
        if c * tn < n_rowmajor:
            o_ref[:, cols] = acc.astype(BF16)
        else:
            for part in range(tn // DIFF_V_DIM):
                head = (c * tn - n_rowmajor) // DIFF_V_DIM + part
                vt_ref[head, :DIFF_V_DIM, :] = acc[:, part * DIFF_V_DIM:(part + 1) * DIFF_V_DIM].T.astype(BF16)

    acc = project(0)
    for c in range(1, n_chunks):
        nxt = project(c)
        finish(c - 1, acc)
        acc = nxt
    finish(n_chunks - 1, acc)


def _in_proj(x, gain, w_bf16, *, tm, tn, qk_gain_row=None, values_layout=None):
    t, d = x.shape
    n = w_bf16.shape[1]
    if qk_gain_row is None:
        n_normed = 0
        qk_gain_row = jnp.zeros((1, n), F32)
    else:
        n_normed = qk_gain_row.shape[1]
        qk_gain_row = jnp.pad(qk_gain_row, ((0, 0), (0, n - n_normed)))
    grp = jnp.arange(tn) // DIFF_HEAD_DIM
    gmat = jnp.where(grp[:, None] == grp[None, :], 1.0 / DIFF_HEAD_DIM, 0.0).astype(BF16)
    n_transposed = DIFF_HEADS * DIFF_V_DIM if values_layout else 0
    n_rowmajor = n - n_transposed
    out_specs = [pl.BlockSpec((tm, n_rowmajor), lambda i: (i, 0))]
    out_shape = [jax.ShapeDtypeStruct((t, n_rowmajor), BF16)]
    if values_layout:
        batch, seq = values_layout
        per_seq = seq // tm
        rows = DIFF_V_DIM + V_ONES_ROWS
        out_specs.append(pl.BlockSpec((None, DIFF_HEADS, rows, tm), lambda i: (i // per_seq, 0, 0, i % per_seq)))
        out_shape.append(jax.ShapeDtypeStruct((batch, DIFF_HEADS, rows, seq), BF16))
    return pl.pallas_call(
        functools.partial(_in_proj_kernel, tn=tn, n_normed=n_normed, n_transposed=n_transposed),
        grid=(t // tm,),
        in_specs=[
            pl.BlockSpec((tm, d), lambda i: (i, 0)),
            pl.BlockSpec((1, d), lambda i: (0, 0)),
            pl.BlockSpec((d, n), lambda i: (0, 0)),
            pl.BlockSpec((tn, tn), lambda i: (0, 0)),
            pl.BlockSpec((1, n), lambda i: (0, 0)),
        ],
        out_specs=out_specs,
        out_shape=out_shape,
        scratch_shapes=[pltpu.VMEM((tm, d), BF16)],
        compiler_params=_params(("parallel",)),
        name="in_proj",
    )(x, gain.reshape(1, d), w_bf16, gmat, qk_gain_row)


LOG2E = math.log2(math.e)
LOG2E_BF16_PARTS = (1.4453125, -0.00262451171875, 7.063150405883789e-06)
ALIBI_LOW_BITS = 255
ALIBI_HIGH_BIT = 256
V_ONES_ROWS = 16


def _alibi_key_features(slopes, seq, tk):
    dj = jnp.arange(seq, dtype=I32) % tk
    lo = (dj & ALIBI_LOW_BITS).astype(F32)
    hi = (dj & ALIBI_HIGH_BIT).astype(F32)
    lane = jnp.arange(LANES)
    feat = jnp.where(lane[None, :] < 3, lo[:, None], jnp.where(lane[None, :] < 6, hi[:, None], 0.0))
    return (slopes[:, None, None] * feat[None]).astype(BF16)


def _diff_attn_kernel(slopes_ref, lam_ref, q_ref, k_ref, vt_ref, fk_ref, sg_ref, o_ref,
                      qa_scr, s0_scr, s1_scr, mt0_scr, mt1_scr, p0_scr, p1_scr, a0_scr, a1_scr,
                      m_scr, acc_scr, *, tq, tk, out_scale):
    h = pl.program_id(1)
    qi = pl.program_id(2)
    slope2 = slopes_ref[h] * LOG2E
    lam = lam_ref[0]
    nk = k_ref.shape[0] // tk
    i0 = qi * tq
    jd = i0 // tk
    cols2 = 2 * tq

    q = q_ref[...]
    lane = lax.broadcasted_iota(I32, q.shape, 1)
    zero = jnp.zeros_like(q)
    qq = jnp.concatenate([jnp.where(lane < DIFF_HEAD_DIM, q, zero),
                          jnp.where(lane >= DIFF_HEAD_DIM, q, zero)], axis=0)
    flane = lax.broadcasted_iota(I32, (cols2, LANES), 1)
    a, b, c = LOG2E_BF16_PARTS
    feat = jnp.where(flane % 3 == 0, a, jnp.where(flane % 3 == 1, b, c))
    feat = jnp.where(flane < 6, feat, 0.0)
    qa_scr[0] = jnp.concatenate([qq, feat.astype(BF16)], axis=1)
    qa_scr[1] = jnp.concatenate([qq, (-feat).astype(BF16)], axis=1)
    col = lax.broadcasted_iota(I32, (1, cols2), 1)
    q_pos = (i0 + jnp.where(col >= tq, col - tq, col)).astype(F32)

    m_scr[...] = jnp.full_like(m_scr, -jnp.inf)
    acc_scr[...] = jnp.zeros_like(acc_scr)
    nt = (((1,), (1,)), ((), ()))

    s_scr, mt_scr, p_scr, a_scr = (s0_scr, s1_scr), (mt0_scr, mt1_scr), (p0_scr, p1_scr), (a0_scr, a1_scr)

    def key_rows(j):
        return pl.ds(pl.multiple_of(j * tk, tk), tk)

    def exponentials(slot, col_shift):
        for cb in range(cols2 // LANES):
            cs = slice(cb * LANES, (cb + 1) * LANES)
            shift = col_shift[:, cs]
            m_old = m_scr[:, cs]
            m_new = jnp.maximum(m_old, mt_scr[slot][:, cs] - shift)
            alpha = jnp.exp2(m_old - m_new)
            p = jnp.exp2((s_scr[slot][:, cs] - (m_new + shift)).astype(BF16))
            m_scr[:, cs] = m_new
            a_scr[slot][:, cs] = alpha
            p_scr[slot][:, cs] = p

    def values(slot, j):
        acc_scr[...] = (a_scr[slot][...] * acc_scr[...]
                        + jnp.dot(vt_ref[:, key_rows(j)], p_scr[slot][...], preferred_element_type=F32))

    s = lax.dot_general(k_ref[key_rows(jd), :], qq, nt, preferred_element_type=F32)
    key_pos = (jd * tk + lax.broadcasted_iota(I32, (tk, 1), 0)).astype(F32)
    s = s - slope2 * jnp.abs(key_pos - q_pos)
    s_scr[1][...] = s
    mt_scr[1][...] = jnp.max(s, axis=0, keepdims=True)

    n_lin = nk - 1

    def tile_of(t):
        return t + (t >= jd).astype(I32)

    def scores(t, slot):
        j = tile_of(t)
        k_aug = jnp.concatenate([k_ref[key_rows(j), :], fk_ref[key_rows(j), :]], axis=1)
        s = lax.dot_general(k_aug, qa_scr[(j > jd).astype(I32)], nt, preferred_element_type=F32)
        s_scr[slot][...] = s
        mt_scr[slot][...] = jnp.max(s, axis=0, keepdims=True)

    def shift_of(t):
        j = tile_of(t)
        d = slope2 * (q_pos - (j * tk).astype(F32))
        return jnp.where(j > jd, -d, d)

    no_shift = jnp.zeros((1, cols2), F32)
    if n_lin == 0:
        exponentials(1, no_shift)
        values(1, jd)
    else:
        scores(0, 0)
        exponentials(1, no_shift)

        def stage(t, slot, with_scores=True):
            if with_scores:
                scores(t + 1, 1 - slot)
            values(1 - slot, jnp.where(t == 0, jd, tile_of(jnp.maximum(t - 1, 0))))
            exponentials(slot, shift_of(t))

        def pair(tt, _):
            stage(2 * tt, 0)
            stage(2 * tt + 1, 1)
            return 0

        lax.fori_loop(0, (n_lin - 1) // 2, pair, 0)
        if (n_lin - 1) % 2:
            stage(jnp.int32(n_lin - 2), (n_lin - 2) % 2)
        stage(jnp.int32(n_lin - 1), (n_lin - 1) % 2, with_scores=False)
        values((n_lin - 1) % 2, tile_of(jnp.int32(n_lin - 1)))

    o = acc_scr[:DIFF_V_DIM, :] / acc_scr[DIFF_V_DIM:DIFF_V_DIM + 1, :]
    o = o[:, :tq] - lam * o[:, tq:]
    o = o * lax.rsqrt(jnp.mean(o * o, axis=0, keepdims=True) + NORM_EPS)
    o_ref[...] = (o.T * sg_ref[...] * out_scale).astype(BF16)


def _diff_attention(qk, vt, slopes, lam, subln_gain, lambda_init, *, batch, seq, tq, tk):
    assert tk % tq == 0 and tk <= 2 * ALIBI_HIGH_BIT and tq % LANES == 0
    d = DIFF_HEADS * DIFF_V_DIM
    qkv3 = qk.reshape(batch, seq, 2 * d)
    fk = _alibi_key_features(slopes, seq, tk)
    out = pl.pallas_call(
        functools.partial(_diff_attn_kernel, tq=tq, tk=tk, out_scale=1.0 - lambda_init),
        grid=(batch, DIFF_HEADS, seq // tq),
        in_specs=[
            pl.BlockSpec(memory_space=pltpu.SMEM),
            pl.BlockSpec(memory_space=pltpu.SMEM),
            pl.BlockSpec((None, tq, DIFF_V_DIM), lambda b, h, i: (b, i, h)),
            pl.BlockSpec((None, seq, DIFF_V_DIM), lambda b, h, i: (b, 0, DIFF_HEADS + h)),
            pl.BlockSpec((None, None, DIFF_V_DIM + V_ONES_ROWS, seq), lambda b, h, i: (b, h, 0, 0)),
            pl.BlockSpec((None, seq, LANES), lambda b, h, i: (h, 0, 0)),
            pl.BlockSpec((1, DIFF_V_DIM), lambda b, h, i: (0, 0)),
        ],
        out_specs=pl.BlockSpec((None, tq, DIFF_V_DIM), lambda b, h, i: (b, i, h)),
        out_shape=jax.ShapeDtypeStruct((batch, seq, d), BF16),
        scratch_shapes=[
            pltpu.VMEM((2, 2 * tq, 2 * LANES), BF16),
            pltpu.VMEM((tk, 2 * tq), F32), pltpu.VMEM((tk, 2 * tq), F32),
            pltpu.VMEM((1, 2 * tq), F32), pltpu.VMEM((1, 2 * tq), F32),
            pltpu.VMEM((tk, 2 * tq), BF16), pltpu.VMEM((tk, 2 * tq), BF16),
            pltpu.VMEM((1, 2 * tq), F32), pltpu.VMEM((1, 2 * tq), F32),
            pltpu.VMEM((1, 2 * tq), F32),
            pltpu.VMEM((DIFF_V_DIM + V_ONES_ROWS, 2 * tq), F32),
        ],
        compiler_params=_params(("parallel", "parallel", "parallel")),
        name="diff_attn",
    )(slopes, lam.reshape(1), qkv3, qkv3, vt, fk, subln_gain.reshape(1, DIFF_V_DIM))
    return out.reshape(batch * seq, d)


def _retention_kernel(lgf_ref, lgb_ref, q_ref, k_ref, v_ref, g_ref, o_ref, ob_scr, st_scr, *, chunk):
    h = pl.program_id(1)
    lgf = lgf_ref[h]
    lgb = lgb_ref[h]
    seq = q_ref.shape[0]
    nc = seq // chunk
    c_f = float(chunk)

    row = lax.broadcasted_iota(I32, (chunk, chunk), 0)
    col = lax.broadcasted_iota(I32, (chunk, chunk), 1)
    rel = (row - col).astype(F32)
    d_both = (jnp.where(rel >= 0, jnp.exp(lgf * jnp.maximum(rel, 0.0)), 0.0)
              + jnp.where(rel < 0, jnp.exp(lgb * jnp.maximum(-rel, 0.0)), 0.0))
    idx = lax.broadcasted_iota(I32, (chunk, 1), 0).astype(F32)
    xi_f = jnp.exp(lgf * (idx + 1.0))
    zeta_f = jnp.exp(lgf * (c_f - 1.0 - idx))
    xi_b = jnp.exp(lgb * (c_f - idx))
    zeta_b = jnp.exp(lgb * idx)
    decay_f = jnp.exp(lgf * c_f)
    decay_b = jnp.exp(lgb * c_f)

    def state_update(k, v, zeta, decay):
        kz_t = (k.astype(F32) * zeta).T.astype(BF16)
        upd = jnp.dot(kz_t, v, preferred_element_type=F32)
        st_scr[...] = st_scr[...] * decay + upd

    st_scr[...] = jnp.zeros_like(st_scr)

    def bwd_body(i, _):
        c = nc - 1 - i
        rows = pl.ds(pl.multiple_of(c * chunk, chunk), chunk)
        q = q_ref[rows, :]
        inter = jnp.dot(q, st_scr[...].astype(BF16), preferred_element_type=F32)
        ob_scr[rows, :] = inter * xi_b
        state_update(k_ref[rows, :], v_ref[rows, :], zeta_b, decay_b)
        return 0

    lax.fori_loop(0, nc, bwd_body, 0)

    st_scr[...] = jnp.zeros_like(st_scr)

    def fwd_body(c, _):
        rows = pl.ds(pl.multiple_of(c * chunk, chunk), chunk)
        q = q_ref[rows, :]
        k = k_ref[rows, :]
        v = v_ref[rows, :]
        scores = lax.dot_general(q, k, (((1,), (1,)), ((), ())), preferred_element_type=F32) * d_both
        o = jnp.dot(scores.astype(BF16), v, preferred_element_type=F32)
        o = o + jnp.dot(q, st_scr[...].astype(BF16), preferred_element_type=F32) * xi_f
        o = o + ob_scr[rows, :]
        state_update(k, v, zeta_f, decay_f)
        g = g_ref[rows, :].astype(F32)
        o_ref[rows, :] = (g * jax.nn.sigmoid(g) * _rms(o)).astype(BF16)
        return 0

    lax.fori_loop(0, nc, fwd_body, 0)


def _retention(proj, lg_fwd, lg_bwd, *, batch, seq, d_model, chunk):
    dk = d_model // RET_HEADS
    dv = 2 * dk
    vw = RET_HEADS * dv
    p3 = proj.reshape(batch, seq, 2 * d_model + 2 * vw)
    out = pl.pallas_call(
        functools.partial(_retention_kernel, chunk=chunk),
        grid=(batch, RET_HEADS),
        in_specs=[
            pl.BlockSpec(memory_space=pltpu.SMEM),
            pl.BlockSpec(memory_space=pltpu.SMEM),
            pl.BlockSpec((None, seq, dk), lambda b, h: (b, 0, h)),
            pl.BlockSpec((None, seq, dk), lambda b, h: (b, 0, RET_HEADS + h)),
            pl.BlockSpec((None, seq, dv), lambda b, h: (b, 0, RET_HEADS + h)),
            pl.BlockSpec((None, seq, dv), lambda b, h: (b, 0, 2 * RET_HEADS + h)),
        ],
        out_specs=pl.BlockSpec((None, seq, dv), lambda b, h: (b, 0, h)),
        out_shape=jax.ShapeDtypeStruct((batch, seq, vw), BF16),
        scratch_shapes=[pltpu.VMEM((seq, dv), F32), pltpu.VMEM((dk, dv), F32)],
        compiler_params=_params(("parallel", "parallel")),
        name="retention",
    )(lg_fwd, lg_bwd, p3, p3, p3, p3)
    return out.reshape(batch * seq, vw)


def _out_router_kernel(o_ref, w_ref, x_ref, g_ref, rwh_ref, rwl_ref, rb_ref,
                       xn_ref, hm_ref, idx_ref, wts_ref, rank_ref, cnt_ref, carry_scr):
    i = pl.program_id(0)
    tm = x_ref.shape[0]

    @pl.when(i == 0)
    def _():
        carry_scr[...] = jnp.zeros_like(carry_scr)

    xn = x_ref[...] + jnp.dot(o_ref[...], w_ref[...], preferred_element_type=F32)
    xn_ref[...] = xn
    hm = _rms(xn) * g_ref[...]
    hm_ref[...] = hm

    hm_hi = hm.astype(BF16)
    hm_lo = (hm - hm_hi.astype(F32)).astype(BF16)
    nt = (((1,), (1,)), ((), ()))
    logits = (lax.dot_general(rwh_ref[...], hm_hi, nt, preferred_element_type=F32)
              + lax.dot_general(rwh_ref[...], hm_lo, nt, preferred_element_type=F32)
              + lax.dot_general(rwl_ref[...], hm_hi, nt, preferred_element_type=F32)
              + rb_ref[...])

    eidx = lax.broadcasted_iota(I32, (N_EXPERTS, tm), 0)
    work = logits
    vals, sels, hots = [], [], []
    for _ in range(TOP_K):
        m = jnp.max(work, axis=0, keepdims=True)
        sel = jnp.min(jnp.where(work == m, eidx, N_EXPERTS), axis=0, keepdims=True)
        hot = eidx == sel
        vals.append(m)
        sels.append(sel)
        hots.append(hot)
        work = jnp.where(hot, -jnp.inf, work)
    exps = [jnp.exp(v - vals[0]) for v in vals]
    denom = exps[0] + exps[1] + exps[2] + exps[3]

    mask = jnp.where(hots[0] | hots[1] | hots[2] | hots[3], 1.0, 0.0)
    tri = (lax.broadcasted_iota(I32, (tm, tm), 0) < lax.broadcasted_iota(I32, (tm, tm), 1))
    excl = jnp.dot(mask.astype(BF16), jnp.where(tri, 1.0, 0.0).astype(BF16), preferred_element_type=F32)
    rank_e = excl + carry_scr[:, 0:1]
    for k in range(TOP_K):
        idx_ref[k:k + 1, :] = sels[k]
        wts_ref[k:k + 1, :] = exps[k] / denom
        rank_ref[k:k + 1, :] = jnp.sum(jnp.where(hots[k], rank_e, 0.0), axis=0, keepdims=True).astype(I32)
    carry_scr[...] = carry_scr[...] + jnp.sum(mask, axis=1, keepdims=True)
    cnt_ref[...] = carry_scr[...].astype(I32)


def _out_router(o, w_out_bf16, x, moe_gain, rw_hi_t, rw_lo_t, router_b, *, tm):
    t, d = x.shape
    dv = o.shape[1]
    tok = lambda i: (i, 0)
    fixed = lambda i: (0, 0)
    lanes_tok = lambda i: (0, i)
    return pl.pallas_call(
        _out_router_kernel,
        grid=(t // tm,),
        in_specs=[
            pl.BlockSpec((tm, dv), tok),
            pl.BlockSpec((dv, d), fixed),
            pl.BlockSpec((tm, d), tok),
            pl.BlockSpec((1, d), fixed),
            pl.BlockSpec((N_EXPERTS, d), fixed),
            pl.BlockSpec((N_EXPERTS, d), fixed),
            pl.BlockSpec((N_EXPERTS, 1), fixed),
        ],
        out_specs=[
            pl.BlockSpec((tm, d), tok),
            pl.BlockSpec((tm, d), tok),
            pl.BlockSpec((TOP_K, tm), lanes_tok),
            pl.BlockSpec((TOP_K, tm), lanes_tok),
            pl.BlockSpec((TOP_K, tm), lanes_tok),
            pl.BlockSpec((N_EXPERTS, LANES), fixed),
        ],
        out_shape=[
            jax.ShapeDtypeStruct((t, d), F32),
            jax.ShapeDtypeStruct((t, d), F32),
            jax.ShapeDtypeStruct((TOP_K, t), I32),
            jax.ShapeDtypeStruct((TOP_K, t), F32),
            jax.ShapeDtypeStruct((TOP_K, t), I32),
            jax.ShapeDtypeStruct((N_EXPERTS, LANES), I32),
        ],
        scratch_shapes=[pltpu.VMEM((N_EXPERTS, LANES), F32)],
        compiler_params=_params(("arbitrary",)),
        name="out_router",
    )(o, w_out_bf16, x, moe_gain.reshape(1, d), rw_hi_t, rw_lo_t, router_b.reshape(N_EXPERTS, 1))


SUBLANES = 8


def _idx_slot_copy(dest_hbm, idx_smem, idx_sem, step, s):
    n_idx = dest_hbm.shape[1]
    return pltpu.make_async_copy(dest_hbm.at[step], idx_smem.at[pl.ds(pl.multiple_of(s * n_idx, n_idx), n_idx)],
                                 idx_sem.at[s])


def _dispatch_kernel(ends_ref, padded_ref, dest_hbm, hm_ref, xs_hbm, idx_smem, zero_scr, idx_sem, fill_sem, row_sem,
                     *, tm, tr, n_tiles):
    i = pl.program_id(0)
    n = pl.num_programs(0)
    slot = i % 2
    groups = tm // SUBLANES

    def idx_copy(step, s):
        return _idx_slot_copy(dest_hbm, idx_smem, idx_sem, step, s)

    @pl.when(i == 0)
    def _():
        idx_copy(0, 0).start()
        zero_scr[...] = jnp.zeros_like(zero_scr)

        def fill(start):
            return pltpu.make_async_copy(zero_scr, xs_hbm.at[pl.ds(pl.multiple_of(start, tr), tr), :], fill_sem)

        first_unused = ends_ref[N_EXPERTS - 1] // tr
        for e in range(N_EXPERTS):
            @pl.when(padded_ref[e] > 0)
            def _():
                fill(ends_ref[e] - tr).start()
        lax.fori_loop(first_unused, n_tiles, lambda c, _: (fill(c * tr).start(), 0)[1], 0)
        for e in range(N_EXPERTS):
            @pl.when(padded_ref[e] > 0)
            def _():
                fill(0).wait()
        lax.fori_loop(first_unused, n_tiles, lambda c, _: (fill(0).wait(), 0)[1], 0)

    @pl.when(i + 1 < n)
    def _():
        idx_copy(i + 1, 1 - slot).start()

    idx_copy(i, slot).wait()
    idx_base = slot * (TOP_K * tm)

    def start_group(g, _):
        for u in range(SUBLANES):
            src = hm_ref.at[g, pl.ds(u, 1), :]
            for k in range(TOP_K):
                row = idx_smem[idx_base + k * tm + g * SUBLANES + u]
                pltpu.make_async_copy(src, xs_hbm.at[pl.ds(row, 1), :], row_sem).start(priority=k % 2)
        return 0

    lax.fori_loop(0, groups, start_group, 0)

    for _ in range(TOP_K):
        pltpu.make_async_copy(xs_hbm.at[pl.ds(0, tm), :], xs_hbm.at[pl.ds(0, tm), :], row_sem).wait()


def _dispatch(hm, dest_tiles, ends, padded, *, tm, tr, n_rows):
    t, d = hm.shape
    return pl.pallas_call(
        functools.partial(_dispatch_kernel, tm=tm, tr=tr, n_tiles=n_rows // tr),
        grid_spec=pltpu.PrefetchScalarGridSpec(
            num_scalar_prefetch=2,
            grid=(t // tm,),
            in_specs=[
                pl.BlockSpec(memory_space=pl.ANY),
                pl.BlockSpec((tm // SUBLANES, SUBLANES, d), lambda i, *_: (i, 0, 0)),
            ],
            out_specs=pl.BlockSpec(memory_space=pl.ANY),
            scratch_shapes=[
                pltpu.SMEM((2 * TOP_K * tm,), I32),
                pltpu.VMEM((tr, d), F32),
                pltpu.SemaphoreType.DMA((2,)),
                pltpu.SemaphoreType.DMA(()),
                pltpu.SemaphoreType.DMA(()),
            ],
        ),
        out_shape=jax.ShapeDtypeStruct((n_rows, d), F32),
        compiler_params=_params(("arbitrary",)),
        name="moe_dispatch",
    )(ends, padded, dest_tiles, hm.reshape(t // SUBLANES, SUBLANES, d))


MXU_WIDTH = 256


def _split_w1_kernel(w_ref, perm_ref, g_ref, l_ref):
    half = MXU_WIDTH // 2
    for c in range(w_ref.shape[1] // MXU_WIDTH):
        y = jnp.dot(w_ref[:, c * MXU_WIDTH:(c + 1) * MXU_WIDTH].astype(BF16), perm_ref[...],
                    preferred_element_type=F32)
        g_ref[:, c * half:(c + 1) * half] = y[:, :half].astype(BF16)
        l_ref[:, c * half:(c + 1) * half] = y[:, half:].astype(BF16)


def _split_w1(w1_layers, layer):
    _, e, d, ff2 = w1_layers.shape
    ff = ff2 // 2
    rows = min(512, d)
    src = jnp.arange(MXU_WIDTH)
    dst = jnp.where(src % 2 == 0, src // 2, MXU_WIDTH // 2 + src // 2)
    perm = (dst[:, None] == jnp.arange(MXU_WIDTH)[None, :]).astype(BF16)
    blk = lambda i, r: (i, r, 0)
    return pl.pallas_call(
        _split_w1_kernel,
        grid=(e, d // rows),
        in_specs=[pl.BlockSpec((None, None, rows, ff2), lambda i, r: (layer, i, r, 0)),
                  pl.BlockSpec((MXU_WIDTH, MXU_WIDTH), lambda i, r: (0, 0))],
        out_specs=[pl.BlockSpec((None, rows, ff), blk), pl.BlockSpec((None, rows, ff), blk)],
        out_shape=[jax.ShapeDtypeStruct((e, d, ff), BF16), jax.ShapeDtypeStruct((e, d, ff), BF16)],
        compiler_params=_params(("parallel", "parallel")),
        name="split_w1",
    )(w1_layers, perm)


def _experts_kernel(te_ref, tv_ref, xs_ref, w1g_ref, w1l_ref, b1g_ref, b1l_ref, w2_ref, b2_ref, ys_ref):
    i = pl.program_id(0)

    @pl.when(tv_ref[i] > 0)
    def _():
        x = xs_ref[...].astype(BF16)
        glu = jnp.dot(x, w1g_ref[...], preferred_element_type=F32) + b1g_ref[...]
        lin = jnp.dot(x, w1l_ref[...], preferred_element_type=F32) + b1l_ref[...]
        glu = jnp.minimum(glu, SWIGLU_LIMIT)
        lin = jnp.clip(lin, -SWIGLU_LIMIT, SWIGLU_LIMIT)
        act = glu * jax.nn.sigmoid(SWIGLU_ALPHA * glu) * (lin + 1.0)
        ys_ref[...] = (jnp.dot(act.astype(BF16), w2_ref[...].astype(BF16), preferred_element_type=F32)
                       + b2_ref[...])

    @pl.when(tv_ref[i] == 0)
    def _():
        ys_ref[...] = jnp.zeros_like(ys_ref)


def _experts(xs, tile_expert, tile_valid, w1g, w1l, b1g, b1l, w2_layers, layer, b2, *, tr):
    n_rows, d = xs.shape
    ff = w1g.shape[2]
    n_tiles = n_rows // tr
    row_tile = lambda i, te, tv: (i, 0)
    by_expert = lambda i, te, tv: (te[i], 0, 0)
    return pl.pallas_call(
        _experts_kernel,
        grid_spec=pltpu.PrefetchScalarGridSpec(
            num_scalar_prefetch=2,
            grid=(n_tiles,),
            in_specs=[
                pl.BlockSpec((tr, d), row_tile),
                pl.BlockSpec((None, d, ff), by_expert),
                pl.BlockSpec((None, d, ff), by_expert),
                pl.BlockSpec((None, 1, ff), by_expert),
                pl.BlockSpec((None, 1, ff), by_expert),
                pl.BlockSpec((None, None, ff, d), lambda i, te, tv: (layer, te[i], 0, 0)),
                pl.BlockSpec((None, 1, d), by_expert),
            ],
            out_specs=pl.BlockSpec((tr, d), row_tile),
        ),
        out_shape=jax.ShapeDtypeStruct((n_rows, d), F32),
        compiler_params=_params(("arbitrary",)),
        name="moe_experts",
    )(tile_expert, tile_valid, xs, w1g, w1l, b1g, b1l, w2_layers, b2)


GATHER_GROUPS_PER_ITER = 4


def _combine_kernel(dest_hbm, ys_hbm, x_ref, wt_ref, o_ref, idx_smem, rows_scr, idx_sem, row_sem, *, tm):
    i = pl.program_id(0)
    n = pl.num_programs(0)
    slot = i % 2
    groups = TOP_K * tm // SUBLANES

    def idx_copy(step, s):
        return _idx_slot_copy(dest_hbm, idx_smem, idx_sem, step, s)

    def gather_rows(s):
        idx_base = s * (TOP_K * tm)

        def start_groups(gg, _):
            for part in range(GATHER_GROUPS_PER_ITER):
                g = gg * GATHER_GROUPS_PER_ITER + part
                for u in range(SUBLANES):
                    row = idx_smem[idx_base + g * SUBLANES + u]
                    pltpu.make_async_copy(ys_hbm.at[pl.ds(row, 1), :], rows_scr.at[s, g, pl.ds(u, 1), :],
                                          row_sem.at[s]).start(priority=u % 2)
            return 0

        lax.fori_loop(0, groups // GATHER_GROUPS_PER_ITER, start_groups, 0)

    @pl.when(i == 0)
    def _():
        idx_copy(0, 0).start()
        idx_copy(0, 0).wait()
        gather_rows(0)

        @pl.when(n > 1)
        def _():
            idx_copy(1, 1).start()

    @pl.when(i + 1 < n)
    def _():
        idx_copy(i + 1, 1 - slot).wait()
        gather_rows(1 - slot)

    @pl.when(i + 2 < n)
    def _():
        idx_copy(i + 2, slot).start()

    per_k = tm // SUBLANES
    for k in range(TOP_K):
        part = rows_scr.at[slot, pl.ds(k * per_k, per_k)]
        pltpu.make_async_copy(part, part, row_sem.at[slot]).wait()

    out = x_ref[...]
    for k in range(TOP_K):
        rows = rows_scr[slot, k * per_k:(k + 1) * per_k].reshape(tm, x_ref.shape[1])
        out = out + wt_ref[:, k:k + 1] * rows
    o_ref[...] = out


def _combine(ys, dest_tiles, x, wts_t, *, tm):
    t, d = x.shape
    return pl.pallas_call(
        functools.partial(_combine_kernel, tm=tm),
        grid=(t // tm,),
        in_specs=[
            pl.BlockSpec(memory_space=pl.ANY),
            pl.BlockSpec(memory_space=pl.ANY),
            pl.BlockSpec((tm, d), lambda i: (i, 0)),
            pl.BlockSpec((tm, TOP_K), lambda i: (i, 0)),
        ],
        out_specs=pl.BlockSpec((tm, d), lambda i: (i, 0)),
        out_shape=jax.ShapeDtypeStruct((t, d), F32),
        scratch_shapes=[
            pltpu.SMEM((2 * TOP_K * tm,), I32),
            pltpu.VMEM((2, TOP_K * tm // SUBLANES, SUBLANES, d), F32),
            pltpu.SemaphoreType.DMA((2,)),
            pltpu.SemaphoreType.DMA((2,)),
        ],
        compiler_params=_params(("arbitrary",)),
        name="moe_combine",
    )(dest_tiles, ys, x, wts_t)


def _tiles(batch, seq):
    t = batch * seq
    tm = min(512, t)
    return dict(
        tm=tm,
        tn=256,
        tq=min(512, seq),
        tk=min(512, seq),
        chunk=min(256, seq),
        tmd=min(256, t),
        tr=min(512, max(8, TOP_K * t // N_EXPERTS)),
    )


def _moe(o, w_out, x, moe_gain, router_w, router_b, w1_layers, b1, w2_layers, b2, layer, *, cfg):
    t, d = x.shape
    tm, tmd, tr = cfg["tm"], cfg["tmd"], cfg["tr"]
    rw_t = router_w.T
    rw_hi = rw_t.astype(BF16)
    rw_lo = (rw_t - rw_hi.astype(F32)).astype(BF16)
    x, hm, idx, wts, rank, cnt = _out_router(o, w_out.astype(BF16), x, moe_gain, rw_hi, rw_lo, router_b, tm=tm)

    counts = cnt[:, 0]
    padded = ((counts + tr - 1) // tr) * tr
    ends = jnp.cumsum(padded).astype(I32)
    offs = ends - padded
    experts = jnp.arange(N_EXPERTS, dtype=I32)[:, None, None]
    dest = rank + jnp.sum(jnp.where(idx[None] == experts, offs[:, None, None], 0), axis=0)
    dest_tiles = dest.reshape(TOP_K, t // tmd, tmd).transpose(1, 0, 2).reshape(t // tmd, TOP_K * tmd)
    n_rows = TOP_K * t + N_EXPERTS * tr
    tile_start = jnp.arange(n_rows // tr, dtype=I32) * tr
    tile_expert = jnp.minimum(jnp.sum((tile_start[:, None] >= ends[None, :]).astype(I32), axis=1), N_EXPERTS - 1)
    tile_valid = (tile_start < ends[-1]).astype(I32)

    xs = _dispatch(hm, dest_tiles, ends, padded.astype(I32), tm=tmd, tr=tr, n_rows=n_rows)
    ff = w2_layers.shape[2]
    w1g, w1l = _split_w1(w1_layers, layer)
    ys = _experts(xs, tile_expert, tile_valid, w1g, w1l,
                  b1[:, 0::2].reshape(N_EXPERTS, 1, ff), b1[:, 1::2].reshape(N_EXPERTS, 1, ff),
                  w2_layers, layer, b2.reshape(N_EXPERTS, 1, d), tr=tr)
    return _combine(ys, dest_tiles, x, wts.T, tm=tmd)


def kernel(x, diff_norm, diff_w_in, diff_w_out, diff_lambda_q1, diff_lambda_k1, diff_lambda_q2, diff_lambda_k2, diff_q_norm, diff_k_norm, diff_subln, ret_norm, ret_w_in, ret_w_out, ret_log_decay_fwd, ret_log_decay_bwd, moe_norm, moe_router_w, moe_router_b, moe_w1, moe_b1, moe_w2, moe_b2):
    batch, seq, d = x.shape
    depth = moe_norm.shape[0]
    cfg = _tiles(batch, seq)
    x = x.reshape(batch * seq, d)
    slopes = jnp.exp2(-8.0 * jnp.arange(1, DIFF_HEADS + 1, dtype=F32) / DIFF_HEADS)
    for i in range(depth):
        j = i // N_MIXERS
        if i % N_MIXERS == 0:
            lambda_init = _lambda_init(i)
            lam = (jnp.exp(jnp.sum(diff_lambda_q1[j] * diff_lambda_k1[j]))
                   - jnp.exp(jnp.sum(diff_lambda_q2[j] * diff_lambda_k2[j])) + lambda_init)
            reps = d // DIFF_HEAD_DIM
            qk_gain = jnp.concatenate([jnp.tile(diff_q_norm[j], reps) * (DIFF_HEAD_DIM ** -0.5 * LOG2E),
                                       jnp.tile(diff_k_norm[j], reps)]).reshape(1, 2 * d)
            qk, vt = _in_proj(x, diff_norm[j], diff_w_in[j].astype(BF16), tm=cfg["tm"], tn=cfg["tn"],
                              qk_gain_row=qk_gain, values_layout=(batch, seq))
            o = _diff_attention(qk, vt, slopes, lam, diff_subln[j], lambda_init,
                                batch=batch, seq=seq, tq=cfg["tq"], tk=cfg["tk"])
            w_out = diff_w_out[j]
        else:
            dk = d // RET_HEADS
            col_scale = jnp.ones((ret_w_in.shape[2],), F32).at[d:2 * d].set(dk ** -0.5)
            proj, = _in_proj(x, ret_norm[j], (ret_w_in[j] * col_scale).astype(BF16), tm=cfg["tm"], tn=cfg["tn"])
            o = _retention(proj, ret_log_decay_fwd[j], ret_log_decay_bwd[j],
                           batch=batch, seq=seq, d_model=d, chunk=cfg["chunk"])
            w_out = ret_w_out[j]
        x = _moe(o, w_out, x, moe_norm[i], moe_router_w[i], moe_router_b[i],
                 moe_w1, moe_b1[i], moe_w2, moe_b2[i], i, cfg=cfg)
    return x.reshape(batch, seq, d)
```

```python
import functools
import math

import jax
import jax.numpy as jnp
from jax import lax
from jax.experimental import pallas as pl
from jax.experimental.pallas import tpu as pltpu

F32 = jnp.float32
BF16 = jnp.bfloat16
I32 = jnp.int32

NORM_EPS = 1e-5
N_MIXERS = 2

DIFF_HEADS = 8
DIFF_HEAD_DIM = 64
DIFF_V_DIM = 2 * DIFF_HEAD_DIM

RET_HEADS = 4

N_EXPERTS = 32
TOP_K = 4
SWIGLU_ALPHA = 1.702
SWIGLU_LIMIT = 7.0

V7X_VMEM_LIMIT_BYTES = 56 * 1024 * 1024
LANES = 128


def _lambda_init(layer_idx):
    return 0.8 - 0.6 * math.exp(-0.3 * layer_idx)


def _params(semantics):
    return pltpu.CompilerParams(dimension_semantics=semantics, vmem_limit_bytes=V7X_VMEM_LIMIT_BYTES)


def _rms(x, eps=NORM_EPS):
    return x * lax.rsqrt(jnp.mean(x * x, axis=-1, keepdims=True) + eps)


def _in_proj_kernel(x_ref, g_ref, w_ref, gmat_ref, qkg_ref, o_ref, *rest, tn, n_normed, n_transposed):
    vt_ref, h_scr = rest if n_transposed else (None, rest[0])
    h_scr[...] = (_rms(x_ref[...]) * g_ref[...]).astype(BF16)
    n_total = w_ref.shape[1]
    n_chunks = n_total // tn
    n_rowmajor = n_total - n_transposed
    if n_transposed:
        vt_ref[:, DIFF_V_DIM:, :] = jnp.ones((vt_ref.shape[0], V_ONES_ROWS, vt_ref.shape[2]), BF16)

    def project(c):
        return jnp.dot(h_scr[...], w_ref[:, c * tn:(c + 1) * tn], preferred_element_type=F32)

    def finish(c, acc):
        cols = slice(c * tn, (c + 1) * tn)
        if c * tn < n_normed:
            sq = acc * acc
            hi = sq.astype(BF16)
            lo = (sq - hi.astype(F32)).astype(BF16)
            msq = (jnp.dot(hi, gmat_ref[...], preferred_element_type=F32)
                   + jnp.dot(lo, gmat_ref[...], preferred_element_type=F32))
            acc = acc * lax.rsqrt(msq + NORM_EPS) * qkg_ref[:, cols]
        if c * tn < n_rowmajor:
            o_ref[:, cols] = acc.astype(BF16)
        else:
            for part in range(tn // DIFF_V_DIM):
                head = (c * tn - n_rowmajor) // DIFF_V_DIM + part
                vt_ref[head, :DIFF_V_DIM, :] = acc[:, part * DIFF_V_DIM:(part + 1) * DIFF_V_DIM].T.astype(BF16)

    acc = project(0)
    for c in range(1, n_chunks):
        nxt = project(c)
        finish(c - 1, acc)
        acc = nxt
    finish(n_chunks - 1, acc)


def _in_proj(x, gain, w_bf16, *, tm, tn, qk_gain_row=None, values_layout=None):
    t, d = x.shape
    n = w_bf16.shape[1]
    if qk_gain_row is None:
        n_normed = 0
        qk_gain_row = jnp.zeros((1, n), F32)
    else:
        n_normed = qk_gain_row.shape[1]
        qk_gain_row = jnp.pad(qk_gain_row, ((0, 0), (0, n - n_normed)))
    grp = jnp.arange(tn) // DIFF_HEAD_DIM
    gmat = jnp.where(grp[:, None] == grp[None, :], 1.0 / DIFF_HEAD_DIM, 0.0).astype(BF16)
    n_transposed = DIFF_HEADS * DIFF_V_DIM if values_layout else 0
    n_rowmajor = n - n_transposed
    out_specs = [pl.BlockSpec((tm, n_rowmajor), lambda i: (i, 0))]
    out_shape = [jax.ShapeDtypeStruct((t, n_rowmajor), BF16)]
    if values_layout:
        batch, seq = values_layout
        per_seq = seq // tm
        rows = DIFF_V_DIM + V_ONES_ROWS
        out_specs.append(pl.BlockSpec((None, DIFF_HEADS, rows, tm), lambda i: (i // per_seq, 0, 0, i % per_seq)))
        out_shape.append(jax.ShapeDtypeStruct((batch, DIFF_HEADS, rows, seq), BF16))
    return pl.pallas_call(
        functools.partial(_in_proj_kernel, tn=tn, n_normed=n_normed, n_transposed=n_transposed),
        grid=(t // tm,),
        in_specs=[
            pl.BlockSpec((tm, d), lambda i: (i, 0)),
            pl.BlockSpec((1, d), lambda i: (0, 0)),
            pl.BlockSpec((d, n), lambda i: (0, 0)),
            pl.BlockSpec((tn, tn), lambda i: (0, 0)),
            pl.BlockSpec((1, n), lambda i: (0, 0)),
        ],
        out_specs=out_specs,
        out_shape=out_shape,
        scratch_shapes=[pltpu.VMEM((tm, d), BF16)],
        compiler_params=_params(("parallel",)),
        name="in_proj",
    )(x, gain.reshape(1, d), w_bf16, gmat, qk_gain_row)


LOG2E = math.log2(math.e)
LOG2E_BF16_PARTS = (1.4453125, -0.00262451171875, 7.063150405883789e-06)
ALIBI_LOW_BITS = 255
ALIBI_HIGH_BIT = 256
V_ONES_ROWS = 16


def _alibi_key_features(slopes, seq, tk):
    dj = jnp.arange(seq, dtype=I32) % tk
    lo = (dj & ALIBI_LOW_BITS).astype(F32)
    hi = (dj & ALIBI_HIGH_BIT).astype(F32)
    lane = jnp.arange(LANES)
    feat = jnp.where(lane[None, :] < 3, lo[:, None], jnp.where(lane[None, :] < 6, hi[:, None], 0.0))
    return (slopes[:, None, None] * feat[None]).astype(BF16)


def _diff_attn_kernel(slopes_ref, lam_ref, q_ref, k_ref, vt_ref, fk_ref, sg_ref, o_ref,
                      qa_scr, s0_scr, s1_scr, mt0_scr, mt1_scr, p0_scr, p1_scr, a0_scr, a1_scr,
                      m_scr, acc_scr, *, tq, tk, out_scale):
    h = pl.program_id(1)
    qi = pl.program_id(2)
    slope2 = slopes_ref[h] * LOG2E
    lam = lam_ref[0]
    nk = k_ref.shape[0] // tk
    i0 = qi * tq
    jd = i0 // tk
    cols2 = 2 * tq

    q = q_ref[...]
    lane = lax.broadcasted_iota(I32, q.shape, 1)
    zero = jnp.zeros_like(q)
    qq = jnp.concatenate([jnp.where(lane < DIFF_HEAD_DIM, q, zero),
                          jnp.where(lane >= DIFF_HEAD_DIM, q, zero)], axis=0)
    flane = lax.broadcasted_iota(I32, (cols2, LANES), 1)
    a, b, c = LOG2E_BF16_PARTS
    feat = jnp.where(flane % 3 == 0, a, jnp.where(flane % 3 == 1, b, c))
    feat = jnp.where(flane < 6, feat, 0.0)
    qa_scr[0] = jnp.concatenate([qq, feat.astype(BF16)], axis=1)
    qa_scr[1] = jnp.concatenate([qq, (-feat).astype(BF16)], axis=1)
    col = lax.broadcasted_iota(I32, (1, cols2), 1)
    q_pos = (i0 + jnp.where(col >= tq, col - tq, col)).astype(F32)

    m_scr[...] = jnp.full_like(m_scr, -jnp.inf)
    acc_scr[...] = jnp.zeros_like(acc_scr)
    nt = (((1,), (1,)), ((), ()))

    s_scr, mt_scr, p_scr, a_scr = (s0_scr, s1_scr), (mt0_scr, mt1_scr), (p0_scr, p1_scr), (a0_scr, a1_scr)

    def key_rows(j):
        return pl.ds(pl.multiple_of(j * tk, tk), tk)

    def exponentials(slot, col_shift):
        for cb in range(cols2 // LANES):
            cs = slice(cb * LANES, (cb + 1) * LANES)
            shift = col_shift[:, cs]
            m_old = m_scr[:, cs]
            m_new = jnp.maximum(m_old, mt_scr[slot][:, cs] - shift)
            alpha = jnp.exp2(m_old - m_new)
            p = jnp.exp2((s_scr[slot][:, cs] - (m_new + shift)).astype(BF16))
            m_scr[:, cs] = m_new
            a_scr[slot][:, cs] = alpha
            p_scr[slot][:, cs] = p

    def values(slot, j):
        acc_scr[...] = (a_scr[slot][...] * acc_scr[...]
                        + jnp.dot(vt_ref[:, key_rows(j)], p_scr[slot][...], preferred_element_type=F32))

    s = lax.dot_general(k_ref[key_rows(jd), :], qq, nt, preferred_element_type=F32)
    key_pos = (jd * tk + lax.broadcasted_iota(I32, (tk, 1), 0)).astype(F32)
    s = s - slope2 * jnp.abs(key_pos - q_pos)
    s_scr[1][...] = s
    mt_scr[1][...] = jnp.max(s, axis=0, keepdims=True)

    n_lin = nk - 1

    def tile_of(t):
        return t + (t >= jd).astype(I32)

    def scores(t, slot):
        j = tile_of(t)
        k_aug = jnp.concatenate([k_ref[key_rows(j), :], fk_ref[key_rows(j), :]], axis=1)
        s = lax.dot_general(k_aug, qa_scr[(j > jd).astype(I32)], nt, preferred_element_type=F32)
        s_scr[slot][...] = s
        mt_scr[slot][...] = jnp.max(s, axis=0, keepdims=True)

    def shift_of(t):
        j = tile_of(t)
        d = slope2 * (q_pos - (j * tk).astype(F32))
        return jnp.where(j > jd, -d, d)

    no_shift = jnp.zeros((1, cols2), F32)
    if n_lin == 0:
        exponentials(1, no_shift)
        values(1, jd)
    else:
        scores(0, 0)
        exponentials(1, no_shift)

        def stage(t, slot, with_scores=True):
            if with_scores:
                scores(t + 1, 1 - slot)
            values(1 - slot, jnp.where(t == 0, jd, tile_of(jnp.maximum(t - 1, 0))))
            exponentials(slot, shift_of(t))

        def pair(tt, _):
            stage(2 * tt, 0)
            stage(2 * tt + 1, 1)
            return 0

        lax.fori_loop(0, (n_lin - 1) // 2, pair, 0)
        if (n_lin - 1) % 2:
            stage(jnp.int32(n_lin - 2), (n_lin - 2) % 2)
        stage(jnp.int32(n_lin - 1), (n_lin - 1) % 2, with_scores=False)
        values((n_lin - 1) % 2, tile_of(jnp.int32(n_lin - 1)))

    o = acc_scr[:DIFF_V_DIM, :] / acc_scr[DIFF_V_DIM:DIFF_V_DIM + 1, :]
    o = o[:, :tq] - lam * o[:, tq:]
    o = o * lax.rsqrt(jnp.mean(o * o, axis=0, keepdims=True) + NORM_EPS)
    o_ref[...] = (o.T * sg_ref[...] * out_scale).astype(BF16)


def _diff_attention(qk, vt, slopes, lam, subln_gain, lambda_init, *, batch, seq, tq, tk):
    assert tk % tq == 0 and tk <= 2 * ALIBI_HIGH_BIT and tq % LANES == 0
    d = DIFF_HEADS * DIFF_V_DIM
    qkv3 = qk.reshape(batch, seq, 2 * d)
    fk = _alibi_key_features(slopes, seq, tk)
    out = pl.pallas_call(
        functools.partial(_diff_attn_kernel, tq=tq, tk=tk, out_scale=1.0 - lambda_init),
        grid=(batch, DIFF_HEADS, seq // tq),
        in_specs=[
            pl.BlockSpec(memory_space=pltpu.SMEM),
            pl.BlockSpec(memory_space=pltpu.SMEM),
            pl.BlockSpec((None, tq, DIFF_V_DIM), lambda b, h, i: (b, i, h)),
            pl.BlockSpec((None, seq, DIFF_V_DIM), lambda b, h, i: (b, 0, DIFF_HEADS + h)),
            pl.BlockSpec((None, None, DIFF_V_DIM + V_ONES_ROWS, seq), lambda b, h, i: (b, h, 0, 0)),
            pl.BlockSpec((None, seq, LANES), lambda b, h, i: (h, 0, 0)),
            pl.BlockSpec((1, DIFF_V_DIM), lambda b, h, i: (0, 0)),
        ],
        out_specs=pl.BlockSpec((None, tq, DIFF_V_DIM), lambda b, h, i: (b, i, h)),
        out_shape=jax.ShapeDtypeStruct((batch, seq, d), BF16),
        scratch_shapes=[
            pltpu.VMEM((2, 2 * tq, 2 * LANES), BF16),
            pltpu.VMEM((tk, 2 * tq), F32), pltpu.VMEM((tk, 2 * tq), F32),
            pltpu.VMEM((1, 2 * tq), F32), pltpu.VMEM((1, 2 * tq), F32),
            pltpu.VMEM((tk, 2 * tq), BF16), pltpu.VMEM((tk, 2 * tq), BF16),
            pltpu.VMEM((1, 2 * tq), F32), pltpu.VMEM((1, 2 * tq), F32),
            pltpu.VMEM((1, 2 * tq), F32),
            pltpu.VMEM((DIFF_V_DIM + V_ONES_ROWS, 2 * tq), F32),
        ],
        compiler_params=_params(("parallel", "parallel", "parallel")),
        name="diff_attn",
    )(slopes, lam.reshape(1), qkv3, qkv3, vt, fk, subln_gain.reshape(1, DIFF_V_DIM))
    return out.reshape(batch * seq, d)


def _retention_kernel(lgf_ref, lgb_ref, q_ref, k_ref, v_ref, g_ref, o_ref, ob_scr, st_scr, *, chunk):
    h = pl.program_id(1)
    lgf = lgf_ref[h]
    lgb = lgb_ref[h]
    seq = q_ref.shape[0]
    nc = seq // chunk
    c_f = float(chunk)

    row = lax.broadcasted_iota(I32, (chunk, chunk), 0)
    col = lax.broadcasted_iota(I32, (chunk, chunk), 1)
    rel = (row - col).astype(F32)
    d_both = (jnp.where(rel >= 0, jnp.exp(lgf * jnp.maximum(rel, 0.0)), 0.0)
              + jnp.where(rel < 0, jnp.exp(lgb * jnp.maximum(-rel, 0.0)), 0.0))
    idx = lax.broadcasted_iota(I32, (chunk, 1), 0).astype(F32)
    xi_f = jnp.exp(lgf * (idx + 1.0))
    zeta_f = jnp.exp(lgf * (c_f - 1.0 - idx))
    xi_b = jnp.exp(lgb * (c_f - idx))
    zeta_b = jnp.exp(lgb * idx)
    decay_f = jnp.exp(lgf * c_f)
    decay_b = jnp.exp(lgb * c_f)

    def state_update(k, v, zeta, decay):
        kz_t = (k.astype(F32) * zeta).T.astype(BF16)
        upd = jnp.dot(kz_t, v, preferred_element_type=F32)
        st_scr[...] = st_scr[...] * decay + upd

    st_scr[...] = jnp.zeros_like(st_scr)

    def bwd_body(i, _):
        c = nc - 1 - i
        rows = pl.ds(pl.multiple_of(c * chunk, chunk), chunk)
        q = q_ref[rows, :]
        inter = jnp.dot(q, st_scr[...].astype(BF16), preferred_element_type=F32)
        ob_scr[rows, :] = inter * xi_b
        state_update(k_ref[rows, :], v_ref[rows, :], zeta_b, decay_b)
        return 0

    lax.fori_loop(0, nc, bwd_body, 0, unroll=2)

    st_scr[...] = jnp.zeros_like(st_scr)

    def fwd_body(c, _):
        rows = pl.ds(pl.multiple_of(c * chunk, chunk), chunk)
        q = q_ref[rows, :]
        k = k_ref[rows, :]
        v = v_ref[rows, :]
        scores = lax.dot_general(q, k, (((1,), (1,)), ((), ())), preferred_element_type=F32) * d_both
        o = jnp.dot(scores.astype(BF16), v, preferred_element_type=F32)
        o = o + jnp.dot(q, st_scr[...].astype(BF16), preferred_element_type=F32) * xi_f
        o = o + ob_scr[rows, :]
        state_update(k, v, zeta_f, decay_f)
        g = g_ref[rows, :].astype(F32)
        o_ref[rows, :] = (g * jax.nn.sigmoid(g) * _rms(o)).astype(BF16)
        return 0

    lax.fori_loop(0, nc, fwd_body, 0, unroll=2)


def _retention(proj, lg_fwd, lg_bwd, *, batch, seq, d_model, chunk):
    dk = d_model // RET_HEADS
    dv = 2 * dk
    vw = RET_HEADS * dv
    p3 = proj.reshape(batch, seq, 2 * d_model + 2 * vw)
    out = pl.pallas_call(
        functools.partial(_retention_kernel, chunk=chunk),
        grid=(batch, RET_HEADS),
        in_specs=[
            pl.BlockSpec(memory_space=pltpu.SMEM),
            pl.BlockSpec(memory_space=pltpu.SMEM),
            pl.BlockSpec((None, seq, dk), lambda b, h: (b, 0, h)),
            pl.BlockSpec((None, seq, dk), lambda b, h: (b, 0, RET_HEADS + h)),
            pl.BlockSpec((None, seq, dv), lambda b, h: (b, 0, RET_HEADS + h)),
            pl.BlockSpec((None, seq, dv), lambda b, h: (b, 0, 2 * RET_HEADS + h)),
        ],
        out_specs=pl.BlockSpec((None, seq, dv), lambda b, h: (b, 0, h)),
        out_shape=jax.ShapeDtypeStruct((batch, seq, vw), BF16),
        scratch_shapes=[pltpu.VMEM((seq, dv), F32), pltpu.VMEM((dk, dv), F32)],
        compiler_params=_params(("parallel", "parallel")),
        name="retention",
    )(lg_fwd, lg_bwd, p3, p3, p3, p3)
    return out.reshape(batch * seq, vw)


def _out_router_kernel(o_ref, w_ref, x_ref, g_ref, rwh_ref, rwl_ref, rb_ref,
                       xn_ref, hm_ref, idx_ref, wts_ref, rank_ref, cnt_ref, carry_scr):
    i = pl.program_id(0)
    tm = x_ref.shape[0]

    @pl.when(i == 0)
    def _():
        carry_scr[...] = jnp.zeros_like(carry_scr)

    xn = x_ref[...] + jnp.dot(o_ref[...], w_ref[...], preferred_element_type=F32)
    xn_ref[...] = xn
    hm = _rms(xn) * g_ref[...]
    hm_ref[...] = hm

    hm_hi = hm.astype(BF16)
    hm_lo = (hm - hm_hi.astype(F32)).astype(BF16)
    nt = (((1,), (1,)), ((), ()))
    logits = (lax.dot_general(rwh_ref[...], hm_hi, nt, preferred_element_type=F32)
              + lax.dot_general(rwh_ref[...], hm_lo, nt, preferred_element_type=F32)
              + lax.dot_general(rwl_ref[...], hm_hi, nt, preferred_element_type=F32)
              + rb_ref[...])

    eidx = lax.broadcasted_iota(I32, (N_EXPERTS, tm), 0)
    work = logits
    vals, sels, hots = [], [], []
    for _ in range(TOP_K):
        m = jnp.max(work, axis=0, keepdims=True)
        sel = jnp.min(jnp.where(work == m, eidx, N_EXPERTS), axis=0, keepdims=True)
        hot = eidx == sel
        vals.append(m)
        sels.append(sel)
        hots.append(hot)
        work = jnp.where(hot, -jnp.inf, work)
    exps = [jnp.exp(v - vals[0]) for v in vals]
    denom = exps[0] + exps[1] + exps[2] + exps[3]

    mask = jnp.where(hots[0] | hots[1] | hots[2] | hots[3], 1.0, 0.0)
    tri = (lax.broadcasted_iota(I32, (tm, tm), 0) < lax.broadcasted_iota(I32, (tm, tm), 1))
    excl = jnp.dot(mask.astype(BF16), jnp.where(tri, 1.0, 0.0).astype(BF16), preferred_element_type=F32)
    rank_e = excl + carry_scr[:, 0:1]
    for k in range(TOP_K):
        idx_ref[k:k + 1, :] = sels[k]
        wts_ref[k:k + 1, :] = exps[k] / denom
        rank_ref[k:k + 1, :] = jnp.sum(jnp.where(hots[k], rank_e, 0.0), axis=0, keepdims=True).astype(I32)
    carry_scr[...] = carry_scr[...] + jnp.sum(mask, axis=1, keepdims=True)
    cnt_ref[...] = carry_scr[...].astype(I32)


def _out_router(o, w_out_bf16, x, moe_gain, rw_hi_t, rw_lo_t, router_b, *, tm):
    t, d = x.shape
    dv = o.shape[1]
    tok = lambda i: (i, 0)
    fixed = lambda i: (0, 0)
    lanes_tok = lambda i: (0, i)
    return pl.pallas_call(
        _out_router_kernel,
        grid=(t // tm,),
        in_specs=[
            pl.BlockSpec((tm, dv), tok),
            pl.BlockSpec((dv, d), fixed),
            pl.BlockSpec((tm, d), tok),
            pl.BlockSpec((1, d), fixed),
            pl.BlockSpec((N_EXPERTS, d), fixed),
            pl.BlockSpec((N_EXPERTS, d), fixed),
            pl.BlockSpec((N_EXPERTS, 1), fixed),
        ],
        out_specs=[
            pl.BlockSpec((tm, d), tok),
            pl.BlockSpec((tm, d), tok),
            pl.BlockSpec((TOP_K, tm), lanes_tok),
            pl.BlockSpec((TOP_K, tm), lanes_tok),
            pl.BlockSpec((TOP_K, tm), lanes_tok),
            pl.BlockSpec((N_EXPERTS, LANES), fixed),
        ],
        out_shape=[
            jax.ShapeDtypeStruct((t, d), F32),
            jax.ShapeDtypeStruct((t, d), F32),
            jax.ShapeDtypeStruct((TOP_K, t), I32),
            jax.ShapeDtypeStruct((TOP_K, t), F32),
            jax.ShapeDtypeStruct((TOP_K, t), I32),
            jax.ShapeDtypeStruct((N_EXPERTS, LANES), I32),
        ],
        scratch_shapes=[pltpu.VMEM((N_EXPERTS, LANES), F32)],
        compiler_params=_params(("arbitrary",)),
        name="out_router",
    )(o, w_out_bf16, x, moe_gain.reshape(1, d), rw_hi_t, rw_lo_t, router_b.reshape(N_EXPERTS, 1))


SUBLANES = 8


def _idx_slot_copy(dest_hbm, idx_smem, idx_sem, step, s):
    n_idx = dest_hbm.shape[1]
    return pltpu.make_async_copy(dest_hbm.at[step], idx_smem.at[pl.ds(pl.multiple_of(s * n_idx, n_idx), n_idx)],
                                 idx_sem.at[s])


def _dispatch_kernel(ends_ref, padded_ref, dest_hbm, hm_ref, xs_hbm, idx_smem, zero_scr, idx_sem, fill_sem, row_sem,
                     *, tm, tr, n_tiles):
    i = pl.program_id(0)
    n = pl.num_programs(0)
    slot = i % 2
    groups = tm // SUBLANES

    def idx_copy(step, s):
        return _idx_slot_copy(dest_hbm, idx_smem, idx_sem, step, s)

    @pl.when(i == 0)
    def _():
        idx_copy(0, 0).start()
        zero_scr[...] = jnp.zeros_like(zero_scr)

        def fill(start):
            return pltpu.make_async_copy(zero_scr, xs_hbm.at[pl.ds(pl.multiple_of(start, tr), tr), :], fill_sem)

        first_unused = ends_ref[N_EXPERTS - 1] // tr
        for e in range(N_EXPERTS):
            @pl.when(padded_ref[e] > 0)
            def _():
                fill(ends_ref[e] - tr).start()
        lax.fori_loop(first_unused, n_tiles, lambda c, _: (fill(c * tr).start(), 0)[1], 0)
        for e in range(N_EXPERTS):
            @pl.when(padded_ref[e] > 0)
            def _():
                fill(0).wait()
        lax.fori_loop(first_unused, n_tiles, lambda c, _: (fill(0).wait(), 0)[1], 0)

    @pl.when(i + 1 < n)
    def _():
        idx_copy(i + 1, 1 - slot).start()

    idx_copy(i, slot).wait()

    def scatter_rows(s):
        for g in range(groups):
            for u in range(SUBLANES):
                src = hm_ref.at[g, pl.ds(u, 1), :]
                for k in range(TOP_K):
                    row = idx_smem[s * TOP_K * tm + k * tm + g * SUBLANES + u]
                    pltpu.make_async_copy(src, xs_hbm.at[pl.ds(row, 1), :], row_sem).start(priority=k % 2)

    for s in range(2):
        @pl.when(slot == s)
        def _():
            scatter_rows(s)

    for _ in range(TOP_K):
        pltpu.make_async_copy(xs_hbm.at[pl.ds(0, tm), :], xs_hbm.at[pl.ds(0, tm), :], row_sem).wait()


def _dispatch(hm, dest_tiles, ends, padded, *, tm, tr, n_rows):
    t, d = hm.shape
    return pl.pallas_call(
        functools.partial(_dispatch_kernel, tm=tm, tr=tr, n_tiles=n_rows // tr),
        grid_spec=pltpu.PrefetchScalarGridSpec(
            num_scalar_prefetch=2,
            grid=(t // tm,),
            in_specs=[
                pl.BlockSpec(memory_space=pl.ANY),
                pl.BlockSpec((tm // SUBLANES, SUBLANES, d), lambda i, *_: (i, 0, 0)),
            ],
            out_specs=pl.BlockSpec(memory_space=pl.ANY),
            scratch_shapes=[
                pltpu.SMEM((2 * TOP_K * tm,), I32),
                pltpu.VMEM((tr, d), F32),
                pltpu.SemaphoreType.DMA((2,)),
                pltpu.SemaphoreType.DMA(()),
                pltpu.SemaphoreType.DMA(()),
            ],
        ),
        out_shape=jax.ShapeDtypeStruct((n_rows, d), F32),
        compiler_params=_params(("arbitrary",)),
        name="moe_dispatch",
    )(ends, padded, dest_tiles, hm.reshape(t // SUBLANES, SUBLANES, d))


MXU_WIDTH = 256


def _split_w1_kernel(w_ref, perm_ref, g_ref, l_ref):
    half = MXU_WIDTH // 2
    for c in range(w_ref.shape[1] // MXU_WIDTH):
        y = jnp.dot(w_ref[:, c * MXU_WIDTH:(c + 1) * MXU_WIDTH].astype(BF16), perm_ref[...],
                    preferred_element_type=F32)
        g_ref[:, c * half:(c + 1) * half] = y[:, :half].astype(BF16)
        l_ref[:, c * half:(c + 1) * half] = y[:, half:].astype(BF16)


def _split_w1(w1_layers, layer):
    _, e, d, ff2 = w1_layers.shape
    ff = ff2 // 2
    rows = min(512, d)
    src = jnp.arange(MXU_WIDTH)
    dst = jnp.where(src % 2 == 0, src // 2, MXU_WIDTH // 2 + src // 2)
    perm = (dst[:, None] == jnp.arange(MXU_WIDTH)[None, :]).astype(BF16)
    blk = lambda i, r: (i, r, 0)
    return pl.pallas_call(
        _split_w1_kernel,
        grid=(e, d // rows),
        in_specs=[pl.BlockSpec((None, None, rows, ff2), lambda i, r: (layer, i, r, 0)),
                  pl.BlockSpec((MXU_WIDTH, MXU_WIDTH), lambda i, r: (0, 0))],
        out_specs=[pl.BlockSpec((None, rows, ff), blk), pl.BlockSpec((None, rows, ff), blk)],
        out_shape=[jax.ShapeDtypeStruct((e, d, ff), BF16), jax.ShapeDtypeStruct((e, d, ff), BF16)],
        compiler_params=_params(("parallel", "parallel")),
        name="split_w1",
    )(w1_layers, perm)


def _experts_kernel(te_ref, tv_ref, xs_ref, w1g_ref, w1l_ref, b1g_ref, b1l_ref, w2_ref, b2_ref, ys_ref):
    i = pl.program_id(0)

    @pl.when(tv_ref[i] > 0)
    def _():
        x = xs_ref[...].astype(BF16)
        glu = jnp.dot(x, w1g_ref[...], preferred_element_type=F32) + b1g_ref[...]
        lin = jnp.dot(x, w1l_ref[...], preferred_element_type=F32) + b1l_ref[...]
        glu = jnp.minimum(glu, SWIGLU_LIMIT)
        lin = jnp.clip(lin, -SWIGLU_LIMIT, SWIGLU_LIMIT)
        act = glu * jax.nn.sigmoid(SWIGLU_ALPHA * glu) * (lin + 1.0)
        ys_ref[...] = (jnp.dot(act.astype(BF16), w2_ref[...].astype(BF16), preferred_element_type=F32)
                       + b2_ref[...])

    @pl.when(tv_ref[i] == 0)
    def _():
        ys_ref[...] = jnp.zeros_like(ys_ref)


def _experts(xs, tile_expert, tile_valid, w1g, w1l, b1g, b1l, w2_layers, layer, b2, *, tr):
    n_rows, d = xs.shape
    ff = w1g.shape[2]
    n_tiles = n_rows // tr
    row_tile = lambda i, te, tv: (i, 0)
    by_expert = lambda i, te, tv: (te[i], 0, 0)
    return pl.pallas_call(
        _experts_kernel,
        grid_spec=pltpu.PrefetchScalarGridSpec(
            num_scalar_prefetch=2,
            grid=(n_tiles,),
            in_specs=[
                pl.BlockSpec((tr, d), row_tile),
                pl.BlockSpec((None, d, ff), by_expert),
                pl.BlockSpec((None, d, ff), by_expert),
                pl.BlockSpec((None, 1, ff), by_expert),
                pl.BlockSpec((None, 1, ff), by_expert),
                pl.BlockSpec((None, None, ff, d), lambda i, te, tv: (layer, te[i], 0, 0)),
                pl.BlockSpec((None, 1, d), by_expert),
            ],
            out_specs=pl.BlockSpec((tr, d), row_tile),
        ),
        out_shape=jax.ShapeDtypeStruct((n_rows, d), F32),
        compiler_params=_params(("arbitrary",)),
        name="moe_experts",
    )(tile_expert, tile_valid, xs, w1g, w1l, b1g, b1l, w2_layers, b2)


def _combine_kernel(dest_hbm, ys_hbm, x_ref, wt_ref, o_ref, idx_smem, rows_scr, idx_sem, row_sem, *, tm):
    i = pl.program_id(0)
    n = pl.num_programs(0) - 1
    groups = TOP_K * tm // SUBLANES
    per_k = tm // SUBLANES

    def idx_copy(step, s):
        return _idx_slot_copy(dest_hbm, idx_smem, idx_sem, step, s)

    def gather_rows(s):
        for g in range(groups):
            for u in range(SUBLANES):
                row = idx_smem[s * TOP_K * tm + g * SUBLANES + u]
                pltpu.make_async_copy(ys_hbm.at[pl.ds(row, 1), :], rows_scr.at[s, g, pl.ds(u, 1), :],
                                      row_sem.at[s]).start(priority=u % 2)

    def combine_rows(s):
        for k in range(TOP_K):
            part = rows_scr.at[s, pl.ds(k * per_k, per_k)]
            pltpu.make_async_copy(part, part, row_sem.at[s]).wait()
        out = x_ref[...]
        for k in range(TOP_K):
            rows = rows_scr[s, k * per_k:(k + 1) * per_k].reshape(tm, x_ref.shape[1])
            out = out + wt_ref[:, k:k + 1] * rows
        o_ref[...] = out

    @pl.when(i == 0)
    def _():
        idx_copy(0, 0).start()

    for s in range(2):
        @pl.when((i < n) & (i % 2 == s))
        def _():
            idx_copy(i, s).wait()

            @pl.when(i + 1 < n)
            def _():
                idx_copy(i + 1, 1 - s).start()

            gather_rows(s)

    for s in range(2):
        @pl.when((i >= 1) & ((i - 1) % 2 == s))
        def _():
            combine_rows(s)


def _combine(ys, dest_tiles, x, wts_t, *, tm):
    t, d = x.shape
    prev = lambda i: (jnp.maximum(i - 1, 0), 0)
    return pl.pallas_call(
        functools.partial(_combine_kernel, tm=tm),
        grid=(t // tm + 1,),
        in_specs=[
            pl.BlockSpec(memory_space=pl.ANY),
            pl.BlockSpec(memory_space=pl.ANY),
            pl.BlockSpec((tm, d), prev),
            pl.BlockSpec((tm, TOP_K), prev),
        ],
        out_specs=pl.BlockSpec((tm, d), prev),
        out_shape=jax.ShapeDtypeStruct((t, d), F32),
        scratch_shapes=[
            pltpu.SMEM((2 * TOP_K * tm,), I32),
            pltpu.VMEM((2, TOP_K * tm // SUBLANES, SUBLANES, d), F32),
            pltpu.SemaphoreType.DMA((2,)),
            pltpu.SemaphoreType.DMA((2,)),
        ],
        compiler_params=_params(("arbitrary",)),
        name="moe_combine",
    )(dest_tiles, ys, x, wts_t)


def _tiles(batch, seq):
    t = batch * seq
    tm = min(512, t)
    return dict(
        tm=tm,
        tn=256,
        tq=min(512, seq),
        tk=min(512, seq),
        chunk=min(256, seq),
        tmd=min(256, t),
        tr=min(512, max(8, TOP_K * t // N_EXPERTS)),
    )


def _moe(o, w_out, x, moe_gain, router_w, router_b, w1_layers, b1, w2_layers, b2, layer, *, cfg):
    t, d = x.shape
    tm, tmd, tr = cfg["tm"], cfg["tmd"], cfg["tr"]
    rw_t = router_w.T
    rw_hi = rw_t.astype(BF16)
    rw_lo = (rw_t - rw_hi.astype(F32)).astype(BF16)
    x, hm, idx, wts, rank, cnt = _out_router(o, w_out.astype(BF16), x, moe_gain, rw_hi, rw_lo, router_b, tm=tm)

    counts = cnt[:, 0]
    padded = ((counts + tr - 1) // tr) * tr
    ends = jnp.cumsum(padded).astype(I32)
    offs = ends - padded
    experts = jnp.arange(N_EXPERTS, dtype=I32)[:, None, None]
    dest = rank + jnp.sum(jnp.where(idx[None] == experts, offs[:, None, None], 0), axis=0)
    dest_tiles = dest.reshape(TOP_K, t // tmd, tmd).transpose(1, 0, 2).reshape(t // tmd, TOP_K * tmd)
    n_rows = TOP_K * t + N_EXPERTS * tr
    tile_start = jnp.arange(n_rows // tr, dtype=I32) * tr
    tile_expert = jnp.minimum(jnp.sum((tile_start[:, None] >= ends[None, :]).astype(I32), axis=1), N_EXPERTS - 1)
    tile_valid = (tile_start < ends[-1]).astype(I32)

    xs = _dispatch(hm, dest_tiles, ends, padded.astype(I32), tm=tmd, tr=tr, n_rows=n_rows)
    ff = w2_layers.shape[2]
    w1g, w1l = _split_w1(w1_layers, layer)
    ys = _experts(xs, tile_expert, tile_valid, w1g, w1l,
                  b1[:, 0::2].reshape(N_EXPERTS, 1, ff), b1[:, 1::2].reshape(N_EXPERTS, 1, ff),
                  w2_layers, layer, b2.reshape(N_EXPERTS, 1, d), tr=tr)
    return _combine(ys, dest_tiles, x, wts.T, tm=tmd)


def kernel(x, diff_norm, diff_w_in, diff_w_out, diff_lambda_q1, diff_lambda_k1, diff_lambda_q2, diff_lambda_k2, diff_q_norm, diff_k_norm, diff_subln, ret_norm, ret_w_in, ret_w_out, ret_log_decay_fwd, ret_log_decay_bwd, moe_norm, moe_router_w, moe_router_b, moe_w1, moe_b1, moe_w2, moe_b2):
    batch, seq, d = x.shape
    depth = moe_norm.shape[0]
    cfg = _tiles(batch, seq)
    x = x.reshape(batch * seq, d)
    slopes = jnp.exp2(-8.0 * jnp.arange(1, DIFF_HEADS + 1, dtype=F32) / DIFF_HEADS)
    for i in range(depth):
        j = i // N_MIXERS
        if i % N_MIXERS == 0:
            lambda_init = _lambda_init(i)
            lam = (jnp.exp(jnp.sum(diff_lambda_q1[j] * diff_lambda_k1[j]))
                   - jnp.exp(jnp.sum(diff_lambda_q2[j] * diff_lambda_k2[j])) + lambda_init)
            reps = d // DIFF_HEAD_DIM
            qk_gain = jnp.concatenate([jnp.tile(diff_q_norm[j], reps) * (DIFF_HEAD_DIM ** -0.5 * LOG2E),
                                       jnp.tile(diff_k_norm[j], reps)]).reshape(1, 2 * d)
            qk, vt = _in_proj(x, diff_norm[j], diff_w_in[j].astype(BF16), tm=cfg["tm"], tn=cfg["tn"],
                              qk_gain_row=qk_gain, values_layout=(batch, seq))
            o = _diff_attention(qk, vt, slopes, lam, diff_subln[j], lambda_init,
                                batch=batch, seq=seq, tq=cfg["tq"], tk=cfg["tk"])
            w_out = diff_w_out[j]
        else:
            dk = d // RET_HEADS
            col_scale = jnp.ones((ret_w_in.shape[2],), F32).at[d:2 * d].set(dk ** -0.5)
            proj, = _in_proj(x, ret_norm[j], (ret_w_in[j] * col_scale).astype(BF16), tm=cfg["tm"], tn=cfg["tn"])
            o = _retention(proj, ret_log_decay_fwd[j], ret_log_decay_bwd[j],
                           batch=batch, seq=seq, d_model=d, chunk=cfg["chunk"])
            w_out = ret_w_out[j]
        x = _moe(o, w_out, x, moe_norm[i], moe_router_w[i], moe_router_b[i],
                 moe_w1, moe_b1[i], moe_w2, moe_b2[i], i, cfg=cfg)
    return x.reshape(batch, seq, d)
```

```python
import functools
import math

import jax
import jax.numpy as jnp
from jax import lax
from jax.experimental import pallas as pl
from jax.experimental.pallas import tpu as pltpu

F32 = jnp.float32
BF16 = jnp.bfloat16
I32 = jnp.int32

NORM_EPS = 1e-5
N_MIXERS = 2

DIFF_HEADS = 8
DIFF_HEAD_DIM = 64
DIFF_V_DIM = 2 * DIFF_HEAD_DIM

RET_HEADS = 4

N_EXPERTS = 32
TOP_K = 4
SWIGLU_ALPHA = 1.702
SWIGLU_LIMIT = 7.0

V7X_VMEM_LIMIT_BYTES = 56 * 1024 * 1024
LANES = 128


def _lambda_init(layer_idx):
    return 0.8 - 0.6 * math.exp(-0.3 * layer_idx)


def _params(semantics):
    return pltpu.CompilerParams(dimension_semantics=semantics, vmem_limit_bytes=V7X_VMEM_LIMIT_BYTES)


def _rms(x, eps=NORM_EPS):
    return x * lax.rsqrt(jnp.mean(x * x, axis=-1, keepdims=True) + eps)


def _in_proj_kernel(x_ref, g_ref, w_ref, gmat_ref, qkg_ref, o_ref, *rest, tn, n_normed, n_transposed):
    vt_ref, h_scr = rest if n_transposed else (None, rest[0])
    h_scr[...] = (_rms(x_ref[...]) * g_ref[...]).astype(BF16)
    n_total = w_ref.shape[1]
    n_chunks = n_total // tn
    n_rowmajor = n_total - n_transposed
    if n_transposed:
        vt_ref[:, DIFF_V_DIM:, :] = jnp.ones((vt_ref.shape[0], V_ONES_ROWS, vt_ref.shape[2]), BF16)

    def project(c):
        return jnp.dot(h_scr[...], w_ref[:, c * tn:(c + 1) * tn], preferred_element_type=F32)

    def finish(c, acc):
        cols = slice(c * tn, (c + 1) * tn)
        if c * tn < n_normed:
            sq = acc * acc
            hi = sq.astype(BF16)
            lo = (sq - hi.astype(F32)).astype(BF16)
            msq = (jnp.dot(hi, gmat_ref[...], preferred_element_type=F32)
                   + jnp.dot(lo, gmat_ref[...], preferred_element_type=F32))
            acc = acc * lax.rsqrt(msq + NORM_EPS) * qkg_ref[:, cols]
        if c * tn < n_rowmajor:
            o_ref[:, cols] = acc.astype(BF16)
        else:
            for part in range(tn // DIFF_V_DIM):
                head = (c * tn - n_rowmajor) // DIFF_V_DIM + part
                vt_ref[head, :DIFF_V_DIM, :] = acc[:, part * DIFF_V_DIM:(part + 1) * DIFF_V_DIM].T.astype(BF16)

    acc = project(0)
    for c in range(1, n_chunks):
        nxt = project(c)
        finish(c - 1, acc)
        acc = nxt
    finish(n_chunks - 1, acc)


def _in_proj(x, gain, w_bf16, *, tm, tn, qk_gain_row=None, values_layout=None):
    t, d = x.shape
    n = w_bf16.shape[1]
    if qk_gain_row is None:
        n_normed = 0
        qk_gain_row = jnp.zeros((1, n), F32)
    else:
        n_normed = qk_gain_row.shape[1]
        qk_gain_row = jnp.pad(qk_gain_row, ((0, 0), (0, n - n_normed)))
    grp = jnp.arange(tn) // DIFF_HEAD_DIM
    gmat = jnp.where(grp[:, None] == grp[None, :], 1.0 / DIFF_HEAD_DIM, 0.0).astype(BF16)
    n_transposed = DIFF_HEADS * DIFF_V_DIM if values_layout else 0
    n_rowmajor = n - n_transposed
    out_specs = [pl.BlockSpec((tm, n_rowmajor), lambda i: (i, 0))]
    out_shape = [jax.ShapeDtypeStruct((t, n_rowmajor), BF16)]
    if values_layout:
        batch, seq = values_layout
        per_seq = seq // tm
        rows = DIFF_V_DIM + V_ONES_ROWS
        out_specs.append(pl.BlockSpec((None, DIFF_HEADS, rows, tm), lambda i: (i // per_seq, 0, 0, i % per_seq)))
        out_shape.append(jax.ShapeDtypeStruct((batch, DIFF_HEADS, rows, seq), BF16))
    return pl.pallas_call(
        functools.partial(_in_proj_kernel, tn=tn, n_normed=n_normed, n_transposed=n_transposed),
        grid=(t // tm,),
        in_specs=[
            pl.BlockSpec((tm, d), lambda i: (i, 0)),
            pl.BlockSpec((1, d), lambda i: (0, 0)),
            pl.BlockSpec((d, n), lambda i: (0, 0)),
            pl.BlockSpec((tn, tn), lambda i: (0, 0)),
            pl.BlockSpec((1, n), lambda i: (0, 0)),
        ],
        out_specs=out_specs,
        out_shape=out_shape,
        scratch_shapes=[pltpu.VMEM((tm, d), BF16)],
        compiler_params=_params(("parallel",)),
        name="in_proj",
    )(x, gain.reshape(1, d), w_bf16, gmat, qk_gain_row)


LOG2E = math.log2(math.e)
LOG2E_BF16_PARTS = (1.4453125, -0.00262451171875, 7.063150405883789e-06)
ALIBI_LOW_BITS = 255
ALIBI_HIGH_BIT = 256
V_ONES_ROWS = 16


def _alibi_key_features(slopes, seq, tk):
    dj = jnp.arange(seq, dtype=I32) % tk
    lo = (dj & ALIBI_LOW_BITS).astype(F32)
    hi = (dj & ALIBI_HIGH_BIT).astype(F32)
    lane = jnp.arange(LANES)
    feat = jnp.where(lane[None, :] < 3, lo[:, None], jnp.where(lane[None, :] < 6, hi[:, None], 0.0))
    return (slopes[:, None, None] * feat[None]).astype(BF16)


def _diff_attn_kernel(slopes_ref, lam_ref, q_ref, k_ref, vt_ref, fk_ref, sg_ref, o_ref,
                      qa_scr, s0_scr, s1_scr, mt0_scr, mt1_scr, p0_scr, p1_scr, a0_scr, a1_scr,
                      m_scr, acc_scr, *, tq, tk, out_scale):
    h = pl.program_id(1)
    qi = pl.program_id(2)
    slope2 = slopes_ref[h] * LOG2E
    lam = lam_ref[0]
    nk = k_ref.shape[0] // tk
    i0 = qi * tq
    jd = i0 // tk
    cols2 = 2 * tq

    q = q_ref[...]
    lane = lax.broadcasted_iota(I32, q.shape, 1)
    zero = jnp.zeros_like(q)
    qq = jnp.concatenate([jnp.where(lane < DIFF_HEAD_DIM, q, zero),
                          jnp.where(lane >= DIFF_HEAD_DIM, q, zero)], axis=0)
    flane = lax.broadcasted_iota(I32, (cols2, LANES), 1)
    a, b, c = LOG2E_BF16_PARTS
    feat = jnp.where(flane % 3 == 0, a, jnp.where(flane % 3 == 1, b, c))
    feat = jnp.where(flane < 6, feat, 0.0)
    qa_scr[0] = jnp.concatenate([qq, feat.astype(BF16)], axis=1)
    qa_scr[1] = jnp.concatenate([qq, (-feat).astype(BF16)], axis=1)
    col = lax.broadcasted_iota(I32, (1, cols2), 1)
    q_pos = (i0 + jnp.where(col >= tq, col - tq, col)).astype(F32)

    m_scr[...] = jnp.full_like(m_scr, -jnp.inf)
    acc_scr[...] = jnp.zeros_like(acc_scr)
    nt = (((1,), (1,)), ((), ()))

    s_scr, mt_scr, p_scr, a_scr = (s0_scr, s1_scr), (mt0_scr, mt1_scr), (p0_scr, p1_scr), (a0_scr, a1_scr)

    def key_rows(j):
        return pl.ds(pl.multiple_of(j * tk, tk), tk)

    def exponentials(slot, col_shift):
        for cb in range(cols2 // LANES):
            cs = slice(cb * LANES, (cb + 1) * LANES)
            shift = col_shift[:, cs]
            m_old = m_scr[:, cs]
            m_new = jnp.maximum(m_old, mt_scr[slot][:, cs] - shift)
            alpha = jnp.exp2(m_old - m_new)
            p = jnp.exp2((s_scr[slot][:, cs] - (m_new + shift)).astype(BF16))
            m_scr[:, cs] = m_new
            a_scr[slot][:, cs] = alpha
            p_scr[slot][:, cs] = p

    def values(slot, j):
        acc_scr[...] = (a_scr[slot][...] * acc_scr[...]
                        + jnp.dot(vt_ref[:, key_rows(j)], p_scr[slot][...], preferred_element_type=F32))

    s = lax.dot_general(k_ref[key_rows(jd), :], qq, nt, preferred_element_type=F32)
    key_pos = (jd * tk + lax.broadcasted_iota(I32, (tk, 1), 0)).astype(F32)
    s = s - slope2 * jnp.abs(key_pos - q_pos)
    s_scr[1][...] = s
    mt_scr[1][...] = jnp.max(s, axis=0, keepdims=True)

    n_lin = nk - 1

    def tile_of(t):
        return t + (t >= jd).astype(I32)

    def scores(t, slot):
        j = tile_of(t)
        k_aug = jnp.concatenate([k_ref[key_rows(j), :], fk_ref[key_rows(j), :]], axis=1)
        s = lax.dot_general(k_aug, qa_scr[(j > jd).astype(I32)], nt, preferred_element_type=F32)
        s_scr[slot][...] = s
        mt_scr[slot][...] = jnp.max(s, axis=0, keepdims=True)

    def shift_of(t):
        j = tile_of(t)
        d = slope2 * (q_pos - (j * tk).astype(F32))
        return jnp.where(j > jd, -d, d)

    no_shift = jnp.zeros((1, cols2), F32)
    if n_lin == 0:
        exponentials(1, no_shift)
        values(1, jd)
    else:
        scores(0, 0)
        exponentials(1, no_shift)

        def stage(t, slot, with_scores=True):
            if with_scores:
                scores(t + 1, 1 - slot)
            values(1 - slot, jnp.where(t == 0, jd, tile_of(jnp.maximum(t - 1, 0))))
            exponentials(slot, shift_of(t))

        def pair(tt, _):
            stage(2 * tt, 0)
            stage(2 * tt + 1, 1)
            return 0

        lax.fori_loop(0, (n_lin - 1) // 2, pair, 0)
        if (n_lin - 1) % 2:
            stage(jnp.int32(n_lin - 2), (n_lin - 2) % 2)
        stage(jnp.int32(n_lin - 1), (n_lin - 1) % 2, with_scores=False)
        values((n_lin - 1) % 2, tile_of(jnp.int32(n_lin - 1)))

    o = acc_scr[:DIFF_V_DIM, :] / acc_scr[DIFF_V_DIM:DIFF_V_DIM + 1, :]
    o = o[:, :tq] - lam * o[:, tq:]
    o = o * lax.rsqrt(jnp.mean(o * o, axis=0, keepdims=True) + NORM_EPS)
    o_ref[...] = (o.T * sg_ref[...] * out_scale).astype(BF16)


def _diff_attention(qk, vt, slopes, lam, subln_gain, lambda_init, *, batch, seq, tq, tk):
    assert tk % tq == 0 and tk <= 2 * ALIBI_HIGH_BIT and tq % LANES == 0
    d = DIFF_HEADS * DIFF_V_DIM
    qkv3 = qk.reshape(batch, seq, 2 * d)
    fk = _alibi_key_features(slopes, seq, tk)
    out = pl.pallas_call(
        functools.partial(_diff_attn_kernel, tq=tq, tk=tk, out_scale=1.0 - lambda_init),
        grid=(batch, DIFF_HEADS, seq // tq),
        in_specs=[
            pl.BlockSpec(memory_space=pltpu.SMEM),
            pl.BlockSpec(memory_space=pltpu.SMEM),
            pl.BlockSpec((None, tq, DIFF_V_DIM), lambda b, h, i: (b, i, h)),
            pl.BlockSpec((None, seq, DIFF_V_DIM), lambda b, h, i: (b, 0, DIFF_HEADS + h)),
            pl.BlockSpec((None, None, DIFF_V_DIM + V_ONES_ROWS, seq), lambda b, h, i: (b, h, 0, 0)),
            pl.BlockSpec((None, seq, LANES), lambda b, h, i: (h, 0, 0)),
            pl.BlockSpec((1, DIFF_V_DIM), lambda b, h, i: (0, 0)),
        ],
        out_specs=pl.BlockSpec((None, tq, DIFF_V_DIM), lambda b, h, i: (b, i, h)),
        out_shape=jax.ShapeDtypeStruct((batch, seq, d), BF16),
        scratch_shapes=[
            pltpu.VMEM((2, 2 * tq, 2 * LANES), BF16),
            pltpu.VMEM((tk, 2 * tq), F32), pltpu.VMEM((tk, 2 * tq), F32),
            pltpu.VMEM((1, 2 * tq), F32), pltpu.VMEM((1, 2 * tq), F32),
            pltpu.VMEM((tk, 2 * tq), BF16), pltpu.VMEM((tk, 2 * tq), BF16),
            pltpu.VMEM((1, 2 * tq), F32), pltpu.VMEM((1, 2 * tq), F32),
            pltpu.VMEM((1, 2 * tq), F32),
            pltpu.VMEM((DIFF_V_DIM + V_ONES_ROWS, 2 * tq), F32),
        ],
        compiler_params=_params(("parallel", "parallel", "parallel")),
        name="diff_attn",
    )(slopes, lam.reshape(1), qkv3, qkv3, vt, fk, subln_gain.reshape(1, DIFF_V_DIM))
    return out.reshape(batch * seq, d)


def _retention_kernel(lgf_ref, lgb_ref, q_ref, k_ref, v_ref, g_ref, o_ref, ob_scr, st_scr, *, chunk):
    h = pl.program_id(1)
    lgf = lgf_ref[h]
    lgb = lgb_ref[h]
    seq = q_ref.shape[0]
    nc = seq // chunk
    c_f = float(chunk)

    row = lax.broadcasted_iota(I32, (chunk, chunk), 0)
    col = lax.broadcasted_iota(I32, (chunk, chunk), 1)
    rel = (row - col).astype(F32)
    d_both = (jnp.where(rel >= 0, jnp.exp(lgf * jnp.maximum(rel, 0.0)), 0.0)
              + jnp.where(rel < 0, jnp.exp(lgb * jnp.maximum(-rel, 0.0)), 0.0))
    idx = lax.broadcasted_iota(I32, (chunk, 1), 0).astype(F32)
    xi_f = jnp.exp(lgf * (idx + 1.0))
    zeta_f = jnp.exp(lgf * (c_f - 1.0 - idx))
    xi_b = jnp.exp(lgb * (c_f - idx))
    zeta_b = jnp.exp(lgb * idx)
    decay_f = jnp.exp(lgf * c_f)
    decay_b = jnp.exp(lgb * c_f)

    def state_update(k, v, zeta, decay):
        kz_t = (k.astype(F32) * zeta).T.astype(BF16)
        upd = jnp.dot(kz_t, v, preferred_element_type=F32)
        st_scr[...] = st_scr[...] * decay + upd

    st_scr[...] = jnp.zeros_like(st_scr)

    def bwd_body(i, _):
        c = nc - 1 - i
        rows = pl.ds(pl.multiple_of(c * chunk, chunk), chunk)
        q = q_ref[rows, :]
        inter = jnp.dot(q, st_scr[...].astype(BF16), preferred_element_type=F32)
        ob_scr[rows, :] = inter * xi_b
        state_update(k_ref[rows, :], v_ref[rows, :], zeta_b, decay_b)
        return 0

    lax.fori_loop(0, nc, bwd_body, 0, unroll=2)

    st_scr[...] = jnp.zeros_like(st_scr)

    def fwd_body(c, _):
        rows = pl.ds(pl.multiple_of(c * chunk, chunk), chunk)
        q = q_ref[rows, :]
        k = k_ref[rows, :]
        v = v_ref[rows, :]
        scores = lax.dot_general(q, k, (((1,), (1,)), ((), ())), preferred_element_type=F32) * d_both
        o = jnp.dot(scores.astype(BF16), v, preferred_element_type=F32)
        o = o + jnp.dot(q, st_scr[...].astype(BF16), preferred_element_type=F32) * xi_f
        o = o + ob_scr[rows, :]
        state_update(k, v, zeta_f, decay_f)
        g = g_ref[rows, :].astype(F32)
        o_ref[rows, :] = (g * jax.nn.sigmoid(g) * _rms(o)).astype(BF16)
        return 0

    lax.fori_loop(0, nc, fwd_body, 0, unroll=2)


def _retention(proj, lg_fwd, lg_bwd, *, batch, seq, d_model, chunk):
    dk = d_model // RET_HEADS
    dv = 2 * dk
    vw = RET_HEADS * dv
    p3 = proj.reshape(batch, seq, 2 * d_model + 2 * vw)
    out = pl.pallas_call(
        functools.partial(_retention_kernel, chunk=chunk),
        grid=(batch, RET_HEADS),
        in_specs=[
            pl.BlockSpec(memory_space=pltpu.SMEM),
            pl.BlockSpec(memory_space=pltpu.SMEM),
            pl.BlockSpec((None, seq, dk), lambda b, h: (b, 0, h)),
            pl.BlockSpec((None, seq, dk), lambda b, h: (b, 0, RET_HEADS + h)),
            pl.BlockSpec((None, seq, dv), lambda b, h: (b, 0, RET_HEADS + h)),
            pl.BlockSpec((None, seq, dv), lambda b, h: (b, 0, 2 * RET_HEADS + h)),
        ],
        out_specs=pl.BlockSpec((None, seq, dv), lambda b, h: (b, 0, h)),
        out_shape=jax.ShapeDtypeStruct((batch, seq, vw), BF16),
        scratch_shapes=[pltpu.VMEM((seq, dv), F32), pltpu.VMEM((dk, dv), F32)],
        compiler_params=_params(("parallel", "parallel")),
        name="retention",
    )(lg_fwd, lg_bwd, p3, p3, p3, p3)
    return out.reshape(batch * seq, vw)


def _out_router_kernel(o_ref, w_ref, x_ref, g_ref, rwh_ref, rwl_ref, rb_ref,
                       xn_ref, hm_ref, idx_ref, wts_ref, rank_ref, cnt_ref, carry_scr):
    i = pl.program_id(0)
    tm = x_ref.shape[0]

    @pl.when(i == 0)
    def _():
        carry_scr[...] = jnp.zeros_like(carry_scr)

    xn = x_ref[...] + jnp.dot(o_ref[...], w_ref[...], preferred_element_type=F32)
    xn_ref[...] = xn
    hm = _rms(xn) * g_ref[...]
    hm_ref[...] = hm

    hm_hi = hm.astype(BF16)
    hm_lo = (hm - hm_hi.astype(F32)).astype(BF16)
    nt = (((1,), (1,)), ((), ()))
    logits = (lax.dot_general(rwh_ref[...], hm_hi, nt, preferred_element_type=F32)
              + lax.dot_general(rwh_ref[...], hm_lo, nt, preferred_element_type=F32)
              + lax.dot_general(rwl_ref[...], hm_hi, nt, preferred_element_type=F32)
              + rb_ref[...])

    eidx = lax.broadcasted_iota(I32, (N_EXPERTS, tm), 0)
    work = logits
    vals, sels, hots = [], [], []
    for _ in range(TOP_K):
        m = jnp.max(work, axis=0, keepdims=True)
        sel = jnp.min(jnp.where(work == m, eidx, N_EXPERTS), axis=0, keepdims=True)
        hot = eidx == sel
        vals.append(m)
        sels.append(sel)
        hots.append(hot)
        work = jnp.where(hot, -jnp.inf, work)
    exps = [jnp.exp(v - vals[0]) for v in vals]
    denom = exps[0] + exps[1] + exps[2] + exps[3]

    mask = jnp.where(hots[0] | hots[1] | hots[2] | hots[3], 1.0, 0.0)
    tri = (lax.broadcasted_iota(I32, (tm, tm), 0) < lax.broadcasted_iota(I32, (tm, tm), 1))
    excl = jnp.dot(mask.astype(BF16), jnp.where(tri, 1.0, 0.0).astype(BF16), preferred_element_type=F32)
    rank_e = excl + carry_scr[:, 0:1]
    for k in range(TOP_K):
        idx_ref[k:k + 1, :] = sels[k]
        wts_ref[k:k + 1, :] = exps[k] / denom
        rank_ref[k:k + 1, :] = jnp.sum(jnp.where(hots[k], rank_e, 0.0), axis=0, keepdims=True).astype(I32)
    carry_scr[...] = carry_scr[...] + jnp.sum(mask, axis=1, keepdims=True)
    cnt_ref[...] = carry_scr[...].astype(I32)


def _out_router(o, w_out_bf16, x, moe_gain, rw_hi_t, rw_lo_t, router_b, *, tm):
    t, d = x.shape
    dv = o.shape[1]
    tok = lambda i: (i, 0)
    fixed = lambda i: (0, 0)
    lanes_tok = lambda i: (0, i)
    return pl.pallas_call(
        _out_router_kernel,
        grid=(t // tm,),
        in_specs=[
            pl.BlockSpec((tm, dv), tok),
            pl.BlockSpec((dv, d), fixed),
            pl.BlockSpec((tm, d), tok),
            pl.BlockSpec((1, d), fixed),
            pl.BlockSpec((N_EXPERTS, d), fixed),
            pl.BlockSpec((N_EXPERTS, d), fixed),
            pl.BlockSpec((N_EXPERTS, 1), fixed),
        ],
        out_specs=[
            pl.BlockSpec((tm, d), tok),
            pl.BlockSpec((tm, d), tok),
            pl.BlockSpec((TOP_K, tm), lanes_tok),
            pl.BlockSpec((TOP_K, tm), lanes_tok),
            pl.BlockSpec((TOP_K, tm), lanes_tok),
            pl.BlockSpec((N_EXPERTS, LANES), fixed),
        ],
        out_shape=[
            jax.ShapeDtypeStruct((t, d), F32),
            jax.ShapeDtypeStruct((t, d), F32),
            jax.ShapeDtypeStruct((TOP_K, t), I32),
            jax.ShapeDtypeStruct((TOP_K, t), F32),
            jax.ShapeDtypeStruct((TOP_K, t), I32),
            jax.ShapeDtypeStruct((N_EXPERTS, LANES), I32),
        ],
        scratch_shapes=[pltpu.VMEM((N_EXPERTS, LANES), F32)],
        compiler_params=_params(("arbitrary",)),
        name="out_router",
    )(o, w_out_bf16, x, moe_gain.reshape(1, d), rw_hi_t, rw_lo_t, router_b.reshape(N_EXPERTS, 1))


SUBLANES = 8


def _idx_slot_copy(dest_hbm, idx_smem, idx_sem, step, s):
    n_idx = dest_hbm.shape[1]
    return pltpu.make_async_copy(dest_hbm.at[step], idx_smem.at[pl.ds(pl.multiple_of(s * n_idx, n_idx), n_idx)],
                                 idx_sem.at[s])


def _dispatch_kernel(ends_ref, padded_ref, dest_hbm, hm_hbm, xs_hbm, idx_smem, tok_scr, zero_scr,
                     idx_sem, tok_sem, fill_sem, row_sem, *, tm, tr, n_tiles):
    i = pl.program_id(0)
    n = pl.num_programs(0)
    slot = i % 2
    groups = tm // SUBLANES

    def idx_copy(step, s):
        return _idx_slot_copy(dest_hbm, idx_smem, idx_sem, step, s)

    def tok_copy(step, s):
        return pltpu.make_async_copy(hm_hbm.at[pl.ds(step * groups, groups)], tok_scr.at[s], tok_sem.at[s])

    def drain_rows(s):
        for _ in range(TOP_K):
            pltpu.make_async_copy(xs_hbm.at[pl.ds(0, tm), :], xs_hbm.at[pl.ds(0, tm), :], row_sem.at[s]).wait()

    @pl.when(i == 0)
    def _():
        idx_copy(0, 0).start()
        tok_copy(0, 0).start()
        zero_scr[...] = jnp.zeros_like(zero_scr)

        def fill(start):
            return pltpu.make_async_copy(zero_scr, xs_hbm.at[pl.ds(pl.multiple_of(start, tr), tr), :], fill_sem)

        first_unused = ends_ref[N_EXPERTS - 1] // tr
        for e in range(N_EXPERTS):
            @pl.when(padded_ref[e] > 0)
            def _():
                fill(ends_ref[e] - tr).start()
        lax.fori_loop(first_unused, n_tiles, lambda c, _: (fill(c * tr).start(), 0)[1], 0)
        for e in range(N_EXPERTS):
            @pl.when(padded_ref[e] > 0)
            def _():
                fill(0).wait()
        lax.fori_loop(first_unused, n_tiles, lambda c, _: (fill(0).wait(), 0)[1], 0)

    @pl.when(i + 1 < n)
    def _():
        idx_copy(i + 1, 1 - slot).start()

    idx_copy(i, slot).wait()
    tok_copy(i, slot).wait()

    def scatter_rows(s):
        for g in range(groups):
            for u in range(SUBLANES):
                src = tok_scr.at[s, g, pl.ds(u, 1), :]
                for k in range(TOP_K):
                    row = idx_smem[s * TOP_K * tm + k * tm + g * SUBLANES + u]
                    pltpu.make_async_copy(src, xs_hbm.at[pl.ds(row, 1), :], row_sem.at[s]).start(priority=k % 2)

    for s in range(2):
        @pl.when(slot == s)
        def _():
            scatter_rows(s)

    @pl.when(i >= 1)
    def _():
        drain_rows(1 - slot)

    @pl.when(i + 1 < n)
    def _():
        tok_copy(i + 1, 1 - slot).start()

    @pl.when(i == n - 1)
    def _():
        drain_rows(slot)


def _dispatch(hm, dest_tiles, ends, padded, *, tm, tr, n_rows):
    t, d = hm.shape
    return pl.pallas_call(
        functools.partial(_dispatch_kernel, tm=tm, tr=tr, n_tiles=n_rows // tr),
        grid_spec=pltpu.PrefetchScalarGridSpec(
            num_scalar_prefetch=2,
            grid=(t // tm,),
            in_specs=[
                pl.BlockSpec(memory_space=pl.ANY),
                pl.BlockSpec(memory_space=pl.ANY),
            ],
            out_specs=pl.BlockSpec(memory_space=pl.ANY),
            scratch_shapes=[
                pltpu.SMEM((2 * TOP_K * tm,), I32),
                pltpu.VMEM((2, tm // SUBLANES, SUBLANES, d), F32),
                pltpu.VMEM((tr, d), F32),
                pltpu.SemaphoreType.DMA((2,)),
                pltpu.SemaphoreType.DMA((2,)),
                pltpu.SemaphoreType.DMA(()),
                pltpu.SemaphoreType.DMA((2,)),
            ],
        ),
        out_shape=jax.ShapeDtypeStruct((n_rows, d), F32),
        compiler_params=_params(("arbitrary",)),
        name="moe_dispatch",
    )(ends, padded, dest_tiles, hm.reshape(t // SUBLANES, SUBLANES, d))


MXU_WIDTH = 256


def _split_w1_kernel(w_ref, perm_ref, g_ref, l_ref):
    half = MXU_WIDTH // 2
    for c in range(w_ref.shape[1] // MXU_WIDTH):
        y = jnp.dot(w_ref[:, c * MXU_WIDTH:(c + 1) * MXU_WIDTH].astype(BF16), perm_ref[...],
                    preferred_element_type=F32)
        g_ref[:, c * half:(c + 1) * half] = y[:, :half].astype(BF16)
        l_ref[:, c * half:(c + 1) * half] = y[:, half:].astype(BF16)


def _split_w1(w1_layers, layer):
    _, e, d, ff2 = w1_layers.shape
    ff = ff2 // 2
    rows = min(512, d)
    src = jnp.arange(MXU_WIDTH)
    dst = jnp.where(src % 2 == 0, src // 2, MXU_WIDTH // 2 + src // 2)
    perm = (dst[:, None] == jnp.arange(MXU_WIDTH)[None, :]).astype(BF16)
    blk = lambda i, r: (i, r, 0)
    return pl.pallas_call(
        _split_w1_kernel,
        grid=(e, d // rows),
        in_specs=[pl.BlockSpec((None, None, rows, ff2), lambda i, r: (layer, i, r, 0)),
                  pl.BlockSpec((MXU_WIDTH, MXU_WIDTH), lambda i, r: (0, 0))],
        out_specs=[pl.BlockSpec((None, rows, ff), blk), pl.BlockSpec((None, rows, ff), blk)],
        out_shape=[jax.ShapeDtypeStruct((e, d, ff), BF16), jax.ShapeDtypeStruct((e, d, ff), BF16)],
        compiler_params=_params(("parallel", "parallel")),
        name="split_w1",
    )(w1_layers, perm)


def _experts_kernel(te_ref, tv_ref, xs_ref, w1g_ref, w1l_ref, b1g_ref, b1l_ref, w2_ref, b2_ref, ys_ref):
    i = pl.program_id(0)

    @pl.when(tv_ref[i] > 0)
    def _():
        x = xs_ref[...].astype(BF16)
        glu = jnp.dot(x, w1g_ref[...], preferred_element_type=F32) + b1g_ref[...]
        lin = jnp.dot(x, w1l_ref[...], preferred_element_type=F32) + b1l_ref[...]
        glu = jnp.minimum(glu, SWIGLU_LIMIT)
        lin = jnp.clip(lin, -SWIGLU_LIMIT, SWIGLU_LIMIT)
        act = glu * jax.nn.sigmoid(SWIGLU_ALPHA * glu) * (lin + 1.0)
        ys_ref[...] = (jnp.dot(act.astype(BF16), w2_ref[...].astype(BF16), preferred_element_type=F32)
                       + b2_ref[...])

    @pl.when(tv_ref[i] == 0)
    def _():
        ys_ref[...] = jnp.zeros_like(ys_ref)


def _experts(xs, tile_expert, tile_valid, w1g, w1l, b1g, b1l, w2_layers, layer, b2, *, tr):
    n_rows, d = xs.shape
    ff = w1g.shape[2]
    n_tiles = n_rows // tr
    row_tile = lambda i, te, tv: (i, 0)
    by_expert = lambda i, te, tv: (te[i], 0, 0)
    return pl.pallas_call(
        _experts_kernel,
        grid_spec=pltpu.PrefetchScalarGridSpec(
            num_scalar_prefetch=2,
            grid=(n_tiles,),
            in_specs=[
                pl.BlockSpec((tr, d), row_tile),
                pl.BlockSpec((None, d, ff), by_expert),
                pl.BlockSpec((None, d, ff), by_expert),
                pl.BlockSpec((None, 1, ff), by_expert),
                pl.BlockSpec((None, 1, ff), by_expert),
                pl.BlockSpec((None, None, ff, d), lambda i, te, tv: (layer, te[i], 0, 0)),
                pl.BlockSpec((None, 1, d), by_expert),
            ],
            out_specs=pl.BlockSpec((tr, d), row_tile),
        ),
        out_shape=jax.ShapeDtypeStruct((n_rows, d), F32),
        compiler_params=_params(("arbitrary",)),
        name="moe_experts",
    )(tile_expert, tile_valid, xs, w1g, w1l, b1g, b1l, w2_layers, b2)


def _combine_kernel(dest_hbm, ys_hbm, x_ref, wt_ref, o_ref, idx_smem, rows_scr, idx_sem, row_sem, *, tm):
    i = pl.program_id(0)
    n = pl.num_programs(0) - 1
    groups = TOP_K * tm // SUBLANES
    per_k = tm // SUBLANES

    def idx_copy(step, s):
        return _idx_slot_copy(dest_hbm, idx_smem, idx_sem, step, s)

    def gather_rows(s):
        for g in range(groups):
            for u in range(SUBLANES):
                row = idx_smem[s * TOP_K * tm + g * SUBLANES + u]
                pltpu.make_async_copy(ys_hbm.at[pl.ds(row, 1), :], rows_scr.at[s, g, pl.ds(u, 1), :],
                                      row_sem.at[s]).start(priority=u % 2)

    def combine_rows(s):
        for k in range(TOP_K):
            part = rows_scr.at[s, pl.ds(k * per_k, per_k)]
            pltpu.make_async_copy(part, part, row_sem.at[s]).wait()
        out = x_ref[...]
        for k in range(TOP_K):
            rows = rows_scr[s, k * per_k:(k + 1) * per_k].reshape(tm, x_ref.shape[1])
            out = out + wt_ref[:, k:k + 1] * rows
        o_ref[...] = out

    @pl.when(i == 0)
    def _():
        idx_copy(0, 0).start()

    for s in range(2):
        @pl.when((i < n) & (i % 2 == s))
        def _():
            idx_copy(i, s).wait()

            @pl.when(i + 1 < n)
            def _():
                idx_copy(i + 1, 1 - s).start()

            gather_rows(s)

    for s in range(2):
        @pl.when((i >= 1) & ((i - 1) % 2 == s))
        def _():
            combine_rows(s)


def _combine(ys, dest_tiles, x, wts_t, *, tm):
    t, d = x.shape
    prev = lambda i: (jnp.maximum(i - 1, 0), 0)
    return pl.pallas_call(
        functools.partial(_combine_kernel, tm=tm),
        grid=(t // tm + 1,),
        in_specs=[
            pl.BlockSpec(memory_space=pl.ANY),
            pl.BlockSpec(memory_space=pl.ANY),
            pl.BlockSpec((tm, d), prev),
            pl.BlockSpec((tm, TOP_K), prev),
        ],
        out_specs=pl.BlockSpec((tm, d), prev),
        out_shape=jax.ShapeDtypeStruct((t, d), F32),
        scratch_shapes=[
            pltpu.SMEM((2 * TOP_K * tm,), I32),
            pltpu.VMEM((2, TOP_K * tm // SUBLANES, SUBLANES, d), F32),
            pltpu.SemaphoreType.DMA((2,)),
            pltpu.SemaphoreType.DMA((2,)),
        ],
        compiler_params=_params(("arbitrary",)),
        name="moe_combine",
    )(dest_tiles, ys, x, wts_t)


def _tiles(batch, seq):
    t = batch * seq
    tm = min(512, t)
    return dict(
        tm=tm,
        tn=256,
        tq=min(512, seq),
        tk=min(512, seq),
        chunk=min(256, seq),
        tmd=min(256, t),
        tr=min(512, max(8, TOP_K * t // N_EXPERTS)),
    )


def _moe(o, w_out, x, moe_gain, router_w, router_b, w1_layers, b1, w2_layers, b2, layer, *, cfg):
    t, d = x.shape
    tm, tmd, tr = cfg["tm"], cfg["tmd"], cfg["tr"]
    rw_t = router_w.T
    rw_hi = rw_t.astype(BF16)
    rw_lo = (rw_t - rw_hi.astype(F32)).astype(BF16)
    x, hm, idx, wts, rank, cnt = _out_router(o, w_out.astype(BF16), x, moe_gain, rw_hi, rw_lo, router_b, tm=tm)

    counts = cnt[:, 0]
    padded = ((counts + tr - 1) // tr) * tr
    ends = jnp.cumsum(padded).astype(I32)
    offs = ends - padded
    experts = jnp.arange(N_EXPERTS, dtype=I32)[:, None, None]
    dest = rank + jnp.sum(jnp.where(idx[None] == experts, offs[:, None, None], 0), axis=0)
    dest_tiles = dest.reshape(TOP_K, t // tmd, tmd).transpose(1, 0, 2).reshape(t // tmd, TOP_K * tmd)
    n_rows = TOP_K * t + N_EXPERTS * tr
    tile_start = jnp.arange(n_rows // tr, dtype=I32) * tr
    tile_expert = jnp.minimum(jnp.sum((tile_start[:, None] >= ends[None, :]).astype(I32), axis=1), N_EXPERTS - 1)
    tile_valid = (tile_start < ends[-1]).astype(I32)

    xs = _dispatch(hm, dest_tiles, ends, padded.astype(I32), tm=tmd, tr=tr, n_rows=n_rows)
    ff = w2_layers.shape[2]
    w1g, w1l = _split_w1(w1_layers, layer)
    ys = _experts(xs, tile_expert, tile_valid, w1g, w1l,
                  b1[:, 0::2].reshape(N_EXPERTS, 1, ff), b1[:, 1::2].reshape(N_EXPERTS, 1, ff),
                  w2_layers, layer, b2.reshape(N_EXPERTS, 1, d), tr=tr)
    return _combine(ys, dest_tiles, x, wts.T, tm=tmd)


def kernel(x, diff_norm, diff_w_in, diff_w_out, diff_lambda_q1, diff_lambda_k1, diff_lambda_q2, diff_lambda_k2, diff_q_norm, diff_k_norm, diff_subln, ret_norm, ret_w_in, ret_w_out, ret_log_decay_fwd, ret_log_decay_bwd, moe_norm, moe_router_w, moe_router_b, moe_w1, moe_b1, moe_w2, moe_b2):
    batch, seq, d = x.shape
    depth = moe_norm.shape[0]
    cfg = _tiles(batch, seq)
    x = x.reshape(batch * seq, d)
    slopes = jnp.exp2(-8.0 * jnp.arange(1, DIFF_HEADS + 1, dtype=F32) / DIFF_HEADS)
    for i in range(depth):
        j = i // N_MIXERS
        if i % N_MIXERS == 0:
            lambda_init = _lambda_init(i)
            lam = (jnp.exp(jnp.sum(diff_lambda_q1[j] * diff_lambda_k1[j]))
                   - jnp.exp(jnp.sum(diff_lambda_q2[j] * diff_lambda_k2[j])) + lambda_init)
            reps = d // DIFF_HEAD_DIM
            qk_gain = jnp.concatenate([jnp.tile(diff_q_norm[j], reps) * (DIFF_HEAD_DIM ** -0.5 * LOG2E),
                                       jnp.tile(diff_k_norm[j], reps)]).reshape(1, 2 * d)
            qk, vt = _in_proj(x, diff_norm[j], diff_w_in[j].astype(BF16), tm=cfg["tm"], tn=cfg["tn"],
                              qk_gain_row=qk_gain, values_layout=(batch, seq))
            o = _diff_attention(qk, vt, slopes, lam, diff_subln[j], lambda_init,
                                batch=batch, seq=seq, tq=cfg["tq"], tk=cfg["tk"])
            w_out = diff_w_out[j]
        else:
            dk = d // RET_HEADS
            col_scale = jnp.ones((ret_w_in.shape[2],), F32).at[d:2 * d].set(dk ** -0.5)
            proj, = _in_proj(x, ret_norm[j], (ret_w_in[j] * col_scale).astype(BF16), tm=cfg["tm"], tn=cfg["tn"])
            o = _retention(proj, ret_log_decay_fwd[j], ret_log_decay_bwd[j],
                           batch=batch, seq=seq, d_model=d, chunk=cfg["chunk"])
            w_out = ret_w_out[j]
        x = _moe(o, w_out, x, moe_norm[i], moe_router_w[i], moe_router_b[i],
                 moe_w1, moe_b1[i], moe_w2, moe_b2[i], i, cfg=cfg)
    return x.reshape(batch, seq, d)
```

```python
import functools
import math

import jax
import jax.numpy as jnp
from jax import lax
from jax.experimental import pallas as pl
from jax.experimental.pallas import tpu as pltpu

F32 = jnp.float32
BF16 = jnp.bfloat16
I32 = jnp.int32

NORM_EPS = 1e-5
N_MIXERS = 2

DIFF_HEADS = 8
DIFF_HEAD_DIM = 64
DIFF_V_DIM = 2 * DIFF_HEAD_DIM

RET_HEADS = 4

N_EXPERTS = 32
TOP_K = 4
SWIGLU_ALPHA = 1.702
SWIGLU_LIMIT = 7.0

V7X_VMEM_LIMIT_BYTES = 56 * 1024 * 1024
LANES = 128


def _lambda_init(layer_idx):
    return 0.8 - 0.6 * math.exp(-0.3 * layer_idx)


def _params(semantics):
    return pltpu.CompilerParams(dimension_semantics=semantics, vmem_limit_bytes=V7X_VMEM_LIMIT_BYTES)


def _rms(x, eps=NORM_EPS):
    return x * lax.rsqrt(jnp.mean(x * x, axis=-1, keepdims=True) + eps)


def _in_proj_kernel(x_ref, g_ref, w_ref, gmat_ref, qkg_ref, o_ref, *rest, tn, n_normed, n_transposed):
    vt_ref, h_scr = rest if n_transposed else (None, rest[0])
    h_scr[...] = (_rms(x_ref[...]) * g_ref[...]).astype(BF16)
    n_total = w_ref.shape[1]
    n_chunks = n_total // tn
    n_rowmajor = n_total - n_transposed
    if n_transposed:
        vt_ref[:, DIFF_V_DIM:, :] = jnp.ones((vt_ref.shape[0], V_ONES_ROWS, vt_ref.shape[2]), BF16)

    def project(c):
        return jnp.dot(h_scr[...], w_ref[:, c * tn:(c + 1) * tn], preferred_element_type=F32)

    def finish(c, acc):
        cols = slice(c * tn, (c + 1) * tn)
        if c * tn < n_normed:
            sq = acc * acc
            hi = sq.astype(BF16)
            lo = (sq - hi.astype(F32)).astype(BF16)
            msq = (jnp.dot(hi, gmat_ref[...], preferred_element_type=F32)
                   + jnp.dot(lo, gmat_ref[...], preferred_element_type=F32))
            acc = acc * lax.rsqrt(msq + NORM_EPS) * qkg_ref[:, cols]
        if c * tn < n_rowmajor:
            o_ref[:, cols] = acc.astype(BF16)
        else:
            for part in range(tn // DIFF_V_DIM):
                head = (c * tn - n_rowmajor) // DIFF_V_DIM + part
                vt_ref[head, :DIFF_V_DIM, :] = acc[:, part * DIFF_V_DIM:(part + 1) * DIFF_V_DIM].T.astype(BF16)

    acc = project(0)
    for c in range(1, n_chunks):
        nxt = project(c)
        finish(c - 1, acc)
        acc = nxt
    finish(n_chunks - 1, acc)


def _in_proj(x, gain, w_bf16, *, tm, tn, qk_gain_row=None, values_layout=None):
    t, d = x.shape
    n = w_bf16.shape[1]
    if qk_gain_row is None:
        n_normed = 0
        qk_gain_row = jnp.zeros((1, n), F32)
    else:
        n_normed = qk_gain_row.shape[1]
        qk_gain_row = jnp.pad(qk_gain_row, ((0, 0), (0, n - n_normed)))
    grp = jnp.arange(tn) // DIFF_HEAD_DIM
    gmat = jnp.where(grp[:, None] == grp[None, :], 1.0 / DIFF_HEAD_DIM, 0.0).astype(BF16)
    n_transposed = DIFF_HEADS * DIFF_V_DIM if values_layout else 0
    n_rowmajor = n - n_transposed
    out_specs = [pl.BlockSpec((tm, n_rowmajor), lambda i: (i, 0))]
    out_shape = [jax.ShapeDtypeStruct((t, n_rowmajor), BF16)]
    if values_layout:
        batch, seq = values_layout
        per_seq = seq // tm
        rows = DIFF_V_DIM + V_ONES_ROWS
        out_specs.append(pl.BlockSpec((None, DIFF_HEADS, rows, tm), lambda i: (i // per_seq, 0, 0, i % per_seq)))
        out_shape.append(jax.ShapeDtypeStruct((batch, DIFF_HEADS, rows, seq), BF16))
    return pl.pallas_call(
        functools.partial(_in_proj_kernel, tn=tn, n_normed=n_normed, n_transposed=n_transposed),
        grid=(t // tm,),
        in_specs=[
            pl.BlockSpec((tm, d), lambda i: (i, 0)),
            pl.BlockSpec((1, d), lambda i: (0, 0)),
            pl.BlockSpec((d, n), lambda i: (0, 0)),
            pl.BlockSpec((tn, tn), lambda i: (0, 0)),
            pl.BlockSpec((1, n), lambda i: (0, 0)),
        ],
        out_specs=out_specs,
        out_shape=out_shape,
        scratch_shapes=[pltpu.VMEM((tm, d), BF16)],
        compiler_params=_params(("parallel",)),
        name="in_proj",
    )(x, gain.reshape(1, d), w_bf16, gmat, qk_gain_row)


LOG2E = math.log2(math.e)
LOG2E_BF16_PARTS = (1.4453125, -0.00262451171875, 7.063150405883789e-06)
ALIBI_LOW_BITS = 255
ALIBI_HIGH_BIT = 256
V_ONES_ROWS = 16


def _alibi_key_features(slopes, seq, tk):
    dj = jnp.arange(seq, dtype=I32) % tk
    lo = (dj & ALIBI_LOW_BITS).astype(F32)
    hi = (dj & ALIBI_HIGH_BIT).astype(F32)
    lane = jnp.arange(LANES)
    feat = jnp.where(lane[None, :] < 3, lo[:, None], jnp.where(lane[None, :] < 6, hi[:, None], 0.0))
    return (slopes[:, None, None] * feat[None]).astype(BF16)


def _diff_attn_kernel(slopes_ref, lam_ref, q_ref, k_ref, vt_ref, fk_ref, sg_ref, o_ref,
                      qa_scr, s0_scr, s1_scr, mt0_scr, mt1_scr, p0_scr, p1_scr, a0_scr, a1_scr,
                      m_scr, acc_scr, *, tq, tk, out_scale):
    h = pl.program_id(1)
    qi = pl.program_id(2)
    slope2 = slopes_ref[h] * LOG2E
    lam = lam_ref[0]
    nk = k_ref.shape[0] // tk
    i0 = qi * tq
    jd = i0 // tk
    cols2 = 2 * tq

    q = q_ref[...]
    lane = lax.broadcasted_iota(I32, q.shape, 1)
    zero = jnp.zeros_like(q)
    qq = jnp.concatenate([jnp.where(lane < DIFF_HEAD_DIM, q, zero),
                          jnp.where(lane >= DIFF_HEAD_DIM, q, zero)], axis=0)
    flane = lax.broadcasted_iota(I32, (cols2, LANES), 1)
    a, b, c = LOG2E_BF16_PARTS
    feat = jnp.where(flane % 3 == 0, a, jnp.where(flane % 3 == 1, b, c))
    feat = jnp.where(flane < 6, feat, 0.0)
    qa_scr[0] = jnp.concatenate([qq, feat.astype(BF16)], axis=1)
    qa_scr[1] = jnp.concatenate([qq, (-feat).astype(BF16)], axis=1)
    col = lax.broadcasted_iota(I32, (1, cols2), 1)
    q_pos = (i0 + jnp.where(col >= tq, col - tq, col)).astype(F32)

    m_scr[...] = jnp.full_like(m_scr, -jnp.inf)
    acc_scr[...] = jnp.zeros_like(acc_scr)
    nt = (((1,), (1,)), ((), ()))

    s_scr, mt_scr, p_scr, a_scr = (s0_scr, s1_scr), (mt0_scr, mt1_scr), (p0_scr, p1_scr), (a0_scr, a1_scr)

    def key_rows(j):
        return pl.ds(pl.multiple_of(j * tk, tk), tk)

    def exponentials(slot, col_shift):
        for cb in range(cols2 // LANES):
            cs = slice(cb * LANES, (cb + 1) * LANES)
            shift = col_shift[:, cs]
            m_old = m_scr[:, cs]
            m_new = jnp.maximum(m_old, mt_scr[slot][:, cs] - shift)
            alpha = jnp.exp2(m_old - m_new)
            p = jnp.exp2((s_scr[slot][:, cs] - (m_new + shift)).astype(BF16))
            m_scr[:, cs] = m_new
            a_scr[slot][:, cs] = alpha
            p_scr[slot][:, cs] = p

    def values(slot, j):
        acc_scr[...] = (a_scr[slot][...] * acc_scr[...]
                        + jnp.dot(vt_ref[:, key_rows(j)], p_scr[slot][...], preferred_element_type=F32))

    s = lax.dot_general(k_ref[key_rows(jd), :], qq, nt, preferred_element_type=F32)
    key_pos = (jd * tk + lax.broadcasted_iota(I32, (tk, 1), 0)).astype(F32)
    s = s - slope2 * jnp.abs(key_pos - q_pos)
    s_scr[1][...] = s
    mt_scr[1][...] = jnp.max(s, axis=0, keepdims=True)

    n_lin = nk - 1

    def tile_of(t):
        return t + (t >= jd).astype(I32)

    def scores(t, slot):
        j = tile_of(t)
        k_aug = jnp.concatenate([k_ref[key_rows(j), :], fk_ref[key_rows(j), :]], axis=1)
        s = lax.dot_general(k_aug, qa_scr[(j > jd).astype(I32)], nt, preferred_element_type=F32)
        s_scr[slot][...] = s
        mt_scr[slot][...] = jnp.max(s, axis=0, keepdims=True)

    def shift_of(t):
        j = tile_of(t)
        d = slope2 * (q_pos - (j * tk).astype(F32))
        return jnp.where(j > jd, -d, d)

    no_shift = jnp.zeros((1, cols2), F32)
    if n_lin == 0:
        exponentials(1, no_shift)
        values(1, jd)
    else:
        scores(0, 0)
        exponentials(1, no_shift)

        def stage(t, slot, with_scores=True):
            if with_scores:
                scores(t + 1, 1 - slot)
            values(1 - slot, jnp.where(t == 0, jd, tile_of(jnp.maximum(t - 1, 0))))
            exponentials(slot, shift_of(t))

        def pair(tt, _):
            stage(2 * tt, 0)
            stage(2 * tt + 1, 1)
            return 0

        lax.fori_loop(0, (n_lin - 1) // 2, pair, 0)
        if (n_lin - 1) % 2:
            stage(jnp.int32(n_lin - 2), (n_lin - 2) % 2)
        stage(jnp.int32(n_lin - 1), (n_lin - 1) % 2, with_scores=False)
        values((n_lin - 1) % 2, tile_of(jnp.int32(n_lin - 1)))

    o = acc_scr[:DIFF_V_DIM, :] / acc_scr[DIFF_V_DIM:DIFF_V_DIM + 1, :]
    o = o[:, :tq] - lam * o[:, tq:]
    o = o * lax.rsqrt(jnp.mean(o * o, axis=0, keepdims=True) + NORM_EPS)
    o_ref[...] = (o.T * sg_ref[...] * out_scale).astype(BF16)


def _diff_attention(qk, vt, slopes, lam, subln_gain, lambda_init, *, batch, seq, tq, tk):
    assert tk % tq == 0 and tk <= 2 * ALIBI_HIGH_BIT and tq % LANES == 0
    d = DIFF_HEADS * DIFF_V_DIM
    qkv3 = qk.reshape(batch, seq, 2 * d)
    fk = _alibi_key_features(slopes, seq, tk)
    out = pl.pallas_call(
        functools.partial(_diff_attn_kernel, tq=tq, tk=tk, out_scale=1.0 - lambda_init),
        grid=(batch, DIFF_HEADS, seq // tq),
        in_specs=[
            pl.BlockSpec(memory_space=pltpu.SMEM),
            pl.BlockSpec(memory_space=pltpu.SMEM),
            pl.BlockSpec((None, tq, DIFF_V_DIM), lambda b, h, i: (b, i, h)),
            pl.BlockSpec((None, seq, DIFF_V_DIM), lambda b, h, i: (b, 0, DIFF_HEADS + h)),
            pl.BlockSpec((None, None, DIFF_V_DIM + V_ONES_ROWS, seq), lambda b, h, i: (b, h, 0, 0)),
            pl.BlockSpec((None, seq, LANES), lambda b, h, i: (h, 0, 0)),
            pl.BlockSpec((1, DIFF_V_DIM), lambda b, h, i: (0, 0)),
        ],
        out_specs=pl.BlockSpec((None, tq, DIFF_V_DIM), lambda b, h, i: (b, i, h)),
        out_shape=jax.ShapeDtypeStruct((batch, seq, d), BF16),
        scratch_shapes=[
            pltpu.VMEM((2, 2 * tq, 2 * LANES), BF16),
            pltpu.VMEM((tk, 2 * tq), F32), pltpu.VMEM((tk, 2 * tq), F32),
            pltpu.VMEM((1, 2 * tq), F32), pltpu.VMEM((1, 2 * tq), F32),
            pltpu.VMEM((tk, 2 * tq), BF16), pltpu.VMEM((tk, 2 * tq), BF16),
            pltpu.VMEM((1, 2 * tq), F32), pltpu.VMEM((1, 2 * tq), F32),
            pltpu.VMEM((1, 2 * tq), F32),
            pltpu.VMEM((DIFF_V_DIM + V_ONES_ROWS, 2 * tq), F32),
        ],
        compiler_params=_params(("parallel", "parallel", "parallel")),
        name="diff_attn",
    )(slopes, lam.reshape(1), qkv3, qkv3, vt, fk, subln_gain.reshape(1, DIFF_V_DIM))
    return out.reshape(batch * seq, d)


def _retention_kernel(lgf_ref, lgb_ref, q_ref, k_ref, v_ref, g_ref, o_ref, ob_scr, st_scr, *, chunk):
    h = pl.program_id(1)
    lgf = lgf_ref[h]
    lgb = lgb_ref[h]
    seq = q_ref.shape[0]
    nc = seq // chunk
    c_f = float(chunk)

    row = lax.broadcasted_iota(I32, (chunk, chunk), 0)
    col = lax.broadcasted_iota(I32, (chunk, chunk), 1)
    rel = (row - col).astype(F32)
    d_both = (jnp.where(rel >= 0, jnp.exp(lgf * jnp.maximum(rel, 0.0)), 0.0)
              + jnp.where(rel < 0, jnp.exp(lgb * jnp.maximum(-rel, 0.0)), 0.0))
    idx = lax.broadcasted_iota(I32, (chunk, 1), 0).astype(F32)
    xi_f = jnp.exp(lgf * (idx + 1.0))
    zeta_f = jnp.exp(lgf * (c_f - 1.0 - idx))
    xi_b = jnp.exp(lgb * (c_f - idx))
    zeta_b = jnp.exp(lgb * idx)
    decay_f = jnp.exp(lgf * c_f)
    decay_b = jnp.exp(lgb * c_f)

    def state_update(k, v, zeta, decay):
        kz_t = (k.astype(F32) * zeta).T.astype(BF16)
        upd = jnp.dot(kz_t, v, preferred_element_type=F32)
        st_scr[...] = st_scr[...] * decay + upd

    st_scr[...] = jnp.zeros_like(st_scr)

    def bwd_body(i, _):
        c = nc - 1 - i
        rows = pl.ds(pl.multiple_of(c * chunk, chunk), chunk)
        q = q_ref[rows, :]
        inter = jnp.dot(q, st_scr[...].astype(BF16), preferred_element_type=F32)
        ob_scr[rows, :] = inter * xi_b
        state_update(k_ref[rows, :], v_ref[rows, :], zeta_b, decay_b)
        return 0

    lax.fori_loop(0, nc, bwd_body, 0, unroll=2)

    st_scr[...] = jnp.zeros_like(st_scr)

    def fwd_body(c, _):
        rows = pl.ds(pl.multiple_of(c * chunk, chunk), chunk)
        q = q_ref[rows, :]
        k = k_ref[rows, :]
        v = v_ref[rows, :]
        scores = lax.dot_general(q, k, (((1,), (1,)), ((), ())), preferred_element_type=F32) * d_both
        o = jnp.dot(scores.astype(BF16), v, preferred_element_type=F32)
        o = o + jnp.dot(q, st_scr[...].astype(BF16), preferred_element_type=F32) * xi_f
        o = o + ob_scr[rows, :]
        state_update(k, v, zeta_f, decay_f)
        g = g_ref[rows, :].astype(F32)
        o_ref[rows, :] = (g * jax.nn.sigmoid(g) * _rms(o)).astype(BF16)
        return 0

    lax.fori_loop(0, nc, fwd_body, 0, unroll=2)


def _retention(proj, lg_fwd, lg_bwd, *, batch, seq, d_model, chunk):
    dk = d_model // RET_HEADS
    dv = 2 * dk
    vw = RET_HEADS * dv
    p3 = proj.reshape(batch, seq, 2 * d_model + 2 * vw)
    out = pl.pallas_call(
        functools.partial(_retention_kernel, chunk=chunk),
        grid=(batch, RET_HEADS),
        in_specs=[
            pl.BlockSpec(memory_space=pltpu.SMEM),
            pl.BlockSpec(memory_space=pltpu.SMEM),
            pl.BlockSpec((None, seq, dk), lambda b, h: (b, 0, h)),
            pl.BlockSpec((None, seq, dk), lambda b, h: (b, 0, RET_HEADS + h)),
            pl.BlockSpec((None, seq, dv), lambda b, h: (b, 0, RET_HEADS + h)),
            pl.BlockSpec((None, seq, dv), lambda b, h: (b, 0, 2 * RET_HEADS + h)),
        ],
        out_specs=pl.BlockSpec((None, seq, dv), lambda b, h: (b, 0, h)),
        out_shape=jax.ShapeDtypeStruct((batch, seq, vw), BF16),
        scratch_shapes=[pltpu.VMEM((seq, dv), F32), pltpu.VMEM((dk, dv), F32)],
        compiler_params=_params(("parallel", "parallel")),
        name="retention",
    )(lg_fwd, lg_bwd, p3, p3, p3, p3)
    return out.reshape(batch * seq, vw)


BF16_BITS = 16


def _pack_bf16_halves(x):
    n = x.shape[1] // 2
    hi = lax.bitcast_convert_type(x[:, :n].astype(BF16).astype(F32), jnp.uint32)
    lo = lax.bitcast_convert_type(x[:, n:].astype(BF16).astype(F32), jnp.uint32)
    return lax.bitcast_convert_type(hi | (lo >> BF16_BITS), I32)


def _unpack_bf16_halves(w):
    u = lax.bitcast_convert_type(w, jnp.uint32)
    hi = lax.bitcast_convert_type(u & jnp.uint32(0xFFFF0000), F32).astype(BF16)
    lo = lax.bitcast_convert_type(u << BF16_BITS, F32).astype(BF16)
    return jnp.concatenate([hi, lo], axis=1)


def _out_router_kernel(o_ref, w_ref, x_ref, g_ref, rwh_ref, rwl_ref, rb_ref,
                       xn_ref, hm_ref, idx_ref, wts_ref, rank_ref, cnt_ref, carry_scr):
    i = pl.program_id(0)
    tm = x_ref.shape[0]

    @pl.when(i == 0)
    def _():
        carry_scr[...] = jnp.zeros_like(carry_scr)

    xn = x_ref[...] + jnp.dot(o_ref[...], w_ref[...], preferred_element_type=F32)
    xn_ref[...] = xn
    hm = _rms(xn) * g_ref[...]
    hm_ref[...] = _pack_bf16_halves(hm)

    hm_hi = hm.astype(BF16)
    hm_lo = (hm - hm_hi.astype(F32)).astype(BF16)
    nt = (((1,), (1,)), ((), ()))
    logits = (lax.dot_general(rwh_ref[...], hm_hi, nt, preferred_element_type=F32)
              + lax.dot_general(rwh_ref[...], hm_lo, nt, preferred_element_type=F32)
              + lax.dot_general(rwl_ref[...], hm_hi, nt, preferred_element_type=F32)
              + rb_ref[...])

    eidx = lax.broadcasted_iota(I32, (N_EXPERTS, tm), 0)
    work = logits
    vals, sels, hots = [], [], []
    for _ in range(TOP_K):
        m = jnp.max(work, axis=0, keepdims=True)
        sel = jnp.min(jnp.where(work == m, eidx, N_EXPERTS), axis=0, keepdims=True)
        hot = eidx == sel
        vals.append(m)
        sels.append(sel)
        hots.append(hot)
        work = jnp.where(hot, -jnp.inf, work)
    exps = [jnp.exp(v - vals[0]) for v in vals]
    denom = exps[0] + exps[1] + exps[2] + exps[3]

    mask = jnp.where(hots[0] | hots[1] | hots[2] | hots[3], 1.0, 0.0)
    tri = (lax.broadcasted_iota(I32, (tm, tm), 0) < lax.broadcasted_iota(I32, (tm, tm), 1))
    excl = jnp.dot(mask.astype(BF16), jnp.where(tri, 1.0, 0.0).astype(BF16), preferred_element_type=F32)
    rank_e = excl + carry_scr[:, 0:1]
    for k in range(TOP_K):
        idx_ref[k:k + 1, :] = sels[k]
        wts_ref[k:k + 1, :] = exps[k] / denom
        rank_ref[k:k + 1, :] = jnp.sum(jnp.where(hots[k], rank_e, 0.0), axis=0, keepdims=True).astype(I32)
    carry_scr[...] = carry_scr[...] + jnp.sum(mask, axis=1, keepdims=True)
    cnt_ref[...] = carry_scr[...].astype(I32)


def _out_router(o, w_out_bf16, x, moe_gain, rw_hi_t, rw_lo_t, router_b, *, tm):
    t, d = x.shape
    dv = o.shape[1]
    tok = lambda i: (i, 0)
    fixed = lambda i: (0, 0)
    lanes_tok = lambda i: (0, i)
    return pl.pallas_call(
        _out_router_kernel,
        grid=(t // tm,),
        in_specs=[
            pl.BlockSpec((tm, dv), tok),
            pl.BlockSpec((dv, d), fixed),
            pl.BlockSpec((tm, d), tok),
            pl.BlockSpec((1, d), fixed),
            pl.BlockSpec((N_EXPERTS, d), fixed),
            pl.BlockSpec((N_EXPERTS, d), fixed),
            pl.BlockSpec((N_EXPERTS, 1), fixed),
        ],
        out_specs=[
            pl.BlockSpec((tm, d), tok),
            pl.BlockSpec((tm, d // 2), tok),
            pl.BlockSpec((TOP_K, tm), lanes_tok),
            pl.BlockSpec((TOP_K, tm), lanes_tok),
            pl.BlockSpec((TOP_K, tm), lanes_tok),
            pl.BlockSpec((N_EXPERTS, LANES), fixed),
        ],
        out_shape=[
            jax.ShapeDtypeStruct((t, d), F32),
            jax.ShapeDtypeStruct((t, d // 2), I32),
            jax.ShapeDtypeStruct((TOP_K, t), I32),
            jax.ShapeDtypeStruct((TOP_K, t), F32),
            jax.ShapeDtypeStruct((TOP_K, t), I32),
            jax.ShapeDtypeStruct((N_EXPERTS, LANES), I32),
        ],
        scratch_shapes=[pltpu.VMEM((N_EXPERTS, LANES), F32)],
        compiler_params=_params(("arbitrary",)),
        name="out_router",
    )(o, w_out_bf16, x, moe_gain.reshape(1, d), rw_hi_t, rw_lo_t, router_b.reshape(N_EXPERTS, 1))


SUBLANES = 8


def _idx_slot_copy(dest_hbm, idx_smem, idx_sem, step, s):
    n_idx = dest_hbm.shape[1]
    return pltpu.make_async_copy(dest_hbm.at[step], idx_smem.at[pl.ds(pl.multiple_of(s * n_idx, n_idx), n_idx)],
                                 idx_sem.at[s])


def _dispatch_kernel(ends_ref, padded_ref, dest_hbm, hm_ref, xs_hbm, idx_smem, zero_scr, idx_sem, fill_sem, row_sem,
                     *, tm, tr, n_tiles):
    i = pl.program_id(0)
    n = pl.num_programs(0)
    slot = i % 2
    groups = tm // SUBLANES

    def idx_copy(step, s):
        return _idx_slot_copy(dest_hbm, idx_smem, idx_sem, step, s)

    @pl.when(i == 0)
    def _():
        idx_copy(0, 0).start()
        zero_scr[...] = jnp.zeros_like(zero_scr)

        def fill(start):
            return pltpu.make_async_copy(zero_scr, xs_hbm.at[pl.ds(pl.multiple_of(start, tr), tr), :], fill_sem)

        first_unused = ends_ref[N_EXPERTS - 1] // tr
        for e in range(N_EXPERTS):
            @pl.when(padded_ref[e] > 0)
            def _():
                fill(ends_ref[e] - tr).start()
        lax.fori_loop(first_unused, n_tiles, lambda c, _: (fill(c * tr).start(), 0)[1], 0)
        for e in range(N_EXPERTS):
            @pl.when(padded_ref[e] > 0)
            def _():
                fill(0).wait()
        lax.fori_loop(first_unused, n_tiles, lambda c, _: (fill(0).wait(), 0)[1], 0)

    @pl.when(i + 1 < n)
    def _():
        idx_copy(i + 1, 1 - slot).start()

    idx_copy(i, slot).wait()

    def scatter_rows(s):
        for g in range(groups):
            for u in range(SUBLANES):
                src = hm_ref.at[g, pl.ds(u, 1), :]
                for k in range(TOP_K):
                    row = idx_smem[s * TOP_K * tm + k * tm + g * SUBLANES + u]
                    pltpu.make_async_copy(src, xs_hbm.at[pl.ds(row, 1), :], row_sem).start(priority=k % 2)

    for s in range(2):
        @pl.when(slot == s)
        def _():
            scatter_rows(s)

    for _ in range(TOP_K):
        pltpu.make_async_copy(xs_hbm.at[pl.ds(0, tm), :], xs_hbm.at[pl.ds(0, tm), :], row_sem).wait()


def _dispatch(hm, dest_tiles, ends, padded, *, tm, tr, n_rows):
    t, d = hm.shape
    return pl.pallas_call(
        functools.partial(_dispatch_kernel, tm=tm, tr=tr, n_tiles=n_rows // tr),
        grid_spec=pltpu.PrefetchScalarGridSpec(
            num_scalar_prefetch=2,
            grid=(t // tm,),
            in_specs=[
                pl.BlockSpec(memory_space=pl.ANY),
                pl.BlockSpec((tm // SUBLANES, SUBLANES, d), lambda i, *_: (i, 0, 0)),
            ],
            out_specs=pl.BlockSpec(memory_space=pl.ANY),
            scratch_shapes=[
                pltpu.SMEM((2 * TOP_K * tm,), I32),
                pltpu.VMEM((tr, d), hm.dtype),
                pltpu.SemaphoreType.DMA((2,)),
                pltpu.SemaphoreType.DMA(()),
                pltpu.SemaphoreType.DMA(()),
            ],
        ),
        out_shape=jax.ShapeDtypeStruct((n_rows, d), hm.dtype),
        compiler_params=_params(("arbitrary",)),
        name="moe_dispatch",
    )(ends, padded, dest_tiles, hm.reshape(t // SUBLANES, SUBLANES, d))


MXU_WIDTH = 256


def _split_w1_kernel(w_ref, perm_ref, g_ref, l_ref):
    half = MXU_WIDTH // 2
    for c in range(w_ref.shape[1] // MXU_WIDTH):
        y = jnp.dot(w_ref[:, c * MXU_WIDTH:(c + 1) * MXU_WIDTH].astype(BF16), perm_ref[...],
                    preferred_element_type=F32)
        g_ref[:, c * half:(c + 1) * half] = y[:, :half].astype(BF16)
        l_ref[:, c * half:(c + 1) * half] = y[:, half:].astype(BF16)


def _split_w1(w1_layers, layer):
    _, e, d, ff2 = w1_layers.shape
    ff = ff2 // 2
    rows = min(512, d)
    src = jnp.arange(MXU_WIDTH)
    dst = jnp.where(src % 2 == 0, src // 2, MXU_WIDTH // 2 + src // 2)
    perm = (dst[:, None] == jnp.arange(MXU_WIDTH)[None, :]).astype(BF16)
    blk = lambda i, r: (i, r, 0)
    return pl.pallas_call(
        _split_w1_kernel,
        grid=(e, d // rows),
        in_specs=[pl.BlockSpec((None, None, rows, ff2), lambda i, r: (layer, i, r, 0)),
                  pl.BlockSpec((MXU_WIDTH, MXU_WIDTH), lambda i, r: (0, 0))],
        out_specs=[pl.BlockSpec((None, rows, ff), blk), pl.BlockSpec((None, rows, ff), blk)],
        out_shape=[jax.ShapeDtypeStruct((e, d, ff), BF16), jax.ShapeDtypeStruct((e, d, ff), BF16)],
        compiler_params=_params(("parallel", "parallel")),
        name="split_w1",
    )(w1_layers, perm)


def _experts_kernel(te_ref, tv_ref, xs_ref, w1g_ref, w1l_ref, b1g_ref, b1l_ref, w2_ref, b2_ref, ys_ref):
    i = pl.program_id(0)

    @pl.when(tv_ref[i] > 0)
    def _():
        x = _unpack_bf16_halves(xs_ref[...])
        glu = jnp.dot(x, w1g_ref[...], preferred_element_type=F32) + b1g_ref[...]
        lin = jnp.dot(x, w1l_ref[...], preferred_element_type=F32) + b1l_ref[...]
        glu = jnp.minimum(glu, SWIGLU_LIMIT)
        lin = jnp.clip(lin, -SWIGLU_LIMIT, SWIGLU_LIMIT)
        act = glu * jax.nn.sigmoid(SWIGLU_ALPHA * glu) * (lin + 1.0)
        ys_ref[...] = (jnp.dot(act.astype(BF16), w2_ref[...].astype(BF16), preferred_element_type=F32)
                       + b2_ref[...])

    @pl.when(tv_ref[i] == 0)
    def _():
        ys_ref[...] = jnp.zeros_like(ys_ref)


def _experts(xs, tile_expert, tile_valid, w1g, w1l, b1g, b1l, w2_layers, layer, b2, *, tr):
    n_rows = xs.shape[0]
    d = w1g.shape[1]
    ff = w1g.shape[2]
    n_tiles = n_rows // tr
    row_tile = lambda i, te, tv: (i, 0)
    by_expert = lambda i, te, tv: (te[i], 0, 0)
    return pl.pallas_call(
        _experts_kernel,
        grid_spec=pltpu.PrefetchScalarGridSpec(
            num_scalar_prefetch=2,
            grid=(n_tiles,),
            in_specs=[
                pl.BlockSpec((tr, d // 2), row_tile),
                pl.BlockSpec((None, d, ff), by_expert),
                pl.BlockSpec((None, d, ff), by_expert),
                pl.BlockSpec((None, 1, ff), by_expert),
                pl.BlockSpec((None, 1, ff), by_expert),
                pl.BlockSpec((None, None, ff, d), lambda i, te, tv: (layer, te[i], 0, 0)),
                pl.BlockSpec((None, 1, d), by_expert),
            ],
            out_specs=pl.BlockSpec((tr, d), row_tile),
        ),
        out_shape=jax.ShapeDtypeStruct((n_rows, d), F32),
        compiler_params=_params(("arbitrary",)),
        name="moe_experts",
    )(tile_expert, tile_valid, xs, w1g, w1l, b1g, b1l, w2_layers, b2)


def _combine_kernel(dest_hbm, ys_hbm, x_ref, wt_ref, o_ref, idx_smem, rows_scr, idx_sem, row_sem, *, tm):
    i = pl.program_id(0)
    n = pl.num_programs(0) - 1
    groups = TOP_K * tm // SUBLANES
    per_k = tm // SUBLANES

    def idx_copy(step, s):
        return _idx_slot_copy(dest_hbm, idx_smem, idx_sem, step, s)

    def gather_rows(s):
        for g in range(groups):
            for u in range(SUBLANES):
                row = idx_smem[s * TOP_K * tm + g * SUBLANES + u]
                pltpu.make_async_copy(ys_hbm.at[pl.ds(row, 1), :], rows_scr.at[s, g, pl.ds(u, 1), :],
                                      row_sem.at[s]).start(priority=u % 2)

    def combine_rows(s):
        for k in range(TOP_K):
            part = rows_scr.at[s, pl.ds(k * per_k, per_k)]
            pltpu.make_async_copy(part, part, row_sem.at[s]).wait()
        out = x_ref[...]
        for k in range(TOP_K):
            rows = rows_scr[s, k * per_k:(k + 1) * per_k].reshape(tm, x_ref.shape[1])
            out = out + wt_ref[:, k:k + 1] * rows
        o_ref[...] = out

    @pl.when(i == 0)
    def _():
        idx_copy(0, 0).start()

    for s in range(2):
        @pl.when((i < n) & (i % 2 == s))
        def _():
            idx_copy(i, s).wait()

            @pl.when(i + 1 < n)
            def _():
                idx_copy(i + 1, 1 - s).start()

            gather_rows(s)

    for s in range(2):
        @pl.when((i >= 1) & ((i - 1) % 2 == s))
        def _():
            combine_rows(s)


def _combine(ys, dest_tiles, x, wts_t, *, tm):
    t, d = x.shape
    prev = lambda i: (jnp.maximum(i - 1, 0), 0)
    return pl.pallas_call(
        functools.partial(_combine_kernel, tm=tm),
        grid=(t // tm + 1,),
        in_specs=[
            pl.BlockSpec(memory_space=pl.ANY),
            pl.BlockSpec(memory_space=pl.ANY),
            pl.BlockSpec((tm, d), prev),
            pl.BlockSpec((tm, TOP_K), prev),
        ],
        out_specs=pl.BlockSpec((tm, d), prev),
        out_shape=jax.ShapeDtypeStruct((t, d), F32),
        scratch_shapes=[
            pltpu.SMEM((2 * TOP_K * tm,), I32),
            pltpu.VMEM((2, TOP_K * tm // SUBLANES, SUBLANES, d), F32),
            pltpu.SemaphoreType.DMA((2,)),
            pltpu.SemaphoreType.DMA((2,)),
        ],
        compiler_params=_params(("arbitrary",)),
        name="moe_combine",
    )(dest_tiles, ys, x, wts_t)


def _tiles(batch, seq):
    t = batch * seq
    tm = min(512, t)
    return dict(
        tm=tm,
        tn=256,
        tq=min(512, seq),
        tk=min(512, seq),
        chunk=min(256, seq),
        tmd=min(256, t),
        tr=min(512, max(8, TOP_K * t // N_EXPERTS)),
    )


def _moe(o, w_out, x, moe_gain, router_w, router_b, w1_layers, b1, w2_layers, b2, layer, *, cfg):
    t, d = x.shape
    tm, tmd, tr = cfg["tm"], cfg["tmd"], cfg["tr"]
    rw_t = router_w.T
    rw_hi = rw_t.astype(BF16)
    rw_lo = (rw_t - rw_hi.astype(F32)).astype(BF16)
    x, hm, idx, wts, rank, cnt = _out_router(o, w_out.astype(BF16), x, moe_gain, rw_hi, rw_lo, router_b, tm=tm)

    counts = cnt[:, 0]
    padded = ((counts + tr - 1) // tr) * tr
    ends = jnp.cumsum(padded).astype(I32)
    offs = ends - padded
    experts = jnp.arange(N_EXPERTS, dtype=I32)[:, None, None]
    dest = rank + jnp.sum(jnp.where(idx[None] == experts, offs[:, None, None], 0), axis=0)
    dest_tiles = dest.reshape(TOP_K, t // tmd, tmd).transpose(1, 0, 2).reshape(t // tmd, TOP_K * tmd)
    n_rows = TOP_K * t + N_EXPERTS * tr
    tile_start = jnp.arange(n_rows // tr, dtype=I32) * tr
    tile_expert = jnp.minimum(jnp.sum((tile_start[:, None] >= ends[None, :]).astype(I32), axis=1), N_EXPERTS - 1)
    tile_valid = (tile_start < ends[-1]).astype(I32)

    xs = _dispatch(hm, dest_tiles, ends, padded.astype(I32), tm=tmd, tr=tr, n_rows=n_rows)
    ff = w2_layers.shape[2]
    w1g, w1l = _split_w1(w1_layers, layer)
    ys = _experts(xs, tile_expert, tile_valid, w1g, w1l,
                  b1[:, 0::2].reshape(N_EXPERTS, 1, ff), b1[:, 1::2].reshape(N_EXPERTS, 1, ff),
                  w2_layers, layer, b2.reshape(N_EXPERTS, 1, d), tr=tr)
    return _combine(ys, dest_tiles, x, wts.T, tm=tmd)


def kernel(x, diff_norm, diff_w_in, diff_w_out, diff_lambda_q1, diff_lambda_k1, diff_lambda_q2, diff_lambda_k2, diff_q_norm, diff_k_norm, diff_subln, ret_norm, ret_w_in, ret_w_out, ret_log_decay_fwd, ret_log_decay_bwd, moe_norm, moe_router_w, moe_router_b, moe_w1, moe_b1, moe_w2, moe_b2):
    batch, seq, d = x.shape
    depth = moe_norm.shape[0]
    cfg = _tiles(batch, seq)
    x = x.reshape(batch * seq, d)
    slopes = jnp.exp2(-8.0 * jnp.arange(1, DIFF_HEADS + 1, dtype=F32) / DIFF_HEADS)
    for i in range(depth):
        j = i // N_MIXERS
        if i % N_MIXERS == 0:
            lambda_init = _lambda_init(i)
            lam = (jnp.exp(jnp.sum(diff_lambda_q1[j] * diff_lambda_k1[j]))
                   - jnp.exp(jnp.sum(diff_lambda_q2[j] * diff_lambda_k2[j])) + lambda_init)
            reps = d // DIFF_HEAD_DIM
            qk_gain = jnp.concatenate([jnp.tile(diff_q_norm[j], reps) * (DIFF_HEAD_DIM ** -0.5 * LOG2E),
                                       jnp.tile(diff_k_norm[j], reps)]).reshape(1, 2 * d)
            qk, vt = _in_proj(x, diff_norm[j], diff_w_in[j].astype(BF16), tm=cfg["tm"], tn=cfg["tn"],
                              qk_gain_row=qk_gain, values_layout=(batch, seq))
            o = _diff_attention(qk, vt, slopes, lam, diff_subln[j], lambda_init,
                                batch=batch, seq=seq, tq=cfg["tq"], tk=cfg["tk"])
            w_out = diff_w_out[j]
        else:
            dk = d // RET_HEADS
            col_scale = jnp.ones((ret_w_in.shape[2],), F32).at[d:2 * d].set(dk ** -0.5)
            proj, = _in_proj(x, ret_norm[j], (ret_w_in[j] * col_scale).astype(BF16), tm=cfg["tm"], tn=cfg["tn"])
            o = _retention(proj, ret_log_decay_fwd[j], ret_log_decay_bwd[j],
                           batch=batch, seq=seq, d_model=d, chunk=cfg["chunk"])
            w_out = ret_w_out[j]
        x = _moe(o, w_out, x, moe_norm[i], moe_router_w[i], moe_router_b[i],
                 moe_w1, moe_b1[i], moe_w2, moe_b2[i], i, cfg=cfg)
    return x.reshape(batch, seq, d)
```

```python
import functools
import math

import jax
import jax.numpy as jnp
from jax import lax
from jax.experimental import pallas as pl
from jax.experimental.pallas import tpu as pltpu

F32 = jnp.float32
BF16 = jnp.bfloat16
I32 = jnp.int32

NORM_EPS = 1e-5
N_MIXERS = 2

DIFF_HEADS = 8
DIFF_HEAD_DIM = 64
DIFF_V_DIM = 2 * DIFF_HEAD_DIM

RET_HEADS = 4

N_EXPERTS = 32
TOP_K = 4
SWIGLU_ALPHA = 1.702
SWIGLU_LIMIT = 7.0

V7X_VMEM_LIMIT_BYTES = 56 * 1024 * 1024
LANES = 128


def _lambda_init(layer_idx):
    return 0.8 - 0.6 * math.exp(-0.3 * layer_idx)


def _params(semantics):
    return pltpu.CompilerParams(dimension_semantics=semantics, vmem_limit_bytes=V7X_VMEM_LIMIT_BYTES)


def _rms(x, eps=NORM_EPS):
    return x * lax.rsqrt(jnp.mean(x * x, axis=-1, keepdims=True) + eps)


def _in_proj_kernel(x_ref, g_ref, w_ref, gmat_ref, qkg_ref, o_ref, *rest, tn, n_normed, n_transposed):
    vt_ref, h_scr = rest if n_transposed else (None, rest[0])
    h_scr[...] = (_rms(x_ref[...]) * g_ref[...]).astype(BF16)
    n_total = w_ref.shape[1]
    n_chunks = n_total // tn
    n_rowmajor = n_total - n_transposed
    if n_transposed:
        vt_ref[:, DIFF_V_DIM:, :] = jnp.ones((vt_ref.shape[0], V_ONES_ROWS, vt_ref.shape[2]), BF16)

    def project(c):
        return jnp.dot(h_scr[...], w_ref[:, c * tn:(c + 1) * tn], preferred_element_type=F32)

    def finish(c, acc):
        cols = slice(c * tn, (c + 1) * tn)
        if c * tn < n_normed:
            sq = acc * acc
            hi = sq.astype(BF16)
            lo = (sq - hi.astype(F32)).astype(BF16)
            msq = (jnp.dot(hi, gmat_ref[...], preferred_element_type=F32)
                   + jnp.dot(lo, gmat_ref[...], preferred_element_type=F32))
            acc = acc * lax.rsqrt(msq + NORM_EPS) * qkg_ref[:, cols]
        if c * tn < n_rowmajor:
            o_ref[:, cols] = acc.astype(BF16)
        else:
            for part in range(tn // DIFF_V_DIM):
                head = (c * tn - n_rowmajor) // DIFF_V_DIM + part
                vt_ref[head, :DIFF_V_DIM, :] = acc[:, part * DIFF_V_DIM:(part + 1) * DIFF_V_DIM].T.astype(BF16)

    acc = project(0)
    for c in range(1, n_chunks):
        nxt = project(c)
        finish(c - 1, acc)
        acc = nxt
    finish(n_chunks - 1, acc)


def _in_proj(x, gain, w_bf16, *, tm, tn, qk_gain_row=None, values_layout=None):
    t, d = x.shape
    n = w_bf16.shape[1]
    if qk_gain_row is None:
        n_normed = 0
        qk_gain_row = jnp.zeros((1, n), F32)
    else:
        n_normed = qk_gain_row.shape[1]
        qk_gain_row = jnp.pad(qk_gain_row, ((0, 0), (0, n - n_normed)))
    grp = jnp.arange(tn) // DIFF_HEAD_DIM
    gmat = jnp.where(grp[:, None] == grp[None, :], 1.0 / DIFF_HEAD_DIM, 0.0).astype(BF16)
    n_transposed = DIFF_HEADS * DIFF_V_DIM if values_layout else 0
    n_rowmajor = n - n_transposed
    out_specs = [pl.BlockSpec((tm, n_rowmajor), lambda i: (i, 0))]
    out_shape = [jax.ShapeDtypeStruct((t, n_rowmajor), BF16)]
    if values_layout:
        batch, seq = values_layout
        per_seq = seq // tm
        rows = DIFF_V_DIM + V_ONES_ROWS
        out_specs.append(pl.BlockSpec((None, DIFF_HEADS, rows, tm), lambda i: (i // per_seq, 0, 0, i % per_seq)))
        out_shape.append(jax.ShapeDtypeStruct((batch, DIFF_HEADS, rows, seq), BF16))
    return pl.pallas_call(
        functools.partial(_in_proj_kernel, tn=tn, n_normed=n_normed, n_transposed=n_transposed),
        grid=(t // tm,),
        in_specs=[
            pl.BlockSpec((tm, d), lambda i: (i, 0)),
            pl.BlockSpec((1, d), lambda i: (0, 0)),
            pl.BlockSpec((d, n), lambda i: (0, 0)),
            pl.BlockSpec((tn, tn), lambda i: (0, 0)),
            pl.BlockSpec((1, n), lambda i: (0, 0)),
        ],
        out_specs=out_specs,
        out_shape=out_shape,
        scratch_shapes=[pltpu.VMEM((tm, d), BF16)],
        compiler_params=_params(("parallel",)),
        name="in_proj",
    )(x, gain.reshape(1, d), w_bf16, gmat, qk_gain_row)


LOG2E = math.log2(math.e)
LOG2E_BF16_PARTS = (1.4453125, -0.00262451171875, 7.063150405883789e-06)
ALIBI_LOW_BITS = 255
ALIBI_HIGH_BIT = 256
V_ONES_ROWS = 16


def _alibi_key_features(slopes, seq, tk):
    dj = jnp.arange(seq, dtype=I32) % tk
    lo = (dj & ALIBI_LOW_BITS).astype(F32)
    hi = (dj & ALIBI_HIGH_BIT).astype(F32)
    lane = jnp.arange(LANES)
    feat = jnp.where(lane[None, :] < 3, lo[:, None], jnp.where(lane[None, :] < 6, hi[:, None], 0.0))
    return (slopes[:, None, None] * feat[None]).astype(BF16)


def _diff_attn_kernel(slopes_ref, lam_ref, q_ref, k_ref, vt_ref, fk_ref, sg_ref, o_ref,
                      qa_scr, s0_scr, s1_scr, mt0_scr, mt1_scr, p0_scr, p1_scr, a0_scr, a1_scr,
                      m_scr, acc_scr, *, tq, tk, out_scale):
    h = pl.program_id(1)
    qi = pl.program_id(2)
    slope2 = slopes_ref[h] * LOG2E
    lam = lam_ref[0]
    nk = k_ref.shape[0] // tk
    i0 = qi * tq
    jd = i0 // tk
    cols2 = 2 * tq

    q = q_ref[...]
    lane = lax.broadcasted_iota(I32, q.shape, 1)
    zero = jnp.zeros_like(q)
    qq = jnp.concatenate([jnp.where(lane < DIFF_HEAD_DIM, q, zero),
                          jnp.where(lane >= DIFF_HEAD_DIM, q, zero)], axis=0)
    flane = lax.broadcasted_iota(I32, (cols2, LANES), 1)
    a, b, c = LOG2E_BF16_PARTS
    feat = jnp.where(flane % 3 == 0, a, jnp.where(flane % 3 == 1, b, c))
    feat = jnp.where(flane < 6, feat, 0.0)
    qa_scr[0] = jnp.concatenate([qq, feat.astype(BF16)], axis=1)
    qa_scr[1] = jnp.concatenate([qq, (-feat).astype(BF16)], axis=1)
    col = lax.broadcasted_iota(I32, (1, cols2), 1)
    q_pos = (i0 + jnp.where(col >= tq, col - tq, col)).astype(F32)

    m_scr[...] = jnp.full_like(m_scr, -jnp.inf)
    acc_scr[...] = jnp.zeros_like(acc_scr)
    nt = (((1,), (1,)), ((), ()))

    s_scr, mt_scr, p_scr, a_scr = (s0_scr, s1_scr), (mt0_scr, mt1_scr), (p0_scr, p1_scr), (a0_scr, a1_scr)

    def key_rows(j):
        return pl.ds(pl.multiple_of(j * tk, tk), tk)

    def exponentials(slot, col_shift):
        for cb in range(cols2 // LANES):
            cs = slice(cb * LANES, (cb + 1) * LANES)
            shift = col_shift[:, cs]
            m_old = m_scr[:, cs]
            m_new = jnp.maximum(m_old, mt_scr[slot][:, cs] - shift)
            alpha = jnp.exp2(m_old - m_new)
            p = jnp.exp2((s_scr[slot][:, cs] - (m_new + shift)).astype(BF16))
            m_scr[:, cs] = m_new
            a_scr[slot][:, cs] = alpha
            p_scr[slot][:, cs] = p

    def values(slot, j):
        acc_scr[...] = (a_scr[slot][...] * acc_scr[...]
                        + jnp.dot(vt_ref[:, key_rows(j)], p_scr[slot][...], preferred_element_type=F32))

    s = lax.dot_general(k_ref[key_rows(jd), :], qq, nt, preferred_element_type=F32)
    key_pos = (jd * tk + lax.broadcasted_iota(I32, (tk, 1), 0)).astype(F32)
    s = s - slope2 * jnp.abs(key_pos - q_pos)
    s_scr[1][...] = s
    mt_scr[1][...] = jnp.max(s, axis=0, keepdims=True)

    n_lin = nk - 1

    def tile_of(t):
        return t + (t >= jd).astype(I32)

    def scores(t, slot):
        j = tile_of(t)
        k_aug = jnp.concatenate([k_ref[key_rows(j), :], fk_ref[key_rows(j), :]], axis=1)
        s = lax.dot_general(k_aug, qa_scr[(j > jd).astype(I32)], nt, preferred_element_type=F32)
        s_scr[slot][...] = s
        mt_scr[slot][...] = jnp.max(s, axis=0, keepdims=True)

    def shift_of(t):
        j = tile_of(t)
        d = slope2 * (q_pos - (j * tk).astype(F32))
        return jnp.where(j > jd, -d, d)

    no_shift = jnp.zeros((1, cols2), F32)
    if n_lin == 0:
        exponentials(1, no_shift)
        values(1, jd)
    else:
        scores(0, 0)
        exponentials(1, no_shift)

        def stage(t, slot, with_scores=True):
            if with_scores:
                scores(t + 1, 1 - slot)
            values(1 - slot, jnp.where(t == 0, jd, tile_of(jnp.maximum(t - 1, 0))))
            exponentials(slot, shift_of(t))

        def pair(tt, _):
            stage(2 * tt, 0)
            stage(2 * tt + 1, 1)
            return 0

        lax.fori_loop(0, (n_lin - 1) // 2, pair, 0)
        if (n_lin - 1) % 2:
            stage(jnp.int32(n_lin - 2), (n_lin - 2) % 2)
        stage(jnp.int32(n_lin - 1), (n_lin - 1) % 2, with_scores=False)
        values((n_lin - 1) % 2, tile_of(jnp.int32(n_lin - 1)))

    o = acc_scr[:DIFF_V_DIM, :] / acc_scr[DIFF_V_DIM:DIFF_V_DIM + 1, :]
    o = o[:, :tq] - lam * o[:, tq:]
    o = o * lax.rsqrt(jnp.mean(o * o, axis=0, keepdims=True) + NORM_EPS)
    o_ref[...] = (o.T * sg_ref[...] * out_scale).astype(BF16)


def _diff_attention(qk, vt, slopes, lam, subln_gain, lambda_init, *, batch, seq, tq, tk):
    assert tk % tq == 0 and tk <= 2 * ALIBI_HIGH_BIT and tq % LANES == 0
    d = DIFF_HEADS * DIFF_V_DIM
    qkv3 = qk.reshape(batch, seq, 2 * d)
    fk = _alibi_key_features(slopes, seq, tk)
    out = pl.pallas_call(
        functools.partial(_diff_attn_kernel, tq=tq, tk=tk, out_scale=1.0 - lambda_init),
        grid=(batch, DIFF_HEADS, seq // tq),
        in_specs=[
            pl.BlockSpec(memory_space=pltpu.SMEM),
            pl.BlockSpec(memory_space=pltpu.SMEM),
            pl.BlockSpec((None, tq, DIFF_V_DIM), lambda b, h, i: (b, i, h)),
            pl.BlockSpec((None, seq, DIFF_V_DIM), lambda b, h, i: (b, 0, DIFF_HEADS + h)),
            pl.BlockSpec((None, None, DIFF_V_DIM + V_ONES_ROWS, seq), lambda b, h, i: (b, h, 0, 0)),
            pl.BlockSpec((None, seq, LANES), lambda b, h, i: (h, 0, 0)),
            pl.BlockSpec((1, DIFF_V_DIM), lambda b, h, i: (0, 0)),
        ],
        out_specs=pl.BlockSpec((None, tq, DIFF_V_DIM), lambda b, h, i: (b, i, h)),
        out_shape=jax.ShapeDtypeStruct((batch, seq, d), BF16),
        scratch_shapes=[
            pltpu.VMEM((2, 2 * tq, 2 * LANES), BF16),
            pltpu.VMEM((tk, 2 * tq), F32), pltpu.VMEM((tk, 2 * tq), F32),
            pltpu.VMEM((1, 2 * tq), F32), pltpu.VMEM((1, 2 * tq), F32),
            pltpu.VMEM((tk, 2 * tq), BF16), pltpu.VMEM((tk, 2 * tq), BF16),
            pltpu.VMEM((1, 2 * tq), F32), pltpu.VMEM((1, 2 * tq), F32),
            pltpu.VMEM((1, 2 * tq), F32),
            pltpu.VMEM((DIFF_V_DIM + V_ONES_ROWS, 2 * tq), F32),
        ],
        compiler_params=_params(("parallel", "parallel", "parallel")),
        name="diff_attn",
    )(slopes, lam.reshape(1), qkv3, qkv3, vt, fk, subln_gain.reshape(1, DIFF_V_DIM))
    return out.reshape(batch * seq, d)


def _retention_kernel(lgf_ref, lgb_ref, q_ref, k_ref, v_ref, g_ref, o_ref, ob_scr, st_scr, *, chunk):
    h = pl.program_id(1)
    lgf = lgf_ref[h]
    lgb = lgb_ref[h]
    seq = q_ref.shape[0]
    nc = seq // chunk
    c_f = float(chunk)

    row = lax.broadcasted_iota(I32, (chunk, chunk), 0)
    col = lax.broadcasted_iota(I32, (chunk, chunk), 1)
    rel = (row - col).astype(F32)
    d_both = (jnp.where(rel >= 0, jnp.exp(lgf * jnp.maximum(rel, 0.0)), 0.0)
              + jnp.where(rel < 0, jnp.exp(lgb * jnp.maximum(-rel, 0.0)), 0.0))
    idx = lax.broadcasted_iota(I32, (chunk, 1), 0).astype(F32)
    xi_f = jnp.exp(lgf * (idx + 1.0))
    zeta_f = jnp.exp(lgf * (c_f - 1.0 - idx))
    xi_b = jnp.exp(lgb * (c_f - idx))
    zeta_b = jnp.exp(lgb * idx)
    decay_f = jnp.exp(lgf * c_f)
    decay_b = jnp.exp(lgb * c_f)

    def state_update(k, v, zeta, decay):
        kz_t = (k.astype(F32) * zeta).T.astype(BF16)
        upd = jnp.dot(kz_t, v, preferred_element_type=F32)
        st_scr[...] = st_scr[...] * decay + upd

    st_scr[...] = jnp.zeros_like(st_scr)

    def bwd_body(i, _):
        c = nc - 1 - i
        rows = pl.ds(pl.multiple_of(c * chunk, chunk), chunk)
        q = q_ref[rows, :]
        inter = jnp.dot(q, st_scr[...].astype(BF16), preferred_element_type=F32)
        ob_scr[rows, :] = inter * xi_b
        state_update(k_ref[rows, :], v_ref[rows, :], zeta_b, decay_b)
        return 0

    lax.fori_loop(0, nc, bwd_body, 0, unroll=2)

    st_scr[...] = jnp.zeros_like(st_scr)

    def fwd_body(c, _):
        rows = pl.ds(pl.multiple_of(c * chunk, chunk), chunk)
        q = q_ref[rows, :]
        k = k_ref[rows, :]
        v = v_ref[rows, :]
        scores = lax.dot_general(q, k, (((1,), (1,)), ((), ())), preferred_element_type=F32) * d_both
        o = jnp.dot(scores.astype(BF16), v, preferred_element_type=F32)
        o = o + jnp.dot(q, st_scr[...].astype(BF16), preferred_element_type=F32) * xi_f
        o = o + ob_scr[rows, :]
        state_update(k, v, zeta_f, decay_f)
        g = g_ref[rows, :].astype(F32)
        o_ref[rows, :] = (g * jax.nn.sigmoid(g) * _rms(o)).astype(BF16)
        return 0

    lax.fori_loop(0, nc, fwd_body, 0, unroll=2)


def _retention(proj, lg_fwd, lg_bwd, *, batch, seq, d_model, chunk):
    dk = d_model // RET_HEADS
    dv = 2 * dk
    vw = RET_HEADS * dv
    p3 = proj.reshape(batch, seq, 2 * d_model + 2 * vw)
    out = pl.pallas_call(
        functools.partial(_retention_kernel, chunk=chunk),
        grid=(batch, RET_HEADS),
        in_specs=[
            pl.BlockSpec(memory_space=pltpu.SMEM),
            pl.BlockSpec(memory_space=pltpu.SMEM),
            pl.BlockSpec((None, seq, dk), lambda b, h: (b, 0, h)),
            pl.BlockSpec((None, seq, dk), lambda b, h: (b, 0, RET_HEADS + h)),
            pl.BlockSpec((None, seq, dv), lambda b, h: (b, 0, RET_HEADS + h)),
            pl.BlockSpec((None, seq, dv), lambda b, h: (b, 0, 2 * RET_HEADS + h)),
        ],
        out_specs=pl.BlockSpec((None, seq, dv), lambda b, h: (b, 0, h)),
        out_shape=jax.ShapeDtypeStruct((batch, seq, vw), BF16),
        scratch_shapes=[pltpu.VMEM((seq, dv), F32), pltpu.VMEM((dk, dv), F32)],
        compiler_params=_params(("parallel", "parallel")),
        name="retention",
    )(lg_fwd, lg_bwd, p3, p3, p3, p3)
    return out.reshape(batch * seq, vw)


def _out_router_kernel(o_ref, w_ref, x_ref, g_ref, rwh_ref, rwl_ref, rb_ref,
                       xn_ref, hm_ref, idx_ref, wts_ref, rank_ref, cnt_ref, carry_scr):
    i = pl.program_id(0)
    tm = x_ref.shape[0]

    @pl.when(i == 0)
    def _():
        carry_scr[...] = jnp.zeros_like(carry_scr)

    xn = x_ref[...] + jnp.dot(o_ref[...], w_ref[...], preferred_element_type=F32)
    xn_ref[...] = xn
    hm = _rms(xn) * g_ref[...]
    hm_ref[...] = hm

    hm_hi = hm.astype(BF16)
    hm_lo = (hm - hm_hi.astype(F32)).astype(BF16)
    nt = (((1,), (1,)), ((), ()))
    logits = (lax.dot_general(rwh_ref[...], hm_hi, nt, preferred_element_type=F32)
              + lax.dot_general(rwh_ref[...], hm_lo, nt, preferred_element_type=F32)
              + lax.dot_general(rwl_ref[...], hm_hi, nt, preferred_element_type=F32)
              + rb_ref[...])

    eidx = lax.broadcasted_iota(I32, (N_EXPERTS, tm), 0)
    work = logits
    vals, sels, hots = [], [], []
    for _ in range(TOP_K):
        m = jnp.max(work, axis=0, keepdims=True)
        sel = jnp.min(jnp.where(work == m, eidx, N_EXPERTS), axis=0, keepdims=True)
        hot = eidx == sel
        vals.append(m)
        sels.append(sel)
        hots.append(hot)
        work = jnp.where(hot, -jnp.inf, work)
    exps = [jnp.exp(v - vals[0]) for v in vals]
    denom = exps[0] + exps[1] + exps[2] + exps[3]

    mask = jnp.where(hots[0] | hots[1] | hots[2] | hots[3], 1.0, 0.0)
    tri = (lax.broadcasted_iota(I32, (tm, tm), 0) < lax.broadcasted_iota(I32, (tm, tm), 1))
    excl = jnp.dot(mask.astype(BF16), jnp.where(tri, 1.0, 0.0).astype(BF16), preferred_element_type=F32)
    rank_e = excl + carry_scr[:, 0:1]
    for k in range(TOP_K):
        idx_ref[k:k + 1, :] = sels[k]
        wts_ref[k:k + 1, :] = exps[k] / denom
        rank_ref[k:k + 1, :] = jnp.sum(jnp.where(hots[k], rank_e, 0.0), axis=0, keepdims=True).astype(I32)
    carry_scr[...] = carry_scr[...] + jnp.sum(mask, axis=1, keepdims=True)
    cnt_ref[...] = carry_scr[...].astype(I32)


def _out_router(o, w_out_bf16, x, moe_gain, rw_hi_t, rw_lo_t, router_b, *, tm):
    t, d = x.shape
    dv = o.shape[1]
    tok = lambda i: (i, 0)
    fixed = lambda i: (0, 0)
    lanes_tok = lambda i: (0, i)
    return pl.pallas_call(
        _out_router_kernel,
        grid=(t // tm,),
        in_specs=[
            pl.BlockSpec((tm, dv), tok),
            pl.BlockSpec((dv, d), fixed),
            pl.BlockSpec((tm, d), tok),
            pl.BlockSpec((1, d), fixed),
            pl.BlockSpec((N_EXPERTS, d), fixed),
            pl.BlockSpec((N_EXPERTS, d), fixed),
            pl.BlockSpec((N_EXPERTS, 1), fixed),
        ],
        out_specs=[
            pl.BlockSpec((tm, d), tok),
            pl.BlockSpec((tm, d), tok),
            pl.BlockSpec((TOP_K, tm), lanes_tok),
            pl.BlockSpec((TOP_K, tm), lanes_tok),
            pl.BlockSpec((TOP_K, tm), lanes_tok),
            pl.BlockSpec((N_EXPERTS, LANES), fixed),
        ],
        out_shape=[
            jax.ShapeDtypeStruct((t, d), F32),
            jax.ShapeDtypeStruct((t, d), F32),
            jax.ShapeDtypeStruct((TOP_K, t), I32),
            jax.ShapeDtypeStruct((TOP_K, t), F32),
            jax.ShapeDtypeStruct((TOP_K, t), I32),
            jax.ShapeDtypeStruct((N_EXPERTS, LANES), I32),
        ],
        scratch_shapes=[pltpu.VMEM((N_EXPERTS, LANES), F32)],
        compiler_params=_params(("arbitrary",)),
        name="out_router",
    )(o, w_out_bf16, x, moe_gain.reshape(1, d), rw_hi_t, rw_lo_t, router_b.reshape(N_EXPERTS, 1))


SUBLANES = 8


def _idx_slot_copy(dest_hbm, idx_smem, idx_sem, step, s):
    n_idx = dest_hbm.shape[1]
    return pltpu.make_async_copy(dest_hbm.at[step], idx_smem.at[pl.ds(pl.multiple_of(s * n_idx, n_idx), n_idx)],
                                 idx_sem.at[s])


def _dispatch_kernel(ends_ref, padded_ref, dest_hbm, hm_ref, xs_hbm, idx_smem, zero_scr, idx_sem, fill_sem, row_sem,
                     *, tm, tr, n_tiles):
    i = pl.program_id(0)
    n = pl.num_programs(0)
    slot = i % 2
    groups = tm // SUBLANES

    def idx_copy(step, s):
        return _idx_slot_copy(dest_hbm, idx_smem, idx_sem, step, s)

    @pl.when(i == 0)
    def _():
        idx_copy(0, 0).start()
        zero_scr[...] = jnp.zeros_like(zero_scr)

        def fill(start):
            return pltpu.make_async_copy(zero_scr, xs_hbm.at[pl.ds(pl.multiple_of(start, tr), tr), :], fill_sem)

        first_unused = ends_ref[N_EXPERTS - 1] // tr
        for e in range(N_EXPERTS):
            @pl.when(padded_ref[e] > 0)
            def _():
                fill(ends_ref[e] - tr).start()
        lax.fori_loop(first_unused, n_tiles, lambda c, _: (fill(c * tr).start(), 0)[1], 0)
        for e in range(N_EXPERTS):
            @pl.when(padded_ref[e] > 0)
            def _():
                fill(0).wait()
        lax.fori_loop(first_unused, n_tiles, lambda c, _: (fill(0).wait(), 0)[1], 0)

    @pl.when(i + 1 < n)
    def _():
        idx_copy(i + 1, 1 - slot).start()

    idx_copy(i, slot).wait()

    def scatter_rows(s):
        for g in range(groups):
            for u in range(SUBLANES):
                src = hm_ref.at[g, pl.ds(u, 1), :]
                for k in range(TOP_K):
                    row = idx_smem[s * TOP_K * tm + k * tm + g * SUBLANES + u]
                    pltpu.make_async_copy(src, xs_hbm.at[pl.ds(row, 1), :], row_sem).start(priority=k % 2)

    for s in range(2):
        @pl.when(slot == s)
        def _():
            scatter_rows(s)

    for _ in range(TOP_K):
        pltpu.make_async_copy(xs_hbm.at[pl.ds(0, tm), :], xs_hbm.at[pl.ds(0, tm), :], row_sem).wait()


def _dispatch(hm, dest_tiles, ends, padded, *, tm, tr, n_rows):
    t, d = hm.shape
    return pl.pallas_call(
        functools.partial(_dispatch_kernel, tm=tm, tr=tr, n_tiles=n_rows // tr),
        grid_spec=pltpu.PrefetchScalarGridSpec(
            num_scalar_prefetch=2,
            grid=(t // tm,),
            in_specs=[
                pl.BlockSpec(memory_space=pl.ANY),
                pl.BlockSpec((tm // SUBLANES, SUBLANES, d), lambda i, *_: (i, 0, 0)),
            ],
            out_specs=pl.BlockSpec(memory_space=pl.ANY),
            scratch_shapes=[
                pltpu.SMEM((2 * TOP_K * tm,), I32),
                pltpu.VMEM((tr, d), F32),
                pltpu.SemaphoreType.DMA((2,)),
                pltpu.SemaphoreType.DMA(()),
                pltpu.SemaphoreType.DMA(()),
            ],
        ),
        out_shape=jax.ShapeDtypeStruct((n_rows, d), F32),
        compiler_params=_params(("arbitrary",)),
        name="moe_dispatch",
    )(ends, padded, dest_tiles, hm.reshape(t // SUBLANES, SUBLANES, d))


MXU_WIDTH = 256


def _split_w1_kernel(w_ref, perm_ref, g_ref, l_ref):
    half = MXU_WIDTH // 2
    for c in range(w_ref.shape[1] // MXU_WIDTH):
        y = jnp.dot(w_ref[:, c * MXU_WIDTH:(c + 1) * MXU_WIDTH].astype(BF16), perm_ref[...],
                    preferred_element_type=F32)
        g_ref[:, c * half:(c + 1) * half] = y[:, :half].astype(BF16)
        l_ref[:, c * half:(c + 1) * half] = y[:, half:].astype(BF16)


def _split_w1(w1_layers, layer):
    _, e, d, ff2 = w1_layers.shape
    ff = ff2 // 2
    rows = min(512, d)
    src = jnp.arange(MXU_WIDTH)
    dst = jnp.where(src % 2 == 0, src // 2, MXU_WIDTH // 2 + src // 2)
    perm = (dst[:, None] == jnp.arange(MXU_WIDTH)[None, :]).astype(BF16)
    blk = lambda i, r: (i, r, 0)
    return pl.pallas_call(
        _split_w1_kernel,
        grid=(e, d // rows),
        in_specs=[pl.BlockSpec((None, None, rows, ff2), lambda i, r: (layer, i, r, 0)),
                  pl.BlockSpec((MXU_WIDTH, MXU_WIDTH), lambda i, r: (0, 0))],
        out_specs=[pl.BlockSpec((None, rows, ff), blk), pl.BlockSpec((None, rows, ff), blk)],
        out_shape=[jax.ShapeDtypeStruct((e, d, ff), BF16), jax.ShapeDtypeStruct((e, d, ff), BF16)],
        compiler_params=_params(("parallel", "parallel")),
        name="split_w1",
    )(w1_layers, perm)


def _experts_kernel(te_ref, tv_ref, xs_ref, w1g_ref, w1l_ref, b1g_ref, b1l_ref, w2_ref, b2_ref, ys_ref):
    i = pl.program_id(0)

    @pl.when(tv_ref[i] > 0)
    def _():
        x = xs_ref[...].astype(BF16)
        glu = jnp.dot(x, w1g_ref[...], preferred_element_type=F32) + b1g_ref[...]
        lin = jnp.dot(x, w1l_ref[...], preferred_element_type=F32) + b1l_ref[...]
        glu = jnp.minimum(glu, SWIGLU_LIMIT)
        lin = jnp.clip(lin, -SWIGLU_LIMIT, SWIGLU_LIMIT)
        act = glu * jax.nn.sigmoid(SWIGLU_ALPHA * glu) * (lin + 1.0)
        ys_ref[...] = (jnp.dot(act.astype(BF16), w2_ref[...].astype(BF16), preferred_element_type=F32)
                       + b2_ref[...])

    @pl.when(tv_ref[i] == 0)
    def _():
        ys_ref[...] = jnp.zeros_like(ys_ref)


def _experts(xs, tile_expert, tile_valid, w1g, w1l, b1g, b1l, w2_layers, layer, b2, *, tr):
    n_rows, d = xs.shape
    ff = w1g.shape[2]
    n_tiles = n_rows // tr
    row_tile = lambda i, te, tv: (i, 0)
    by_expert = lambda i, te, tv: (te[i], 0, 0)
    return pl.pallas_call(
        _experts_kernel,
        grid_spec=pltpu.PrefetchScalarGridSpec(
            num_scalar_prefetch=2,
            grid=(n_tiles,),
            in_specs=[
                pl.BlockSpec((tr, d), row_tile),
                pl.BlockSpec((None, d, ff), by_expert),
                pl.BlockSpec((None, d, ff), by_expert),
                pl.BlockSpec((None, 1, ff), by_expert),
                pl.BlockSpec((None, 1, ff), by_expert),
                pl.BlockSpec((None, None, ff, d), lambda i, te, tv: (layer, te[i], 0, 0)),
                pl.BlockSpec((None, 1, d), by_expert),
            ],
            out_specs=pl.BlockSpec((tr, d), row_tile),
        ),
        out_shape=jax.ShapeDtypeStruct((n_rows, d), F32),
        compiler_params=_params(("arbitrary",)),
        name="moe_experts",
    )(tile_expert, tile_valid, xs, w1g, w1l, b1g, b1l, w2_layers, b2)


def _combine_kernel(dest_hbm, ys_hbm, x_ref, wt_ref, o_ref, idx_smem, rows_scr, idx_sem, row_sem, *, tm):
    i = pl.program_id(0)
    n = pl.num_programs(0) - 1
    groups = TOP_K * tm // SUBLANES
    per_k = tm // SUBLANES

    def idx_copy(step, s):
        return _idx_slot_copy(dest_hbm, idx_smem, idx_sem, step, s)

    def gather_rows(s):
        for g in range(groups):
            for u in range(SUBLANES):
                row = idx_smem[s * TOP_K * tm + g * SUBLANES + u]
                pltpu.make_async_copy(ys_hbm.at[pl.ds(row, 1), :], rows_scr.at[s, g, pl.ds(u, 1), :],
                                      row_sem.at[s]).start(priority=u % 2)

    def combine_rows(s):
        for k in range(TOP_K):
            part = rows_scr.at[s, pl.ds(k * per_k, per_k)]
            pltpu.make_async_copy(part, part, row_sem.at[s]).wait()
        out = x_ref[...]
        for k in range(TOP_K):
            rows = rows_scr[s, k * per_k:(k + 1) * per_k].reshape(tm, x_ref.shape[1])
            out = out + wt_ref[:, k:k + 1] * rows
        o_ref[...] = out

    @pl.when(i == 0)
    def _():
        idx_copy(0, 0).start()

    for s in range(2):
        @pl.when((i < n) & (i % 2 == s))
        def _():
            idx_copy(i, s).wait()

            @pl.when(i + 1 < n)
            def _():
                idx_copy(i + 1, 1 - s).start()

            gather_rows(s)

    for s in range(2):
        @pl.when((i >= 1) & ((i - 1) % 2 == s))
        def _():
            combine_rows(s)


def _combine(ys, dest_tiles, x, wts_t, *, tm):
    t, d = x.shape
    prev = lambda i: (jnp.maximum(i - 1, 0), 0)
    return pl.pallas_call(
        functools.partial(_combine_kernel, tm=tm),
        grid=(t // tm + 1,),
        in_specs=[
            pl.BlockSpec(memory_space=pl.ANY),
            pl.BlockSpec(memory_space=pl.ANY),
            pl.BlockSpec((tm, d), prev),
            pl.BlockSpec((tm, TOP_K), prev),
        ],
        out_specs=pl.BlockSpec((tm, d), prev),
        out_shape=jax.ShapeDtypeStruct((t, d), F32),
        scratch_shapes=[
            pltpu.SMEM((2 * TOP_K * tm,), I32),
            pltpu.VMEM((2, TOP_K * tm // SUBLANES, SUBLANES, d), F32),
            pltpu.SemaphoreType.DMA((2,)),
            pltpu.SemaphoreType.DMA((2,)),
        ],
        compiler_params=_params(("arbitrary",)),
        name="moe_combine",
    )(dest_tiles, ys, x, wts_t)


def _tiles(batch, seq):
    t = batch * seq
    tm = min(512, t)
    return dict(
        tm=tm,
        tn=256,
        tq=min(512, seq),
        tk=min(512, seq),
        chunk=min(256, seq),
        tmd=min(512, t),
        tr=min(512, max(8, TOP_K * t // N_EXPERTS)),
    )


def _moe(o, w_out, x, moe_gain, router_w, router_b, w1_layers, b1, w2_layers, b2, layer, *, cfg):
    t, d = x.shape
    tm, tmd, tr = cfg["tm"], cfg["tmd"], cfg["tr"]
    rw_t = router_w.T
    rw_hi = rw_t.astype(BF16)
    rw_lo = (rw_t - rw_hi.astype(F32)).astype(BF16)
    x, hm, idx, wts, rank, cnt = _out_router(o, w_out.astype(BF16), x, moe_gain, rw_hi, rw_lo, router_b, tm=tm)

    counts = cnt[:, 0]
    padded = ((counts + tr - 1) // tr) * tr
    ends = jnp.cumsum(padded).astype(I32)
    offs = ends - padded
    experts = jnp.arange(N_EXPERTS, dtype=I32)[:, None, None]
    dest = rank + jnp.sum(jnp.where(idx[None] == experts, offs[:, None, None], 0), axis=0)
    dest_tiles = dest.reshape(TOP_K, t // tmd, tmd).transpose(1, 0, 2).reshape(t // tmd, TOP_K * tmd)
    n_rows = TOP_K * t + N_EXPERTS * tr
    tile_start = jnp.arange(n_rows // tr, dtype=I32) * tr
    tile_expert = jnp.minimum(jnp.sum((tile_start[:, None] >= ends[None, :]).astype(I32), axis=1), N_EXPERTS - 1)
    tile_valid = (tile_start < ends[-1]).astype(I32)

    xs = _dispatch(hm, dest_tiles, ends, padded.astype(I32), tm=tmd, tr=tr, n_rows=n_rows)
    ff = w2_layers.shape[2]
    w1g, w1l = _split_w1(w1_layers, layer)
    ys = _experts(xs, tile_expert, tile_valid, w1g, w1l,
                  b1[:, 0::2].reshape(N_EXPERTS, 1, ff), b1[:, 1::2].reshape(N_EXPERTS, 1, ff),
                  w2_layers, layer, b2.reshape(N_EXPERTS, 1, d), tr=tr)
    return _combine(ys, dest_tiles, x, wts.T, tm=tmd)


def kernel(x, diff_norm, diff_w_in, diff_w_out, diff_lambda_q1, diff_lambda_k1, diff_lambda_q2, diff_lambda_k2, diff_q_norm, diff_k_norm, diff_subln, ret_norm, ret_w_in, ret_w_out, ret_log_decay_fwd, ret_log_decay_bwd, moe_norm, moe_router_w, moe_router_b, moe_w1, moe_b1, moe_w2, moe_b2):
    batch, seq, d = x.shape
    depth = moe_norm.shape[0]
    cfg = _tiles(batch, seq)
    x = x.reshape(batch * seq, d)
    slopes = jnp.exp2(-8.0 * jnp.arange(1, DIFF_HEADS + 1, dtype=F32) / DIFF_HEADS)
    for i in range(depth):
        j = i // N_MIXERS
        if i % N_MIXERS == 0:
            lambda_init = _lambda_init(i)
            lam = (jnp.exp(jnp.sum(diff_lambda_q1[j] * diff_lambda_k1[j]))
                   - jnp.exp(jnp.sum(diff_lambda_q2[j] * diff_lambda_k2[j])) + lambda_init)
            reps = d // DIFF_HEAD_DIM
            qk_gain = jnp.concatenate([jnp.tile(diff_q_norm[j], reps) * (DIFF_HEAD_DIM ** -0.5 * LOG2E),
                                       jnp.tile(diff_k_norm[j], reps)]).reshape(1, 2 * d)
            qk, vt = _in_proj(x, diff_norm[j], diff_w_in[j].astype(BF16), tm=cfg["tm"], tn=cfg["tn"],
                              qk_gain_row=qk_gain, values_layout=(batch, seq))
            o = _diff_attention(qk, vt, slopes, lam, diff_subln[j], lambda_init,
                                batch=batch, seq=seq, tq=cfg["tq"], tk=cfg["tk"])
            w_out = diff_w_out[j]
        else:
            dk = d // RET_HEADS
            col_scale = jnp.ones((ret_w_in.shape[2],), F32).at[d:2 * d].set(dk ** -0.5)
            proj, = _in_proj(x, ret_norm[j], (ret_w_in[j] * col_scale).astype(BF16), tm=cfg["tm"], tn=cfg["tn"])
            o = _retention(proj, ret_log_decay_fwd[j], ret_log_decay_bwd[j],
                           batch=batch, seq=seq, d_model=d, chunk=cfg["chunk"])
            w_out = ret_w_out[j]
        x = _moe(o, w_out, x, moe_norm[i], moe_router_w[i], moe_router_b[i],
                 moe_w1, moe_b1[i], moe_w2, moe_b2[i], i, cfg=cfg)
    return x.reshape(batch, seq, d)
```
